```python
import jax, jax.numpy as jnp
from jax import lax
import numpy as np

D_MODEL = 1024
BATCH = 32
SEQ = 256
DEPTH = 4
DEC_BATCH = 8
DEC_SEQ = 2048
PAST_LEN = 256

GRID_W = 64
N_EVEN = (DEPTH + 1) // 2
N_ODD = DEPTH // 2
D_MIX = D_MODEL
MIX_W = D_MIX // 2
GLA_HEADS = 4
GLA_DK = MIX_W // (2 * GLA_HEADS)
GLA_DV = MIX_W // GLA_HEADS
GLA_RANK = 16
GLA_TAU = 16.0
HG_HEADS = 4
HG_DK = 128
HG_DV = MIX_W // HG_HEADS
S5_WIDTH = MIX_W
S5_GS = 16
S5_G = S5_WIDTH // S5_GS
S5_P = 64
DN_HEADS = 4
DN_DK = 128
DN_DV = MIX_W // DN_HEADS
DN_CONV = 5
LIN_CHUNK = 32
DN_CHUNK = 64
PK_HEADS = 8
PK_KEYS = 128
PK_EXPERTS = PK_KEYS * PK_KEYS
PK_DQ = 256
PK_TOPK = 16
PK_BLOCK = 128
NORM_EPS = 1e-6
GATE_FLOOR = 1e-30
F32 = jnp.float32

GLA_QK = GLA_HEADS * GLA_DK
HG_K = HG_HEADS * HG_DK
DN_QKV = 2 * DN_HEADS * DN_DK + DN_HEADS * DN_DV
EVEN_SPLITS = (GLA_QK, GLA_QK, MIX_W, GLA_RANK, GLA_RANK, MIX_W, HG_K, HG_K, HG_K, MIX_W, MIX_W)
ODD_SPLITS = (S5_WIDTH, DN_QKV, DN_HEADS, DN_HEADS, DN_HEADS, DN_HEADS, MIX_W)
EVEN_IN = sum(EVEN_SPLITS)
ODD_IN = sum(ODD_SPLITS)

kernel_name = "hybrid_gla_hgrn2_s5_gdn_peer_step"


def rms_norm(x, g):
    xf = x.astype(F32)
    y = xf * lax.rsqrt(jnp.mean(xf * xf, axis=-1, keepdims=True) + NORM_EPS)
    return (y * g.astype(F32)).astype(x.dtype)


def head_norm(o, g):
    o = o * lax.rsqrt(jnp.mean(o * o, axis=-1, keepdims=True) + NORM_EPS)
    o = o * g.astype(F32)
    return o.reshape(o.shape[0], o.shape[1], -1)


def l2n(t):
    return t * lax.rsqrt(jnp.sum(t * t, axis=-1, keepdims=True) + NORM_EPS)


def split_cols(t, sizes):
    return jnp.split(t, [int(s) for s in np.cumsum(sizes)[:-1]], axis=-1)


def to_heads(t, n):
    return t.reshape(t.shape[0], t.shape[1], n, -1)


def rev(t):
    return jnp.flip(t, axis=1)


def to_chunks(t, c):
    b, l, h, d = t.shape
    return t.reshape(b, l // c, c, h, d).transpose(1, 0, 3, 2, 4)


def from_chunks(t):
    n, b, h, c, d = t.shape
    return t.transpose(1, 0, 3, 2, 4).reshape(b, n * c, h, d)


def masked_exp(mask, diff):
    return jnp.where(mask, jnp.exp(jnp.where(mask, diff, 0.0)), 0.0)


def grid_col_major(t):
    b, n, d = t.shape
    rows = n // GRID_W
    return t.reshape(b, rows, GRID_W, d).transpose(0, 2, 1, 3).reshape(b, n, d)


def grid_row_major(t):
    b, n, d = t.shape
    rows = n // GRID_W
    return t.reshape(b, GRID_W, rows, d).transpose(0, 2, 1, 3).reshape(b, n, d)


def chunk_gla(q, k, v, log_a, s0):
    tri = jnp.tril(jnp.ones((LIN_CHUNK, LIN_CHUNK), dtype=bool))[:, :, None]

    def step(S, blk):
        qc, kc, vc, ac = blk
        b = jnp.cumsum(ac, axis=2)
        diff = b[:, :, :, None, :] - b[:, :, None, :, :]
        dec = masked_exp(tri, diff)
        att = jnp.einsum('bhik,bhjk,bhijk->bhij', qc, kc, dec)
        o = jnp.einsum('bhik,bhkv->bhiv', qc * jnp.exp(b), S) + jnp.einsum('bhij,bhjv->bhiv', att, vc)
        b_last = b[:, :, -1:, :]
        S = S * jnp.exp(b_last)[:, :, 0, :, None] + jnp.einsum('bhjk,bhjv->bhkv', kc * jnp.exp(b_last - b), vc)
        return S, o

    blks = tuple(to_chunks(t, LIN_CHUNK) for t in (q, k, v, log_a))
    S, o = lax.scan(step, s0.astype(F32), blks)
    return from_chunks(o), S


def bidir_gla(q, k_f, k_b, v, la_f, la_b, s0):
    o_f, s_f = chunk_gla(q, k_f, v, la_f, s0[:, 0])
    o_b, s_b = chunk_gla(rev(q), rev(k_b), rev(v), rev(la_b), s0[:, 1])
    return o_f + rev(o_b), jnp.stack([s_f, s_b], axis=1)


def chunk_gated_delta(q, k, v, beta, g, s0):
    C = DN_CHUNK
    incl = jnp.tril(jnp.ones((C, C), dtype=bool))
    strict = jnp.tril(jnp.ones((C, C), dtype=bool), -1)
    eye = jnp.eye(C, dtype=F32)
    nv = v.shape[-1]

    def step(S, blk):
        qc, kc, vc, bc, gc = blk
        gam = jnp.cumsum(gc[..., 0], axis=-1)
        diff = gam[..., :, None] - gam[..., None, :]
        dec = masked_exp(incl, diff)
        kb = kc * bc
        m = jnp.where(strict, jnp.einsum('bhik,bhjk->bhij', kb, kc) * dec, 0.0)
        rhs = jnp.concatenate([vc * bc, kb * jnp.exp(gam)[..., None]], axis=-1)
        sol = lax.linalg.triangular_solve(eye + m, rhs, left_side=True, lower=True, unit_diagonal=True)
        u, w = sol[..., :nv], sol[..., nv:]
        v_new = u - jnp.einsum('bhik,bhkv->bhiv', w, S)
        att = jnp.einsum('bhik,bhjk->bhij', qc, kc) * dec
        o = jnp.einsum('bhik,bhkv->bhiv', qc * jnp.exp(gam)[..., None], S) + jnp.einsum('bhij,bhjv->bhiv', att, v_new)
        S = S * jnp.exp(gam[..., -1])[..., None, None] + jnp.einsum(
            'bhjk,bhjv->bhkv', kc * jnp.exp(gam[..., -1:] - gam)[..., None], v_new)
        return S, o

    blks = tuple(to_chunks(t, C) for t in (q, k, v, beta[..., None], g[..., None]))
    S, o = lax.scan(step, s0.astype(F32), blks)
    return from_chunks(o), S


def centred_dwconv(x, w):
    return lax.conv_general_dilated(
        x, w.astype(x.dtype)[:, None, :], window_strides=(1,),
        padding=[(DN_CONV // 2, DN_CONV // 2)], dimension_numbers=('NWC', 'WIO', 'NWC'),
        feature_group_count=x.shape[-1])


def hgrn_gate(f, lb):
    gate = lb + (1.0 - lb) * jax.nn.sigmoid(f)
    log_f = jnp.log(jnp.maximum(gate, GATE_FLOOR))
    k = (1.0 - lb) * jax.nn.sigmoid(-f)
    return to_heads(k, HG_HEADS), to_heads(log_f, HG_HEADS)


def s5_combine(e1, e2):
    a1r, a1i, b1r, b1i = e1
    a2r, a2i, b2r, b2i = e2
    return (a2r * a1r - a2i * a1i, a2r * a1i + a2i * a1r,
            a2r * b1r - a2i * b1i + b2r, a2r * b1i + a2i * b1r + b2i)


def s5_mix(u, j, P, h0_re, h0_im):
    nb, L, _ = u.shape
    lam_re = P['s5_lam_re'][j].astype(F32)
    lam_im = P['s5_lam_im'][j].astype(F32)
    log_dt = P['s5_log_dt'][j].astype(F32)
    b_re = P['s5_b_re'][j].astype(F32)
    b_im = P['s5_b_im'][j].astype(F32)
    c_re = P['s5_c_re'][j].astype(F32)
    c_im = P['s5_c_im'][j].astype(F32)
    ug = u.reshape(nb, L, S5_G, S5_GS)
    y = P['s5_d'][j].astype(F32) * u
    fin_re, fin_im = [], []
    for d in range(2):
        lr = jnp.minimum(lam_re[d], -1e-4)
        li = lam_im[d]
        dt = jnp.exp(log_dt[d])[:, None]
        mag = jnp.exp(lr * dt)
        ar, ai = mag * jnp.cos(li * dt), mag * jnp.sin(li * dt)
        den = lr * lr + li * li
        nr = ar - 1.0
        cr = (nr * lr + ai * li) / den
        ci = (ai * lr - nr * li) / den
        bbr = cr[..., None] * b_re - ci[..., None] * b_im
        bbi = cr[..., None] * b_im + ci[..., None] * b_re
        ud = ug if d == 0 else rev(ug)
        xr = jnp.einsum('gpc,blgc->blgp', bbr, ud)
        xi = jnp.einsum('gpc,blgc->blgp', bbi, ud)
        h0r, h0i = h0_re[:, d].astype(F32), h0_im[:, d].astype(F32)
        xr = xr.at[:, 0].add(ar * h0r - ai * h0i)
        xi = xi.at[:, 0].add(ar * h0i + ai * h0r)
        A_r = jnp.broadcast_to(ar, xr.shape)
        A_i = jnp.broadcast_to(ai, xr.shape)
        _, _, hr, hi = lax.associative_scan(s5_combine, (A_r, A_i, xr, xi), axis=1)
        fin_re.append(hr[:, -1])
        fin_im.append(hi[:, -1])
        yd = jnp.einsum('gcp,blgp->blgc', c_re[d], hr) - jnp.einsum('gcp,blgp->blgc', c_im[d], hi)
        if d == 1:
            yd = rev(yd)
        y = y + yd.reshape(nb, L, S5_WIDTH)
    z = jax.nn.gelu(y)
    out = z * jax.nn.sigmoid(z @ P['s5_w_glu'][j].astype(F32) + P['s5_b_glu'][j].astype(F32))
    return out, jnp.stack(fin_re, axis=1), jnp.stack(fin_im, axis=1)


def mixer_even(h, e, P, s_gla, s_hg):
    z = (h @ P['w_in_even'][e]).astype(F32)
    gq, gk, gv, lr_f, lr_b, gr, hq, hf_f, hf_b, hi, hgate = split_cols(z, EVEN_SPLITS)
    q = to_heads(gq, GLA_HEADS) * GLA_DK ** -0.5
    k = to_heads(gk, GLA_HEADS)
    v = to_heads(gv, GLA_HEADS)
    wg = P['w_gla_gate'][e].astype(F32)
    bg = P['b_gla_gate'][e].astype(F32)
    la_f = to_heads(jax.nn.log_sigmoid(lr_f @ wg[0] + bg[0]) / GLA_TAU, GLA_HEADS)
    la_b = to_heads(jax.nn.log_sigmoid(lr_b @ wg[1] + bg[1]) / GLA_TAU, GLA_HEADS)
    o_gla, st_gla = bidir_gla(q, k, k, v, la_f, la_b, s_gla)
    gla_out = head_norm(o_gla, P['gla_norm'][e]) * jax.nn.silu(gr)
    lbp = jax.nn.softmax(P['hg_lb'].astype(F32), axis=1)
    lb = (jnp.cumsum(lbp, axis=1) - lbp[:, :1])[:, e]
    kf, lgf = hgrn_gate(hf_f, lb[0])
    kb, lgb = hgrn_gate(hf_b, lb[1])
    o_hg, st_hg = bidir_gla(to_heads(hq, HG_HEADS), kf, kb, to_heads(hi, HG_HEADS), lgf, lgb, s_hg)
    hg_out = head_norm(o_hg, P['hg_norm'][e]) * jax.nn.silu(hgate)
    return jnp.concatenate([gla_out, hg_out], axis=-1).astype(h.dtype), st_gla, st_hg


def mixer_odd(h, j, P, s_re, s_im, s_dn):
    z = (h @ P['w_in_odd'][j]).astype(F32)
    u, qkv, a_f, a_b, b_f, b_b, gate = split_cols(z, ODD_SPLITS)
    s5_out, st_re, st_im = s5_mix(u, j, P, s_re, s_im)
    qkv = jax.nn.silu(centred_dwconv(qkv, P['dn_conv'][j]))
    q, k, v = split_cols(qkv, (DN_HEADS * DN_DK, DN_HEADS * DN_DK, DN_HEADS * DN_DV))
    q = l2n(to_heads(q, DN_HEADS)) * DN_DK ** -0.5
    k = l2n(to_heads(k, DN_HEADS))
    v = to_heads(v, DN_HEADS)
    a_log = P['dn_a_log'][j].astype(F32)
    dtb = P['dn_dt_bias'][j].astype(F32)
    g_f = -jnp.exp(a_log[0]) * jax.nn.softplus(a_f + dtb[0])
    g_b = -jnp.exp(a_log[1]) * jax.nn.softplus(a_b + dtb[1])
    o_f, sd_f = chunk_gated_delta(q, k, v, jax.nn.sigmoid(b_f), g_f, s_dn[:, 0])
    o_b, sd_b = chunk_gated_delta(rev(q), rev(k), rev(v), rev(jax.nn.sigmoid(b_b)), rev(g_b), s_dn[:, 1])
    dn_out = head_norm(o_f + rev(o_b), P['dn_norm'][j]) * jax.nn.silu(gate)
    out = jnp.concatenate([s5_out, dn_out], axis=-1).astype(h.dtype)
    return out, st_re, st_im, jnp.stack([sd_f, sd_b], axis=1)


def peer(h, w_q, keys, u_tab, v_tab):
    nb, L, D = h.shape
    T = nb * L
    xt = h.reshape(T, D)
    q = (xt @ w_q).astype(F32).reshape(T, PK_HEADS, 2, PK_DQ // 2)
    s = jnp.einsum('thpd,hpnd->thpn', q, keys.astype(F32))
    sv, si = lax.top_k(s, PK_TOPK)
    cand = (sv[:, :, 0, :, None] + sv[:, :, 1, None, :]).reshape(T, PK_HEADS, PK_TOPK * PK_TOPK)
    cidx = (si[:, :, 0, :, None] * PK_KEYS + si[:, :, 1, None, :]).reshape(T, PK_HEADS, PK_TOPK * PK_TOPK)
    best, pos = lax.top_k(cand, PK_TOPK)
    idx = jnp.take_along_axis(cidx, pos, axis=-1)
    gate = jax.nn.softmax(best, axis=-1)
    nblk = T // PK_BLOCK

    def block(args):
        xb, ib, gb = args
        act = jax.nn.gelu(jnp.einsum('td,thkd->thk', xb.astype(F32), u_tab[ib].astype(F32)))
        return jnp.einsum('thk,thkd->td', gb * act, v_tab[ib].astype(F32))

    y = lax.map(block, (xt.reshape(nblk, PK_BLOCK, D),
                        idx.reshape(nblk, PK_BLOCK, PK_HEADS, PK_TOPK),
                        gate.reshape(nblk, PK_BLOCK, PK_HEADS, PK_TOPK)))
    return y.reshape(nb, L, D).astype(h.dtype)


def trunk(x, cond, st_gla, st_hg, st_re, st_im, st_dn, latent, P):
    cs = jax.nn.silu(cond.astype(F32))
    out_g, out_h, out_re, out_im, out_dn = [], [], [], [], []
    for l in range(DEPTH):
        mod = cs @ P['w_ada'][l].astype(F32) + P['b_ada'][l].astype(F32)
        sh1, sc1, g1, sh2, sc2, g2 = [t[:, None, :].astype(x.dtype) for t in jnp.split(mod, 6, axis=-1)]
        h = rms_norm(x, P['norm1'][l]) * (1 + sc1) + sh1
        if l % 2 == 0:
            e = l // 2
            m, sg, shg = mixer_even(h, e, P, st_gla[:, e], st_hg[:, e])
            out_g.append(sg)
            out_h.append(shg)
        else:
            j = l // 2
            if latent:
                h = grid_col_major(h)
            m, sre, sim, sdn = mixer_odd(h, j, P, st_re[:, j], st_im[:, j], st_dn[:, j])
            if latent:
                m = grid_row_major(m)
            out_re.append(sre)
            out_im.append(sim)
            out_dn.append(sdn)
        x = x + g1 * (m @ P['w_out'][l]).astype(x.dtype)
        h = rms_norm(x, P['norm2'][l]) * (1 + sc2) + sh2
        x = x + g2 * peer(h, P['pk_w_q'][l], P['pk_keys'][l], P['pk_u'][l], P['pk_v'][l])
    y = rms_norm(x, P['final_norm'])
    return (y, jnp.stack(out_g, axis=1), jnp.stack(out_h, axis=1), jnp.stack(out_re, axis=1),
            jnp.stack(out_im, axis=1), jnp.stack(out_dn, axis=1))


def setup_inputs(seed: int = 0) -> dict:
    key = jax.random.key(seed)
    ks = iter(jax.random.split(key, 64))

    def nrm(shape, scale):
        return scale * jax.random.normal(next(ks), shape, F32)

    def gain(shape):
        return 1.0 + 0.02 * jax.random.normal(next(ks), shape, F32)

    def unif(shape, lo, hi):
        return jax.random.uniform(next(ks), shape, F32, minval=lo, maxval=hi)

    n_idx = jnp.arange(S5_P, dtype=F32)
    dn_dt = jnp.exp(unif((N_ODD, 2, DN_HEADS), float(np.log(1e-3)), float(np.log(1e-1))))
    inp = {}
    inp['x_prompt'] = nrm((BATCH, SEQ, D_MODEL), 1.0)
    inp['x_sample'] = nrm((DEC_BATCH, DEC_SEQ, D_MODEL), 1.0)
    inp['state_gla'] = nrm((DEC_BATCH, N_EVEN, 2, GLA_HEADS, GLA_DK, GLA_DV), 0.5)
    inp['state_hgrn'] = nrm((DEC_BATCH, N_EVEN, 2, HG_HEADS, HG_DK, HG_DV), 0.5)
    inp['state_s5_re'] = nrm((DEC_BATCH, N_ODD, 2, S5_G, S5_P), 0.5)
    inp['state_s5_im'] = nrm((DEC_BATCH, N_ODD, 2, S5_G, S5_P), 0.5)
    inp['state_dn'] = nrm((DEC_BATCH, N_ODD, 2, DN_HEADS, DN_DK, DN_DV), 0.1)
    inp['c'] = nrm((DEC_BATCH, D_MODEL), 1.0)
    inp['c_ctx'] = nrm((D_MODEL,), 1.0)
    inp['w_ada'] = nrm((DEPTH, D_MODEL, 6 * D_MODEL), 0.5 * D_MODEL ** -0.5)
    inp['b_ada'] = nrm((DEPTH, 6 * D_MODEL), 0.02)
    inp['norm1'] = gain((DEPTH, D_MODEL))
    inp['norm2'] = gain((DEPTH, D_MODEL))
    inp['w_in_even'] = nrm((N_EVEN, D_MODEL, EVEN_IN), D_MODEL ** -0.5)
    inp['w_gla_gate'] = nrm((N_EVEN, 2, GLA_RANK, GLA_QK), GLA_RANK ** -0.5)
    inp['b_gla_gate'] = nrm((N_EVEN, 2, GLA_QK), 0.02)
    inp['gla_norm'] = gain((N_EVEN, GLA_DV))
    inp['hg_lb'] = nrm((2, N_EVEN, HG_K), 0.1)
    inp['hg_norm'] = gain((N_EVEN, HG_DV))
    inp['w_in_odd'] = nrm((N_ODD, D_MODEL, ODD_IN), D_MODEL ** -0.5)
    inp['s5_lam_re'] = -0.5 + nrm((N_ODD, 2, S5_G, S5_P), 0.01)
    inp['s5_lam_im'] = jnp.pi * n_idx + nrm((N_ODD, 2, S5_G, S5_P), 0.01)
    inp['s5_log_dt'] = unif((N_ODD, 2, S5_G), float(np.log(1e-3)), float(np.log(1e-1)))
    inp['s5_b_re'] = nrm((N_ODD, S5_G, S5_P, S5_GS), (2 * S5_GS) ** -0.5)
    inp['s5_b_im'] = nrm((N_ODD, S5_G, S5_P, S5_GS), (2 * S5_GS) ** -0.5)
    inp['s5_c_re'] = nrm((N_ODD, 2, S5_G, S5_GS, S5_P), (2 * S5_P) ** -0.5)
    inp['s5_c_im'] = nrm((N_ODD, 2, S5_G, S5_GS, S5_P), (2 * S5_P) ** -0.5)
    inp['s5_d'] = nrm((N_ODD, S5_WIDTH), 1.0)
    inp['s5_w_glu'] = nrm((N_ODD, S5_WIDTH, S5_WIDTH), S5_WIDTH ** -0.5)
    inp['s5_b_glu'] = nrm((N_ODD, S5_WIDTH), 0.02)
    inp['dn_conv'] = nrm((N_ODD, DN_CONV, DN_QKV), DN_CONV ** -0.5)
    inp['dn_a_log'] = jnp.log(unif((N_ODD, 2, DN_HEADS), 1.0, 16.0))
    inp['dn_dt_bias'] = dn_dt + jnp.log(-jnp.expm1(-dn_dt))
    inp['dn_norm'] = gain((N_ODD, DN_DV))
    inp['w_out'] = nrm((DEPTH, D_MIX, D_MODEL), D_MIX ** -0.5)
    inp['pk_w_q'] = nrm((DEPTH, D_MODEL, PK_HEADS * PK_DQ), D_MODEL ** -0.5)
    inp['pk_keys'] = nrm((DEPTH, PK_HEADS, 2, PK_KEYS, PK_DQ // 2), (PK_DQ // 2) ** -0.5)
    inp['pk_u'] = nrm((DEPTH, PK_EXPERTS, D_MODEL), D_MODEL ** -0.5)
    inp['pk_v'] = nrm((DEPTH, PK_EXPERTS, D_MODEL), PK_HEADS ** -0.5)
    inp['final_norm'] = gain((D_MODEL,))
    return inp


def reference(x_prompt, x_sample, state_gla, state_hgrn, state_s5_re, state_s5_im, state_dn, c, c_ctx,
              w_ada, b_ada, norm1, norm2, w_in_even, w_gla_gate, b_gla_gate, gla_norm, hg_lb, hg_norm,
              w_in_odd, s5_lam_re, s5_lam_im, s5_log_dt, s5_b_re, s5_b_im, s5_c_re, s5_c_im, s5_d,
              s5_w_glu, s5_b_glu, dn_conv, dn_a_log, dn_dt_bias, dn_norm, w_out, pk_w_q, pk_keys, pk_u,
              pk_v, final_norm):
    P = dict(w_ada=w_ada, b_ada=b_ada, norm1=norm1, norm2=norm2, w_in_even=w_in_even,
             w_gla_gate=w_gla_gate, b_gla_gate=b_gla_gate, gla_norm=gla_norm, hg_lb=hg_lb,
             hg_norm=hg_norm, w_in_odd=w_in_odd, s5_lam_re=s5_lam_re, s5_lam_im=s5_lam_im,
             s5_log_dt=s5_log_dt, s5_b_re=s5_b_re, s5_b_im=s5_b_im, s5_c_re=s5_c_re, s5_c_im=s5_c_im,
             s5_d=s5_d, s5_w_glu=s5_w_glu, s5_b_glu=s5_b_glu, dn_conv=dn_conv, dn_a_log=dn_a_log,
             dn_dt_bias=dn_dt_bias, dn_norm=dn_norm, w_out=w_out, pk_w_q=pk_w_q, pk_keys=pk_keys,
             pk_u=pk_u, pk_v=pk_v, final_norm=final_norm)
    bp = x_prompt.shape[0]
    z_gla = jnp.zeros((bp,) + state_gla.shape[1:], F32)
    z_hg = jnp.zeros((bp,) + state_hgrn.shape[1:], F32)
    z_re = jnp.zeros((bp,) + state_s5_re.shape[1:], F32)
    z_im = jnp.zeros((bp,) + state_s5_im.shape[1:], F32)
    z_dn = jnp.zeros((bp,) + state_dn.shape[1:], F32)
    y_prompt, new_gla, new_hgrn, new_s5_re, new_s5_im, new_dn = trunk(
        x_prompt, c_ctx[None, :], z_gla, z_hg, z_re, z_im, z_dn, False, P)
    y_sample = trunk(x_sample, c, state_gla, state_hgrn, state_s5_re, state_s5_im, state_dn, True, P)[0]
    return (y_prompt, y_sample, new_gla, new_hgrn, new_s5_re, new_s5_im, new_dn)
```

```python
import functools

import numpy as np
import jax
import jax.numpy as jnp
from jax import lax
from jax.experimental import pallas as pl
from jax.experimental.pallas import tpu as pltpu

F32 = jnp.float32
BF16 = jnp.bfloat16

D_MODEL = 1024
BATCH = 32
SEQ = 256
DEPTH = 4
DEC_BATCH = 8
DEC_SEQ = 2048
GRID_W = 64
N_EVEN = 2
N_ODD = 2
MIX_W = 512
GLA_HEADS = 4
GLA_DK = 64
GLA_DV = 128
GLA_RANK = 16
GLA_TAU = 16.0
HG_HEADS = 4
HG_DK = 128
HG_DV = 128
S5_GS = 16
S5_G = 32
S5_P = 64
DN_HEADS = 4
DN_DK = 128
DN_DV = 128
DN_CONV = 5
PK_HEADS = 8
PK_KEYS = 128
PK_DQ = 256
PK_TOPK = 16
NORM_EPS = 1e-6
GATE_FLOOR = 1e-30

T_PROMPT = BATCH * SEQ
T_SAMPLE = DEC_BATCH * DEC_SEQ
T_ALL = T_PROMPT + T_SAMPLE
N_COND = 16

LANES = 128
SUBLANES = 8
VMEM_LIMIT = 56 * 1024 * 1024

TM = 512


def _layout(pieces):
    idx, col = [], {}
    for name, src, width in pieces:
        assert len(idx) % LANES == 0 and width % LANES == 0
        col[name] = len(idx) // LANES
        src = list(src)
        idx += src + [-1] * (width - len(src))
    return np.asarray(idx, np.int32), col


def _even_layout():
    o = np.cumsum((0, 256, 256, 512, 16, 16, 512, 512, 512, 512, 512, 512))
    rng = lambda i: range(o[i], o[i + 1])
    pieces = []
    for nm, base in (("gq", o[0]), ("gk", o[1])):
        for h in range(GLA_HEADS):
            pieces.append((nm if h == 0 else f"{nm}{h}", range(base + 64 * h, base + 64 * h + 64), LANES))
    pieces += [("gv", rng(2), 512), ("gr", rng(5), 512),
               ("lr", list(rng(3)) + list(rng(4)), LANES),
               ("hq", rng(6), 512), ("hf_f", rng(7), 512), ("hf_b", rng(8), 512),
               ("hi", rng(9), 512), ("hgate", rng(10), 512), ("pad", [], LANES)]
    return _layout(pieces)


def _odd_layout():
    o = np.cumsum((0, 512, 1536, 4, 4, 4, 4, 512))
    rng = lambda i: range(o[i], o[i + 1])
    return _layout([("u", rng(0), 512), ("qkv", rng(1), 1536), ("gate", rng(6), 512),
                    ("ab", range(o[2], o[6]), LANES)])


EVEN_IDX, EVEN_COL = _even_layout()
ODD_IDX, ODD_COL = _odd_layout()


def _relayout_cols(w, idx):
    wz = jnp.concatenate([w, jnp.zeros(w.shape[:-1] + (1,), w.dtype)], axis=-1)
    return jnp.take(wz, jnp.where(idx < 0, w.shape[-1], idx), axis=-1)


def _cparams(sem):
    return pltpu.CompilerParams(dimension_semantics=sem, vmem_limit_bytes=VMEM_LIMIT)


def _cond_row(i, tm=TM):
    n_p = T_PROMPT // tm
    per_seq = DEC_SEQ // tm
    return jnp.where(i < n_p, 0, 1 + (i - n_p) // per_seq)


def _split3(x):
    hi = x.astype(BF16)
    r1 = x - hi.astype(F32)
    mid = r1.astype(BF16)
    lo = (r1 - mid.astype(F32)).astype(BF16)
    return hi, mid, lo


def _dot(a, b):
    return jnp.dot(a, b, preferred_element_type=F32)


def _dot_nt(a, b):
    return lax.dot_general(a, b, (((1,), (1,)), ((), ())), preferred_element_type=F32)


def _dot_tn(a, b):
    return lax.dot_general(a, b, (((0,), (0,)), ((), ())), preferred_element_type=F32)


def _dot_sel(sel_bf16, x):
    hi, mid, lo = _split3(x)
    return _dot(sel_bf16, hi) + _dot(sel_bf16, mid) + _dot(sel_bf16, lo)


def _dot_x_sel(x, sel_bf16):
    hi, mid, lo = _split3(x)
    return _dot(hi, sel_bf16) + _dot(mid, sel_bf16) + _dot(lo, sel_bf16)


def _dot_hi(a, b):
    ah = a.astype(BF16)
    al = (a - ah.astype(F32)).astype(BF16)
    bh = b.astype(BF16)
    bl = (b - bh.astype(F32)).astype(BF16)
    return _dot(ah, bh) + (_dot(ah, bl) + _dot(al, bh))


def _ada_kernel(c_ref, w_ref, b_ref, o_ref):
    c = c_ref[...]
    cs = c * jax.nn.sigmoid(c)
    o_ref[0] = _dot(cs, w_ref[0]) + b_ref[0]


def _ada_call(cond, w_ada, b_ada):
    nt = 6
    return pl.pallas_call(
        _ada_kernel,
        grid=(DEPTH, nt),
        in_specs=[
            pl.BlockSpec((N_COND, D_MODEL), lambda l, j: (0, 0)),
            pl.BlockSpec((1, D_MODEL, D_MODEL), lambda l, j: (l, 0, j)),
            pl.BlockSpec((1, 1, D_MODEL), lambda l, j: (l, 0, j)),
        ],
        out_specs=pl.BlockSpec((1, N_COND, D_MODEL), lambda l, j: (l, 0, j)),
        out_shape=jax.ShapeDtypeStruct((DEPTH, N_COND, 6 * D_MODEL), F32),
        compiler_params=_cparams(("parallel", "parallel")),
        name="ada_mod",
    )(cond, w_ada, b_ada.reshape(DEPTH, 1, 6 * D_MODEL))


def _modulated_norm(x, g, sc, sh):
    ms = jnp.mean(x * x, axis=-1, keepdims=True)
    y = x * lax.rsqrt(ms + NORM_EPS) * g
    return y * (1.0 + sc) + sh


def _in_kernel(x_ref, sh_ref, sc_ref, g_ref, w_ref, o_ref):
    h = _modulated_norm(x_ref[...], g_ref[...], sc_ref[0], sh_ref[0])
    o_ref[...] = _dot(h.astype(BF16), w_ref[...])


def _in_call(x, mod3, g, w, n_col_tiles):
    n = w.shape[1]
    tn = n // n_col_tiles
    return pl.pallas_call(
        _in_kernel,
        grid=(n_col_tiles, x.shape[0] // TM),
        in_specs=[
            pl.BlockSpec((TM, D_MODEL), lambda j, i: (i, 0)),
            pl.BlockSpec((1, 1, D_MODEL), lambda j, i: (_cond_row(i) * 6 + 0, 0, 0)),
            pl.BlockSpec((1, 1, D_MODEL), lambda j, i: (_cond_row(i) * 6 + 1, 0, 0)),
            pl.BlockSpec((1, D_MODEL), lambda j, i: (0, 0)),
            pl.BlockSpec((D_MODEL, tn), lambda j, i: (0, j)),
        ],
        out_specs=pl.BlockSpec((TM, tn), lambda j, i: (i, j)),
        out_shape=jax.ShapeDtypeStruct((x.shape[0], n), F32),
        compiler_params=_cparams(("parallel", "parallel")),
        name="in_proj",
    )(x, mod3, mod3, g, w)


def _out_kernel(x_ref, m_ref, w_ref, g1_ref, sh_ref, sc_ref, g_ref, xo_ref, h_ref):
    x = x_ref[...] + g1_ref[0] * _dot(m_ref[...], w_ref[...])
    xo_ref[...] = x
    h_ref[...] = _modulated_norm(x, g_ref[...], sc_ref[0], sh_ref[0])


def _out_call(x, m, w_out, mod3, g2):
    spec_tok = pl.BlockSpec((TM, D_MODEL), lambda i: (i, 0))

    def mod_spec(k):
        return pl.BlockSpec((1, 1, D_MODEL), lambda i: (_cond_row(i) * 6 + k, 0, 0))

    return pl.pallas_call(
        _out_kernel,
        grid=(x.shape[0] // TM,),
        in_specs=[
            spec_tok,
            spec_tok,
            pl.BlockSpec((D_MODEL, D_MODEL), lambda i: (0, 0)),
            mod_spec(2), mod_spec(3), mod_spec(4),
            pl.BlockSpec((1, D_MODEL), lambda i: (0, 0)),
        ],
        out_specs=[spec_tok, spec_tok],
        out_shape=[jax.ShapeDtypeStruct(x.shape, F32)] * 2,
        compiler_params=_cparams(("parallel",)),
        name="out_proj",
    )(x, m, w_out, mod3, mod3, mod3, g2)


CHUNK = 64
SUB = 16
NSUB = CHUNK // SUB
NEG_BIG = -1e30


def _tri(n, lower, strict=False):
    r = lax.broadcasted_iota(jnp.int32, (n, n), 0)
    c = lax.broadcasted_iota(jnp.int32, (n, n), 1)
    if lower:
        return (c < r) if strict else (c <= r)
    return (c > r) if strict else (c >= r)


def _gla_chunk(q, k, v, la, st, reverse):
    kdim = q.shape[1]
    tri = jnp.where(_tri(CHUNK, not reverse), 1.0, 0.0).astype(BF16)
    b = _dot_sel(tri, la)
    last = 0 if reverse else CHUNK - 1
    b_last = b[last:last + 1, :]
    o_inter = _dot_nt((q * jnp.exp(b)).astype(BF16), st.astype(BF16))
    kd = k * jnp.exp(b_last - b)
    st_new = st * jnp.exp(b_last) + _dot_tn(v.astype(BF16), kd.astype(BF16))

    ones_k = jnp.ones((kdim, LANES), BF16)
    jrow = lax.broadcasted_iota(jnp.int32, (SUB, kdim), 0)
    rsel = lax.broadcasted_iota(jnp.int32, (SUB, SUB * SUB), 0)
    csel = lax.broadcasted_iota(jnp.int32, (SUB, SUB * SUB), 1)
    in_grp = jnp.logical_and(csel >= rsel * SUB, csel < rsel * SUB + SUB)
    sel = jnp.where(in_grp, 1.0, 0.0).astype(BF16)
    outs = []
    for blk in range(NSUB):
        r0 = blk * SUB
        q_i, k_i, b_i, v_i = q[r0:r0 + SUB], k[r0:r0 + SUB], b[r0:r0 + SUB], v[r0:r0 + SUB]
        rows = []
        for i in range(SUB):
            keep = (jrow >= i) if reverse else (jrow <= i)
            e = jnp.exp(jnp.where(keep, b_i[i:i + 1, :] - b_i, NEG_BIG))
            rows.append((q_i[i:i + 1, :] * k_i) * e)
        p = jnp.concatenate(rows, axis=0)
        att = _dot(p.astype(BF16), ones_k)
        zv = att * jnp.concatenate([v_i] * SUB, axis=0)
        o_blk = _dot(sel, zv.astype(BF16))
        if reverse and blk < NSUB - 1:
            r1 = r0 + SUB
            ref = b[r1:r1 + 1, :]
            qt = q_i * jnp.exp(b_i - ref)
            kt = k[r1:] * jnp.exp(ref - b[r1:])
            a_off = _dot_nt(qt.astype(BF16), kt.astype(BF16))
            o_blk = o_blk + _dot(a_off.astype(BF16), v[r1:].astype(BF16))
        if (not reverse) and blk > 0:
            ref = b[r0 - 1:r0, :]
            qt = q_i * jnp.exp(b_i - ref)
            kt = k[:r0] * jnp.exp(ref - b[:r0])
            a_off = _dot_nt(qt.astype(BF16), kt.astype(BF16))
            o_blk = o_blk + _dot(a_off.astype(BF16), v[:r0].astype(BF16))
        outs.append(o_blk)
    return o_inter + jnp.concatenate(outs, axis=0), st_new


def _log_sigmoid(x):
    return -(jnp.maximum(-x, 0.0) + jnp.log1p(jnp.exp(-jnp.abs(x))))


def _head_norm_gate(o, g, gate):
    o = o * lax.rsqrt(jnp.mean(o * o, axis=-1, keepdims=True) + NORM_EPS)
    return (o * g) * (gate * jax.nn.sigmoid(gate))


def _even_kernel(*refs, mode, seq_len, with_state_out, has_s0):
    it = iter(refs)
    if mode == "gla":
        q_ref, k_ref, v_ref, lr_ref, gate_ref, wg_ref, bg_ref, gn_ref = (next(it) for _ in range(8))
    else:
        q_ref, ff_ref, fb_ref, v_ref, gate_ref, lb_ref, gn_ref = (next(it) for _ in range(7))
    s0_ref = next(it) if has_s0 else None
    m_ref = next(it)
    so_ref = next(it) if with_state_out else None
    of_scr, ob_scr, st_scr = next(it), next(it), next(it)

    n_chunks = seq_len // CHUNK
    for d in range(2):
        if has_s0:
            st_scr[d] = s0_ref[0, d, 0].T
        else:
            st_scr[d] = jnp.zeros((LANES, LANES), F32)

    def qkv_la(c, d):
        rows = pl.ds(pl.multiple_of(c * CHUNK, CHUNK), CHUNK)
        if mode == "gla":
            q = q_ref[rows, :] * (GLA_DK ** -0.5)
            k = k_ref[rows, :]
            pre = _dot_hi(lr_ref[rows, :], wg_ref[0, d]) + bg_ref[0, d]
            la = _log_sigmoid(pre) / GLA_TAU
        else:
            q = q_ref[rows, :]
            f = (ff_ref, fb_ref)[d][rows, :]
            lb = lb_ref[d:d + 1, :]
            gate = lb + (1.0 - lb) * jax.nn.sigmoid(f)
            la = jnp.log(jnp.maximum(gate, GATE_FLOOR))
            k = (1.0 - lb) * jax.nn.sigmoid(-f)
        return rows, q, k, v_ref[rows, :], la

    def body(c, carry):
        rows, q, k, v, la = qkv_la(c, 0)
        o, st = _gla_chunk(q, k, v, la, st_scr[0], False)
        of_scr[rows, :] = o
        st_scr[0] = st
        rows, q, k, v, la = qkv_la(n_chunks - 1 - c, 1)
        o, st = _gla_chunk(q, k, v, la, st_scr[1], True)
        ob_scr[rows, :] = o
        st_scr[1] = st
        return carry

    lax.fori_loop(0, n_chunks, body, 0)
    m_ref[...] = _head_norm_gate(of_scr[...] + ob_scr[...], gn_ref[...], gate_ref[...]).astype(m_ref.dtype)
    if with_state_out:
        for d in range(2):
            so_ref[0, d, 0] = st_scr[d].T


def _even_params(w_gate, b_gate, gla_norm, hg_lb, hg_norm, e):
    wg = jnp.zeros((GLA_HEADS, 2, LANES, LANES), F32)
    bg = jnp.zeros((GLA_HEADS, 2, 1, LANES), F32)
    for h in range(GLA_HEADS):
        for d in range(2):
            wg = wg.at[h, d, d * GLA_RANK:(d + 1) * GLA_RANK, :GLA_DK].set(
                w_gate[d, :, h * GLA_DK:(h + 1) * GLA_DK].astype(F32))
            bg = bg.at[h, d, 0, :GLA_DK].set(b_gate[d, h * GLA_DK:(h + 1) * GLA_DK].astype(F32))
    lbp = jax.nn.softmax(hg_lb.astype(F32), axis=1)
    lb = (jnp.cumsum(lbp, axis=1) - lbp[:, :1])[:, e]
    return {"gla": (wg, bg, gla_norm.astype(F32).reshape(1, LANES)),
            "hgrn": (lb, hg_norm.astype(F32).reshape(1, LANES))}


def _even_call(z, mode, col, params, s0, seq_len, n_seq, row0, with_state_out):
    heads = 4
    blk0 = row0 // seq_len

    def tok(name):
        c0 = col[name]
        return pl.BlockSpec((seq_len, LANES), lambda s, h: (blk0 + s, c0 + h))

    if mode == "gla":
        wg, bg, gn = params
        in_specs = [tok("gq"), tok("gk"), tok("gv"),
                    pl.BlockSpec((seq_len, LANES), lambda s, h: (blk0 + s, col["lr"])),
                    tok("gr"),
                    pl.BlockSpec((1, 2, LANES, LANES), lambda s, h: (h, 0, 0, 0)),
                    pl.BlockSpec((1, 2, 1, LANES), lambda s, h: (h, 0, 0, 0)),
                    pl.BlockSpec((1, LANES), lambda s, h: (0, 0))]
        args = [z, z, z, z, z, wg, bg, gn]
    else:
        lb, gn = params
        in_specs = [tok("hq"), tok("hf_f"), tok("hf_b"), tok("hi"), tok("hgate"),
                    pl.BlockSpec((2, LANES), lambda s, h: (0, h)),
                    pl.BlockSpec((1, LANES), lambda s, h: (0, 0))]
        args = [z, z, z, z, z, lb, gn]
    st_spec = pl.BlockSpec((1, 2, 1, LANES, LANES), lambda s, h: (s, 0, h, 0, 0))
    has_s0 = s0 is not None
    if has_s0:
        in_specs.append(st_spec)
        args.append(s0)
    out_specs = [pl.BlockSpec((seq_len, LANES), lambda s, h: (s, h))]
    out_shape = [jax.ShapeDtypeStruct((n_seq * seq_len, heads * LANES), BF16)]
    if with_state_out:
        out_specs.append(st_spec)
        out_shape.append(jax.ShapeDtypeStruct((n_seq, 2, heads, LANES, LANES), F32))
    res = pl.pallas_call(
        functools.partial(_even_kernel, mode=mode, seq_len=seq_len,
                          with_state_out=with_state_out, has_s0=has_s0),
        grid=(n_seq, heads),
        in_specs=in_specs,
        out_specs=out_specs,
        out_shape=out_shape,
        scratch_shapes=[pltpu.VMEM((seq_len, LANES), F32), pltpu.VMEM((seq_len, LANES), F32),
                        pltpu.VMEM((2, LANES, LANES), F32)],
        compiler_params=_cparams(("parallel", "parallel")),
        name=f"mix_{mode}_{seq_len}",
    )(*args)
    return res


CONV_PAD = SUBLANES


def _conv_kernel(x_ref, w_ref, o_ref, pad_scr, *, seq_len):
    cb = pl.program_id(1)
    zeros = jnp.zeros((CONV_PAD, LANES), F32)
    pad_scr[0:CONV_PAD, :] = zeros
    pad_scr[CONV_PAD + seq_len:2 * CONV_PAD + seq_len, :] = zeros
    pad_scr[CONV_PAD:CONV_PAD + seq_len, :] = x_ref[...]
    acc = jnp.zeros((seq_len, LANES), F32)
    for w in range(DN_CONV):
        acc = acc + pad_scr[pl.ds(CONV_PAD + w - DN_CONV // 2, seq_len), :] * w_ref[w:w + 1, :]
    y = acc * jax.nn.sigmoid(acc)
    yn = y * lax.rsqrt(jnp.sum(y * y, axis=-1, keepdims=True) + NORM_EPS)
    is_q = cb < DN_HEADS
    is_qk = cb < 2 * DN_HEADS
    scale = jnp.where(is_q, DN_DK ** -0.5, 1.0)
    o_ref[...] = jnp.where(is_qk, yn * scale, y)


def _conv_call(z, col0, w, seq_len, n_seq, row0):
    blk0 = row0 // seq_len
    nblk = 3 * DN_HEADS
    return pl.pallas_call(
        functools.partial(_conv_kernel, seq_len=seq_len),
        grid=(n_seq, nblk),
        in_specs=[pl.BlockSpec((seq_len, LANES), lambda s, c: (blk0 + s, col0 + c)),
                  pl.BlockSpec((DN_CONV, LANES), lambda s, c: (0, c))],
        out_specs=pl.BlockSpec((seq_len, LANES), lambda s, c: (s, c)),
        out_shape=jax.ShapeDtypeStruct((n_seq * seq_len, nblk * LANES), F32),
        scratch_shapes=[pltpu.VMEM((seq_len + 2 * CONV_PAD, LANES), F32)],
        compiler_params=_cparams(("parallel", "parallel")),
        name=f"dn_conv_{seq_len}",
    )(z, w)


def _softplus(x):
    return jnp.maximum(x, 0.0) + jnp.log1p(jnp.exp(-jnp.abs(x)))


def _dn_chunk(q, k, v, g, beta, s, reverse):
    c = CHUNK
    lower = not reverse
    tri = jnp.where(_tri(c, lower), 1.0, 0.0).astype(BF16)
    gam = _dot_sel(tri, g)
    ones_c = jnp.ones((c, c), BF16)
    gam_row = _dot_sel(ones_c, jnp.where(_tri(c, reverse), g[:, :c], 0.0))
    incl = _tri(c, lower)
    strict = _tri(c, lower, strict=True)
    diff = gam[:, :c] - gam_row
    dec = jnp.where(incl, jnp.exp(jnp.where(incl, diff, 0.0)), 0.0)
    kb = k * beta
    k16 = k.astype(BF16)
    m = jnp.where(strict, _dot_nt(kb.astype(BF16), k16) * dec, 0.0)
    eg = jnp.exp(gam)
    x = jnp.concatenate([v * beta, kb * eg], axis=1)
    p = -m
    n_fac = int(np.log2(c))
    for j in range(n_fac):
        x = x + _dot_hi(p, x)
        if j + 1 < n_fac:
            p = _dot_hi(p, p)
    u, w = x[:, :LANES], x[:, LANES:]
    s16 = s.astype(BF16)
    v_new = u - _dot(w.astype(BF16), s16)
    att = _dot_nt(q.astype(BF16), k16) * dec
    o = _dot((q * eg).astype(BF16), s16) + _dot(att.astype(BF16), v_new.astype(BF16))
    last = 0 if reverse else c - 1
    gl = gam[last:last + 1, :]
    s_new = s * jnp.exp(gl) + _dot_tn((k * jnp.exp(gl - gam)).astype(BF16), v_new.astype(BF16))
    return o, s_new


def _dn_kernel(*refs, seq_len, with_state_out, has_s0):
    it = iter(refs)
    q_ref, k_ref, v_ref, ab_ref, gate_ref, alog_ref, dtb_ref, gn_ref = (next(it) for _ in range(8))
    s0_ref = next(it) if has_s0 else None
    m_ref = next(it)
    so_ref = next(it) if with_state_out else None
    of_scr, ob_scr, st_scr = next(it), next(it), next(it)
    h = pl.program_id(1)
    n_chunks = seq_len // CHUNK
    for d in range(2):
        st_scr[d] = s0_ref[0, d, 0] if has_s0 else jnp.zeros((LANES, LANES), F32)
    srow = lax.broadcasted_iota(jnp.int32, (LANES, LANES), 0)

    def run(c, d):
        rows = pl.ds(pl.multiple_of(c * CHUNK, CHUNK), CHUNK)
        ab = ab_ref[rows, :]
        sel_a = jnp.where(srow == DN_HEADS * d + h, 1.0, 0.0).astype(BF16)
        sel_b = jnp.where(srow == 2 * DN_HEADS + DN_HEADS * d + h, 1.0, 0.0).astype(BF16)
        a = _dot_x_sel(ab, sel_a)
        beta = jax.nn.sigmoid(_dot_x_sel(ab, sel_b))
        g = -jnp.exp(alog_ref[0, d]) * _softplus(a + dtb_ref[0, d])
        o, s_new = _dn_chunk(q_ref[rows, :], k_ref[rows, :], v_ref[rows, :], g, beta, st_scr[d], d == 1)
        (of_scr, ob_scr)[d][rows, :] = o
        st_scr[d] = s_new

    def body(c, carry):
        run(c, 0)
        run(n_chunks - 1 - c, 1)
        return carry

    lax.fori_loop(0, n_chunks, body, 0)
    m_ref[...] = _head_norm_gate(of_scr[...] + ob_scr[...], gn_ref[...], gate_ref[...]).astype(m_ref.dtype)
    if with_state_out:
        for d in range(2):
            so_ref[0, d, 0] = st_scr[d]


S5_TC = 128
S5_CB = 4
S5_SW = S5_G * S5_P // S5_CB


def _s5_params(lam_re, lam_im, log_dt, b_re, b_im, c_re, c_im):
    gpb = S5_G // S5_CB
    eye = jnp.eye(gpb, dtype=F32)
    n = jnp.arange(1, S5_TC + 1, dtype=F32)[:, None, None]
    bbs, ccs, aps = [], [], []
    for d in range(2):
        lr = jnp.minimum(lam_re[d].astype(F32), -1e-4)
        li = lam_im[d].astype(F32)
        dt = jnp.exp(log_dt[d].astype(F32))[:, None]
        mag = jnp.exp(lr * dt)
        ar, ai = mag * jnp.cos(li * dt), mag * jnp.sin(li * dt)
        den = lr * lr + li * li
        nr = ar - 1.0
        cr = (nr * lr + ai * li) / den
        ci = (ai * lr - nr * li) / den
        bbr = cr[..., None] * b_re - ci[..., None] * b_im
        bbi = cr[..., None] * b_im + ci[..., None] * b_re
        blk = lambda t: jnp.einsum('bgpc,gh->bgchp', t.reshape(S5_CB, gpb, S5_P, S5_GS), eye).reshape(
            S5_CB, gpb * S5_GS, gpb * S5_P)
        bbs.append(jnp.stack([blk(bbr), blk(bbi)]))
        cblk = lambda t: jnp.einsum('bgcp,gh->bgphc', t.reshape(S5_CB, gpb, S5_GS, S5_P), eye).reshape(
            S5_CB, gpb * S5_P, gpb * S5_GS)
        ccs.append(jnp.stack([cblk(c_re[d].astype(F32)), cblk(c_im[d].astype(F32))]))
        pm = jnp.exp(n * (lr * dt)[None])
        pr = (pm * jnp.cos(n * (li * dt)[None])).reshape(S5_TC, -1)
        pi = (pm * jnp.sin(n * (li * dt)[None])).reshape(S5_TC, -1)
        if d == 1:
            pr, pi = pr[::-1], pi[::-1]
        aps.append(jnp.stack([pr, pi]))
    bb = jnp.stack(bbs, axis=2).astype(BF16)
    cc = jnp.stack(ccs, axis=2).astype(BF16)
    apow = jnp.stack(aps, axis=1)
    return bb, cc, apow


def _s5_scan(xr, xi, pr, pi, reverse):
    tc = xr.shape[0]
    row = lax.broadcasted_iota(jnp.int32, xr.shape, 0)
    s = 1
    while s < tc:
        idx = tc - s if reverse else s - 1
        a_r, a_i = pr[idx:idx + 1, :], pi[idx:idx + 1, :]
        shift = tc - s if reverse else s
        keep = (row < tc - s) if reverse else (row >= s)
        sr = jnp.where(keep, pltpu.roll(xr, shift, 0), 0.0)
        si = jnp.where(keep, pltpu.roll(xi, shift, 0), 0.0)
        xr, xi = xr + (a_r * sr - a_i * si), xi + (a_r * si + a_i * sr)
        s *= 2
    return xr, xi


def _s5_kernel(*refs, seq_len, has_h0):
    it = iter(refs)
    u_ref, bb_ref, cc_ref, ap_ref, dsk_ref = (next(it) for _ in range(5))
    h0r_ref, h0i_ref = (next(it), next(it)) if has_h0 else (None, None)
    y_ref, fr_ref, fi_ref = next(it), next(it), next(it)
    yf_scr, yb_scr = next(it), next(it)
    n_chunks = seq_len // S5_TC

    def run(c, d, carry):
        cr, ci = carry
        rows = pl.ds(pl.multiple_of(c * S5_TC, S5_TC), S5_TC)
        ub = u_ref[rows, :].astype(BF16)
        pr, pi = ap_ref[0, d], ap_ref[1, d]
        xr, xi = _dot(ub, bb_ref[0, 0, d]), _dot(ub, bb_ref[1, 0, d])
        hr, hi = _s5_scan(xr, xi, pr, pi, d == 1)
        hr, hi = hr + (pr * cr - pi * ci), hi + (pr * ci + pi * cr)
        (yf_scr, yb_scr)[d][rows, :] = (_dot(hr.astype(BF16), cc_ref[0, 0, d])
                                         - _dot(hi.astype(BF16), cc_ref[1, 0, d]))
        last = 0 if d == 1 else S5_TC - 1
        return hr[last:last + 1, :], hi[last:last + 1, :]

    def body(c, carry):
        cf, cb = carry
        return run(c, 0, cf), run(n_chunks - 1 - c, 1, cb)

    if has_h0:
        init = tuple((h0r_ref[0, d:d + 1, :], h0i_ref[0, d:d + 1, :]) for d in range(2))
    else:
        z = jnp.zeros((1, S5_SW), F32)
        init = ((z, z), (z, z))
    fin = lax.fori_loop(0, n_chunks, body, init)
    y_ref[...] = dsk_ref[...] * u_ref[...] + (yf_scr[...] + yb_scr[...])
    for d in range(2):
        fr_ref[0, d:d + 1, :] = fin[d][0]
        fi_ref[0, d:d + 1, :] = fin[d][1]


def _s5_call(z, col0, bb, cc, apow, dsk, h0r, h0i, seq_len, n_seq, row0):
    blk0 = row0 // seq_len
    has_h0 = h0r is not None
    st_spec = pl.BlockSpec((1, 2, S5_SW), lambda s, c: (s, 0, c))
    in_specs = [pl.BlockSpec((seq_len, LANES), lambda s, c: (blk0 + s, col0 + c)),
                pl.BlockSpec((2, 1, 2, LANES, S5_SW), lambda s, c: (0, c, 0, 0, 0)),
                pl.BlockSpec((2, 1, 2, S5_SW, LANES), lambda s, c: (0, c, 0, 0, 0)),
                pl.BlockSpec((2, 2, S5_TC, S5_SW), lambda s, c: (0, 0, 0, c)),
                pl.BlockSpec((1, LANES), lambda s, c: (0, c))]
    args = [z, bb, cc, apow, dsk]
    if has_h0:
        in_specs += [st_spec, st_spec]
        args += [h0r, h0i]
    st_shape = jax.ShapeDtypeStruct((n_seq, 2, S5_G * S5_P), F32)
    return pl.pallas_call(
        functools.partial(_s5_kernel, seq_len=seq_len, has_h0=has_h0),
        grid=(n_seq, S5_CB),
        in_specs=in_specs,
        out_specs=[pl.BlockSpec((seq_len, LANES), lambda s, c: (s, c)), st_spec, st_spec],
        out_shape=[jax.ShapeDtypeStruct((n_seq * seq_len, MIX_W), F32), st_shape, st_shape],
        scratch_shapes=[pltpu.VMEM((seq_len, LANES), F32), pltpu.VMEM((seq_len, LANES), F32)],
        compiler_params=_cparams(("parallel", "parallel")),
        name=f"mix_s5_{seq_len}",
    )(*args)


def _glu_kernel(y_ref, w_ref, b_ref, o_ref):
    zz = jax.nn.gelu(y_ref[...])
    o_ref[...] = (zz * jax.nn.sigmoid(_dot(zz.astype(BF16), w_ref[...]) + b_ref[...])).astype(o_ref.dtype)


def _glu_call(y, w, b):
    t = y.shape[0]
    return pl.pallas_call(
        _glu_kernel,
        grid=(t // TM,),
        in_specs=[pl.BlockSpec((TM, MIX_W), lambda i: (i, 0)),
                  pl.BlockSpec((MIX_W, MIX_W), lambda i: (0, 0)),
                  pl.BlockSpec((1, MIX_W), lambda i: (0, 0))],
        out_specs=pl.BlockSpec((TM, MIX_W), lambda i: (i, 0)),
        out_shape=jax.ShapeDtypeStruct((t, MIX_W), BF16),
        compiler_params=_cparams(("parallel",)),
        name="s5_glu",
    )(y, w, b)


PK_HP = 2 * PK_HEADS
PK_SIDE = PK_KEYS
PK_PICKS = PK_HEADS * PK_TOPK
TK_TM = 256
WB_TM = 128
WB_GRP = 16
WB_STRIDE = PK_SIDE + SUBLANES
PM_TM = 512
PM_SLABS = 2


def _score_kernel(h_ref, wq_ref, keys_ref, s_ref):
    q = _dot(h_ref[...].astype(BF16), wq_ref[...])
    for hp in range(PK_HP):
        s_ref[hp] = _dot_nt(q[:, hp * LANES:(hp + 1) * LANES].astype(BF16), keys_ref[hp])


def _score_call(h, wq, keys):
    return pl.pallas_call(
        _score_kernel,
        grid=(h.shape[0] // TM,),
        in_specs=[pl.BlockSpec((TM, D_MODEL), lambda i: (i, 0)),
                  pl.BlockSpec((D_MODEL, PK_HP * LANES), lambda i: (0, 0)),
                  pl.BlockSpec((PK_HP, PK_KEYS, LANES), lambda i: (0, 0, 0))],
        out_specs=pl.BlockSpec((PK_HP, TM, PK_KEYS), lambda i: (0, i, 0)),
        out_shape=jax.ShapeDtypeStruct((PK_HP, h.shape[0], PK_KEYS), F32),
        compiler_params=_cparams(("parallel",)),
        name="peer_scores",
    )(h, wq, keys)


def _top16(s, lane):
    shape = (s.shape[0], LANES)
    out_lane = lax.broadcasted_iota(jnp.int32, shape, 1)

    def body(r, carry):
        s, vals, idxs = carry
        m = jnp.max(s, axis=-1, keepdims=True)
        am = jnp.min(jnp.where(s == m, lane, float(s.shape[1])), axis=-1, keepdims=True)
        vals = jnp.where(out_lane == r, m, vals)
        idxs = jnp.where(out_lane == r, am, idxs)
        return jnp.where(lane == am, -jnp.inf, s), vals, idxs

    zero = jnp.zeros(shape, F32)
    _, vals, idxs = lax.fori_loop(0, PK_TOPK, body, (s, zero, zero))
    return vals, idxs


def _topk_kernel(s_ref, i1_ref, i2_ref, g_ref):
    h = pl.program_id(1)
    lane = lax.broadcasted_iota(jnp.int32, (TK_TM, LANES), 1).astype(F32)
    lane2 = lax.broadcasted_iota(jnp.int32, (TK_TM, PK_TOPK * PK_TOPK), 1).astype(F32)
    v1, a1 = _top16(s_ref[0], lane)
    v2, a2 = _top16(s_ref[1], lane)
    er = lax.broadcasted_iota(jnp.int32, (LANES, PK_TOPK * PK_TOPK), 0)
    ec = lax.broadcasted_iota(jnp.int32, (LANES, PK_TOPK * PK_TOPK), 1)
    rep_a = jnp.where(jnp.logical_and(ec >= er * PK_TOPK, ec < er * PK_TOPK + PK_TOPK), 1.0, 0.0).astype(BF16)
    rep_b = jnp.zeros((LANES, PK_TOPK * PK_TOPK), F32)
    for a in range(PK_TOPK):
        rep_b = jnp.where(ec == er + a * PK_TOPK, 1.0, rep_b)
    rep_b = jnp.where(er < PK_TOPK, rep_b, 0.0).astype(BF16)
    cand = _dot_x_sel(v1, rep_a) + _dot_x_sel(v2, rep_b)
    c1 = _dot_x_sel(a1, rep_a)
    c2 = _dot_x_sel(a2, rep_b)

    out_lane = lax.broadcasted_iota(jnp.int32, (TK_TM, LANES), 1)

    def body(r, carry):
        cand, best, p1, p2 = carry
        m = jnp.max(cand, axis=-1, keepdims=True)
        pos = jnp.min(jnp.where(cand == m, lane2, float(PK_TOPK * PK_TOPK)), axis=-1, keepdims=True)
        hit = lane2 == pos
        e1 = jnp.max(jnp.where(hit, c1, -1.0), axis=-1, keepdims=True)
        e2 = jnp.max(jnp.where(hit, c2, -1.0), axis=-1, keepdims=True)
        at = out_lane == r
        return (jnp.where(hit, -jnp.inf, cand), jnp.where(at, m, best),
                jnp.where(at, e1, p1), jnp.where(at, e2, p2))

    zero = jnp.zeros((TK_TM, LANES), F32)
    _, best, p1, p2 = lax.fori_loop(0, PK_TOPK, body, (cand, zero, zero, zero))
    valid = out_lane < PK_TOPK
    ex = jnp.where(valid, jnp.exp(best - jnp.max(jnp.where(valid, best, -jnp.inf), axis=-1, keepdims=True)), 0.0)
    gate = ex / jnp.sum(ex, axis=-1, keepdims=True)
    pr = lax.broadcasted_iota(jnp.int32, (LANES, LANES), 0)
    pc = lax.broadcasted_iota(jnp.int32, (LANES, LANES), 1)
    place = jnp.where(jnp.logical_and(pr < PK_TOPK, pc == pr + h * PK_TOPK), 1.0, 0.0).astype(BF16)

    @pl.when(h == 0)
    def _():
        i1_ref[...] = jnp.zeros_like(i1_ref)
        i2_ref[...] = jnp.zeros_like(i2_ref)
        g_ref[...] = jnp.zeros_like(g_ref)

    i1_ref[...] += _dot_x_sel(jnp.where(valid, p1, 0.0), place)
    i2_ref[...] += _dot_x_sel(jnp.where(valid, p2, 0.0), place)
    g_ref[...] += _dot_x_sel(gate, place)


def _topk_call(s):
    spec = pl.BlockSpec((TK_TM, LANES), lambda i, h: (i, 0))
    shp = jax.ShapeDtypeStruct((s.shape[1], PK_PICKS), F32)
    return pl.pallas_call(
        _topk_kernel,
        grid=(s.shape[1] // TK_TM, PK_HEADS),
        in_specs=[pl.BlockSpec((2, TK_TM, PK_KEYS), lambda i, h: (h, i, 0))],
        out_specs=[spec, spec, spec],
        out_shape=[shp, shp, shp],
        compiler_params=_cparams(("parallel", "arbitrary")),
        name="peer_topk",
    )(s)


def _wbuild_kernel(i1_ref, i2_ref, g_ref, w_ref, stage):
    sub = lax.broadcasted_iota(jnp.int32, (PK_SIDE, PK_PICKS), 0).astype(F32)

    def group(gi, carry):
        t0 = pl.multiple_of(gi * WB_GRP, WB_GRP)
        for tt in range(WB_GRP):
            r1 = i1_ref[pl.ds(t0 + tt, 1), :]
            r2 = i2_ref[pl.ds(t0 + tt, 1), :]
            rg = g_ref[pl.ds(t0 + tt, 1), :]
            p1t = jnp.where(sub == r1, 1.0, 0.0).astype(BF16)
            q2t = jnp.where(sub == r2, rg, 0.0).astype(BF16)
            stage[tt * WB_STRIDE:tt * WB_STRIDE + PK_SIDE, :] = _dot_nt(p1t, q2t)
        for i1 in range(PK_SIDE):
            lo = stage[pl.ds(i1, SUBLANES, stride=WB_STRIDE), :]
            hi = stage[pl.ds(i1 + SUBLANES * WB_STRIDE, SUBLANES, stride=WB_STRIDE), :]
            w_ref[0, i1, pl.ds(t0, WB_GRP), :] = jnp.concatenate([lo, hi], axis=0).astype(BF16)
        return carry

    lax.fori_loop(0, WB_TM // WB_GRP, group, 0)


def _wbuild_call(i1, i2, g):
    spec = pl.BlockSpec((WB_TM, PK_PICKS), lambda i: (i, 0))
    return pl.pallas_call(
        _wbuild_kernel,
        grid=(i1.shape[0] // WB_TM,),
        in_specs=[spec, spec, spec],
        out_specs=pl.BlockSpec((1, PK_SIDE, WB_TM, PK_SIDE), lambda i: (i, 0, 0, 0)),
        out_shape=jax.ShapeDtypeStruct((i1.shape[0] // WB_TM, PK_SIDE, WB_TM, PK_SIDE), BF16),
        scratch_shapes=[pltpu.VMEM((WB_GRP * WB_STRIDE, PK_SIDE), F32)],
        compiler_params=_cparams(("parallel",)),
        name="peer_route",
    )(i1, i2, g)


def _peer_kernel(h_ref, u_ref, v_ref, w_ref, x_ref, g2_ref, o_ref, acc):
    j = pl.program_id(1)

    @pl.when(j == 0)
    def _():
        acc[...] = jnp.zeros_like(acc)

    a = _dot_nt(h_ref[...], u_ref[...])
    nsub = PM_TM // WB_TM
    w = jnp.concatenate(
        [jnp.concatenate([w_ref[n, sl] for sl in range(PM_SLABS)], axis=1) for n in range(nsub)], axis=0)
    acc[...] += _dot((jax.nn.gelu(a) * w.astype(F32)).astype(BF16), v_ref[...])

    @pl.when(j == pl.num_programs(1) - 1)
    def _():
        o_ref[...] = x_ref[...] + g2_ref[0] * acc[...]


def _peer_call(h16, u16, v16, w, x, mod3):
    ne = PM_SLABS * PK_SIDE
    tok = pl.BlockSpec((PM_TM, D_MODEL), lambda i, j: (i, 0))
    return pl.pallas_call(
        _peer_kernel,
        grid=(x.shape[0] // PM_TM, PK_SIDE // PM_SLABS),
        in_specs=[tok,
                  pl.BlockSpec((ne, D_MODEL), lambda i, j: (j, 0)),
                  pl.BlockSpec((ne, D_MODEL), lambda i, j: (j, 0)),
                  pl.BlockSpec((PM_TM // WB_TM, PM_SLABS, WB_TM, PK_SIDE), lambda i, j: (i, j, 0, 0)),
                  tok,
                  pl.BlockSpec((1, 1, D_MODEL), lambda i, j: (_cond_row(i) * 6 + 5, 0, 0))],
        out_specs=tok,
        out_shape=jax.ShapeDtypeStruct(x.shape, F32),
        scratch_shapes=[pltpu.VMEM((PM_TM, D_MODEL), F32)],
        compiler_params=_cparams(("parallel", "arbitrary")),
        name="peer_experts",
    )(h16, u16, v16, w, x, mod3)


def _final_norm_kernel(x_ref, g_ref, o_ref):
    x = x_ref[...]
    o_ref[...] = x * lax.rsqrt(jnp.mean(x * x, axis=-1, keepdims=True) + NORM_EPS) * g_ref[...]


def _final_norm_call(x, g):
    spec = pl.BlockSpec((TM, D_MODEL), lambda i: (i, 0))
    return pl.pallas_call(
        _final_norm_kernel,
        grid=(x.shape[0] // TM,),
        in_specs=[spec, pl.BlockSpec((1, D_MODEL), lambda i: (0, 0))],
        out_specs=spec,
        out_shape=jax.ShapeDtypeStruct(x.shape, F32),
        compiler_params=_cparams(("parallel",)),
        name="final_norm",
    )(x, g)


def _dn_params(conv_w, a_log, dt_bias, dn_norm):
    bc = lambda p: jnp.broadcast_to(p.astype(F32).T[:, :, None, None], (DN_HEADS, 2, 1, LANES))
    return {"conv": conv_w.astype(F32), "alog": bc(a_log), "dtb": bc(dt_bias),
            "gn": dn_norm.astype(F32).reshape(1, LANES)}


def _dn_call(z, qkv, col, alog, dtb, gn, s0, seq_len, n_seq, row0, with_state_out):
    blk0 = row0 // seq_len
    hd = DN_HEADS
    in_specs = [pl.BlockSpec((seq_len, LANES), lambda s, h: (s, h)),
                pl.BlockSpec((seq_len, LANES), lambda s, h: (s, hd + h)),
                pl.BlockSpec((seq_len, LANES), lambda s, h: (s, 2 * hd + h)),
                pl.BlockSpec((seq_len, LANES), lambda s, h: (blk0 + s, col["ab"])),
                pl.BlockSpec((seq_len, LANES), lambda s, h: (blk0 + s, col["gate"] + h)),
                pl.BlockSpec((1, 2, 1, LANES), lambda s, h: (h, 0, 0, 0)),
                pl.BlockSpec((1, 2, 1, LANES), lambda s, h: (h, 0, 0, 0)),
                pl.BlockSpec((1, LANES), lambda s, h: (0, 0))]
    args = [qkv, qkv, qkv, z, z, alog, dtb, gn]
    st_spec = pl.BlockSpec((1, 2, 1, LANES, LANES), lambda s, h: (s, 0, h, 0, 0))
    has_s0 = s0 is not None
    if has_s0:
        in_specs.append(st_spec)
        args.append(s0)
    out_specs = [pl.BlockSpec((seq_len, LANES), lambda s, h: (s, h))]
    out_shape = [jax.ShapeDtypeStruct((n_seq * seq_len, hd * LANES), BF16)]
    if with_state_out:
        out_specs.append(st_spec)
        out_shape.append(jax.ShapeDtypeStruct((n_seq, 2, hd, LANES, LANES), F32))
    return pl.pallas_call(
        functools.partial(_dn_kernel, seq_len=seq_len, with_state_out=with_state_out, has_s0=has_s0),
        grid=(n_seq, hd),
        in_specs=in_specs,
        out_specs=out_specs,
        out_shape=out_shape,
        scratch_shapes=[pltpu.VMEM((seq_len, LANES), F32), pltpu.VMEM((seq_len, LANES), F32),
                        pltpu.VMEM((2, LANES, LANES), F32)],
        compiler_params=_cparams(("parallel", "parallel")),
        name=f"mix_dn_{seq_len}",
    )(*args)


def _grid_col_major(t):
    n, d = t.shape
    rows = DEC_SEQ // GRID_W
    return t.reshape(DEC_BATCH, rows, GRID_W, d).transpose(0, 2, 1, 3).reshape(n, d)


def _grid_row_major(t):
    n, d = t.shape
    rows = DEC_SEQ // GRID_W
    return t.reshape(DEC_BATCH, GRID_W, rows, d).transpose(0, 2, 1, 3).reshape(n, d)


def _both_groups(fn, s0_sample):
    res_p = fn(None, SEQ, BATCH, 0, True)
    res_s = fn(s0_sample, DEC_SEQ, DEC_BATCH, T_PROMPT, False)
    return jnp.concatenate([res_p[0], res_s[0]], axis=0), res_p[1]


def kernel(x_prompt, x_sample, state_gla, state_hgrn, state_s5_re, state_s5_im, state_dn, c, c_ctx, w_ada, b_ada, norm1, norm2, w_in_even, w_gla_gate, b_gla_gate, gla_norm, hg_lb, hg_norm, w_in_odd, s5_lam_re, s5_lam_im, s5_log_dt, s5_b_re, s5_b_im, s5_c_re, s5_c_im, s5_d, s5_w_glu, s5_b_glu, dn_conv, dn_a_log, dn_dt_bias, dn_norm, w_out, pk_w_q, pk_keys, pk_u, pk_v, final_norm):
    x = jnp.concatenate([x_prompt.reshape(T_PROMPT, D_MODEL), x_sample.reshape(T_SAMPLE, D_MODEL)], axis=0)
    cond = jnp.zeros((N_COND, D_MODEL), F32).at[0].set(c_ctx.astype(F32)).at[1:1 + DEC_BATCH].set(c.astype(F32))
    mod = _ada_call(cond, w_ada.astype(F32), b_ada.astype(F32))

    out_g, out_h, out_re, out_im, out_dn = [], [], [], [], []
    for l in range(DEPTH):
        mod3 = mod[l].reshape(N_COND * 6, 1, D_MODEL)
        g1 = norm1[l].astype(F32).reshape(1, D_MODEL)
        if l % 2 == 0:
            e = l // 2
            w_in = _relayout_cols(w_in_even[e], EVEN_IDX).astype(BF16)
            z = _in_call(x, mod3, g1, w_in, 2)
            prm = _even_params(w_gla_gate[e], b_gla_gate[e], gla_norm[e], hg_lb, hg_norm[e], e)
            s0g = jnp.pad(state_gla[:, e].astype(F32), ((0, 0),) * 3 + ((0, HG_DK - GLA_DK), (0, 0)))
            m_g, st_g = _both_groups(
                lambda s0, L, n, r0, so: _even_call(z, "gla", EVEN_COL, prm["gla"], s0, L, n, r0, so), s0g)
            m_h, st_h = _both_groups(
                lambda s0, L, n, r0, so: _even_call(z, "hgrn", EVEN_COL, prm["hgrn"], s0, L, n, r0, so),
                state_hgrn[:, e].astype(F32))
            m = jnp.concatenate([m_g, m_h], axis=1)
            out_g.append(st_g[..., :GLA_DK, :])
            out_h.append(st_h)
        else:
            j = l // 2
            x_in = jnp.concatenate([x[:T_PROMPT], _grid_col_major(x[T_PROMPT:])], axis=0)
            w_in = _relayout_cols(w_in_odd[j], ODD_IDX).astype(BF16)
            z = _in_call(x_in, mod3, g1, w_in, 3)
            bb, cc, apow = _s5_params(s5_lam_re[j], s5_lam_im[j], s5_log_dt[j], s5_b_re[j], s5_b_im[j],
                                      s5_c_re[j], s5_c_im[j])
            dsk = s5_d[j].astype(F32).reshape(1, MIX_W)
            y_p, fr, fi = _s5_call(z, ODD_COL["u"], bb, cc, apow, dsk, None, None, SEQ, BATCH, 0)
            y_s, _, _ = _s5_call(z, ODD_COL["u"], bb, cc, apow, dsk,
                                 state_s5_re[:, j].astype(F32).reshape(DEC_BATCH, 2, -1),
                                 state_s5_im[:, j].astype(F32).reshape(DEC_BATCH, 2, -1),
                                 DEC_SEQ, DEC_BATCH, T_PROMPT)
            m_s5 = _glu_call(jnp.concatenate([y_p, y_s], axis=0), s5_w_glu[j].astype(BF16),
                             s5_b_glu[j].astype(F32).reshape(1, MIX_W))
            dp = _dn_params(dn_conv[j], dn_a_log[j], dn_dt_bias[j], dn_norm[j])

            def dn(s0, L, n, r0, so):
                qkv = _conv_call(z, ODD_COL["qkv"], dp["conv"], L, n, r0)
                return _dn_call(z, qkv, ODD_COL, dp["alog"], dp["dtb"], dp["gn"], s0, L, n, r0, so)

            m_dn, st_dn = _both_groups(dn, state_dn[:, j].astype(F32))
            m = jnp.concatenate([m_s5, m_dn], axis=1)
            m = jnp.concatenate([m[:T_PROMPT], _grid_row_major(m[T_PROMPT:])], axis=0)
            out_re.append(fr.reshape(BATCH, 2, S5_G, S5_P))
            out_im.append(fi.reshape(BATCH, 2, S5_G, S5_P))
            out_dn.append(st_dn)
        x, h2 = _out_call(x, m, w_out[l].astype(BF16), mod3, norm2[l].astype(F32).reshape(1, D_MODEL))
        s = _score_call(h2, pk_w_q[l].astype(BF16), pk_keys[l].reshape(PK_HP, PK_KEYS, PK_DQ // 2).astype(BF16))
        i1, i2, gate = _topk_call(s)
        w = _wbuild_call(i1, i2, gate)
        x = _peer_call(h2.astype(BF16), pk_u[l].astype(BF16), pk_v[l].astype(BF16), w, x, mod3)
    y = _final_norm_call(x, final_norm.astype(F32).reshape(1, D_MODEL))
    return (y[:T_PROMPT].reshape(BATCH, SEQ, D_MODEL), y[T_PROMPT:].reshape(DEC_BATCH, DEC_SEQ, D_MODEL),
            jnp.stack(out_g, axis=1), jnp.stack(out_h, axis=1), jnp.stack(out_re, axis=1),
            jnp.stack(out_im, axis=1), jnp.stack(out_dn, axis=1))
```

```python
import functools

import numpy as np
import jax
import jax.numpy as jnp
from jax import lax
from jax.experimental import pallas as pl
from jax.experimental.pallas import tpu as pltpu

F32 = jnp.float32
BF16 = jnp.bfloat16

D_MODEL = 1024
BATCH = 32
SEQ = 256
DEPTH = 4
DEC_BATCH = 8
DEC_SEQ = 2048
GRID_W = 64
N_EVEN = 2
N_ODD = 2
MIX_W = 512
GLA_HEADS = 4
GLA_DK = 64
GLA_DV = 128
GLA_RANK = 16
GLA_TAU = 16.0
HG_HEADS = 4
HG_DK = 128
HG_DV = 128
S5_GS = 16
S5_G = 32
S5_P = 64
DN_HEADS = 4
DN_DK = 128
DN_DV = 128
DN_CONV = 5
PK_HEADS = 8
PK_KEYS = 128
PK_DQ = 256
PK_TOPK = 16
NORM_EPS = 1e-6
GATE_FLOOR = 1e-30

T_PROMPT = BATCH * SEQ
T_SAMPLE = DEC_BATCH * DEC_SEQ
T_ALL = T_PROMPT + T_SAMPLE
N_COND = 16

LANES = 128
SUBLANES = 8
VMEM_LIMIT = 56 * 1024 * 1024

TM = 512


def _layout(pieces):
    idx, col = [], {}
    for name, src, width in pieces:
        assert len(idx) % LANES == 0 and width % LANES == 0
        col[name] = len(idx) // LANES
        src = list(src)
        idx += src + [-1] * (width - len(src))
    return np.asarray(idx, np.int32), col


def _even_layout():
    o = np.cumsum((0, 256, 256, 512, 16, 16, 512, 512, 512, 512, 512, 512))
    rng = lambda i: range(o[i], o[i + 1])
    pieces = []
    for nm, base in (("gq", o[0]), ("gk", o[1])):
        for h in range(GLA_HEADS):
            pieces.append((nm if h == 0 else f"{nm}{h}", range(base + 64 * h, base + 64 * h + 64), LANES))
    pieces += [("gv", rng(2), 512), ("gr", rng(5), 512),
               ("hq", rng(6), 512), ("hf_f", rng(7), 512), ("hf_b", rng(8), 512),
               ("hi", rng(9), 512), ("hgate", rng(10), 512),
               ("lr", list(rng(3)) + list(rng(4)), LANES), ("pad", [], LANES)]
    return _layout(pieces)


def _odd_layout():
    o = np.cumsum((0, 512, 1536, 4, 4, 4, 4, 512))
    rng = lambda i: range(o[i], o[i + 1])
    return _layout([("u", rng(0), 512), ("qkv", rng(1), 1536), ("gate", rng(6), 512),
                    ("ab", range(o[2], o[6]), LANES)])


EVEN_IDX, EVEN_COL = _even_layout()
ODD_IDX, ODD_COL = _odd_layout()


def _relayout_cols(w, idx):
    wz = jnp.concatenate([w, jnp.zeros(w.shape[:-1] + (1,), w.dtype)], axis=-1)
    return jnp.take(wz, jnp.where(idx < 0, w.shape[-1], idx), axis=-1)


def _cparams(sem):
    return pltpu.CompilerParams(dimension_semantics=sem, vmem_limit_bytes=VMEM_LIMIT)


def _cond_row(i, tm=TM):
    n_p = T_PROMPT // tm
    per_seq = DEC_SEQ // tm
    return jnp.where(i < n_p, 0, 1 + (i - n_p) // per_seq)


def _split3(x):
    hi = x.astype(BF16)
    r1 = x - hi.astype(F32)
    mid = r1.astype(BF16)
    lo = (r1 - mid.astype(F32)).astype(BF16)
    return hi, mid, lo


def _dot(a, b):
    return jnp.dot(a, b, preferred_element_type=F32)


def _dot_nt(a, b):
    return lax.dot_general(a, b, (((1,), (1,)), ((), ())), preferred_element_type=F32)


def _dot_tn(a, b):
    return lax.dot_general(a, b, (((0,), (0,)), ((), ())), preferred_element_type=F32)


def _dot_sel(sel_bf16, x):
    hi, mid, lo = _split3(x)
    return _dot(sel_bf16, hi) + _dot(sel_bf16, mid) + _dot(sel_bf16, lo)


def _dot_x_sel(x, sel_bf16):
    hi, mid, lo = _split3(x)
    return _dot(hi, sel_bf16) + _dot(mid, sel_bf16) + _dot(lo, sel_bf16)


def _dot_hi(a, b):
    ah = a.astype(BF16)
    al = (a - ah.astype(F32)).astype(BF16)
    bh = b.astype(BF16)
    bl = (b - bh.astype(F32)).astype(BF16)
    return _dot(ah, bh) + (_dot(ah, bl) + _dot(al, bh))


def _ada_kernel(c_ref, w_ref, b_ref, o_ref):
    c = c_ref[...]
    cs = c * jax.nn.sigmoid(c)
    o_ref[0] = _dot(cs, w_ref[0]) + b_ref[0]


def _ada_call(cond, w_ada, b_ada):
    nt = 6
    return pl.pallas_call(
        _ada_kernel,
        grid=(DEPTH, nt),
        in_specs=[
            pl.BlockSpec((N_COND, D_MODEL), lambda l, j: (0, 0)),
            pl.BlockSpec((1, D_MODEL, D_MODEL), lambda l, j: (l, 0, j)),
            pl.BlockSpec((1, 1, D_MODEL), lambda l, j: (l, 0, j)),
        ],
        out_specs=pl.BlockSpec((1, N_COND, D_MODEL), lambda l, j: (l, 0, j)),
        out_shape=jax.ShapeDtypeStruct((DEPTH, N_COND, 6 * D_MODEL), F32),
        compiler_params=_cparams(("parallel", "parallel")),
        name="ada_mod",
    )(cond, w_ada, b_ada.reshape(DEPTH, 1, 6 * D_MODEL))


def _modulated_norm(x, g, sc, sh):
    ms = jnp.mean(x * x, axis=-1, keepdims=True)
    y = x * lax.rsqrt(ms + NORM_EPS) * g
    return y * (1.0 + sc) + sh


def _in_kernel(x_ref, sh_ref, sc_ref, g_ref, w_ref, o_ref):
    h = _modulated_norm(x_ref[...], g_ref[...], sc_ref[0], sh_ref[0])
    o_ref[...] = _dot(h.astype(BF16), w_ref[...])


def _in_call(x, mod3, g, w, n_col_tiles):
    n = w.shape[1]
    tn = n // n_col_tiles
    return pl.pallas_call(
        _in_kernel,
        grid=(n_col_tiles, x.shape[0] // TM),
        in_specs=[
            pl.BlockSpec((TM, D_MODEL), lambda j, i: (i, 0)),
            pl.BlockSpec((1, 1, D_MODEL), lambda j, i: (_cond_row(i) * 6 + 0, 0, 0)),
            pl.BlockSpec((1, 1, D_MODEL), lambda j, i: (_cond_row(i) * 6 + 1, 0, 0)),
            pl.BlockSpec((1, D_MODEL), lambda j, i: (0, 0)),
            pl.BlockSpec((D_MODEL, tn), lambda j, i: (0, j)),
        ],
        out_specs=pl.BlockSpec((TM, tn), lambda j, i: (i, j)),
        out_shape=jax.ShapeDtypeStruct((x.shape[0], n), F32),
        compiler_params=_cparams(("parallel", "parallel")),
        name="in_proj",
    )(x, mod3, mod3, g, w)


def _out_kernel(x_ref, m_ref, w_ref, g1_ref, sh_ref, sc_ref, g_ref, xo_ref, h_ref):
    x = x_ref[...] + g1_ref[0] * _dot(m_ref[...], w_ref[...])
    xo_ref[...] = x
    h_ref[...] = _modulated_norm(x, g_ref[...], sc_ref[0], sh_ref[0])


def _out_call(x, m, w_out, mod3, g2):
    spec_tok = pl.BlockSpec((TM, D_MODEL), lambda i: (i, 0))

    def mod_spec(k):
        return pl.BlockSpec((1, 1, D_MODEL), lambda i: (_cond_row(i) * 6 + k, 0, 0))

    return pl.pallas_call(
        _out_kernel,
        grid=(x.shape[0] // TM,),
        in_specs=[
            spec_tok,
            spec_tok,
            pl.BlockSpec((D_MODEL, D_MODEL), lambda i: (0, 0)),
            mod_spec(2), mod_spec(3), mod_spec(4),
            pl.BlockSpec((1, D_MODEL), lambda i: (0, 0)),
        ],
        out_specs=[spec_tok, spec_tok],
        out_shape=[jax.ShapeDtypeStruct(x.shape, F32)] * 2,
        compiler_params=_cparams(("parallel",)),
        name="out_proj",
    )(x, m, w_out, mod3, mod3, mod3, g2)


CHUNK = 64
SUB = 16
NSUB = CHUNK // SUB
NEG_BIG = -1e30
HPS = 2


def _tri(n, lower, strict=False):
    r = lax.broadcasted_iota(jnp.int32, (n, n), 0)
    c = lax.broadcasted_iota(jnp.int32, (n, n), 1)
    if lower:
        return (c < r) if strict else (c <= r)
    return (c > r) if strict else (c >= r)


def _lockstep(gens):
    results = [None] * len(gens)
    active = list(enumerate(gens))
    while active:
        still = []
        for i, g in active:
            try:
                next(g)
                still.append((i, g))
            except StopIteration as stop:
                results[i] = stop.value
        active = still
    return results


def _gla_chunk(q, k, v, la, st, reverse):
    kdim = q.shape[1]
    tri = jnp.where(_tri(CHUNK, not reverse), 1.0, 0.0).astype(BF16)
    b = _dot_sel(tri, la)
    yield
    last = 0 if reverse else CHUNK - 1
    b_last = b[last:last + 1, :]
    o_inter = _dot_nt((q * jnp.exp(b)).astype(BF16), st.astype(BF16))
    kd = k * jnp.exp(b_last - b)
    st_new = st * jnp.exp(b_last) + _dot_tn(v.astype(BF16), kd.astype(BF16))
    yield

    ones_k = jnp.ones((kdim, LANES), BF16)
    jrow = lax.broadcasted_iota(jnp.int32, (SUB, kdim), 0)
    rsel = lax.broadcasted_iota(jnp.int32, (SUB, SUB * SUB), 0)
    csel = lax.broadcasted_iota(jnp.int32, (SUB, SUB * SUB), 1)
    in_grp = jnp.logical_and(csel >= rsel * SUB, csel < rsel * SUB + SUB)
    sel = jnp.where(in_grp, 1.0, 0.0).astype(BF16)
    outs = []
    for blk in range(NSUB):
        r0 = blk * SUB
        q_i, k_i, b_i, v_i = q[r0:r0 + SUB], k[r0:r0 + SUB], b[r0:r0 + SUB], v[r0:r0 + SUB]
        rows = []
        for i in range(SUB):
            keep = (jrow >= i) if reverse else (jrow <= i)
            e = jnp.exp(jnp.where(keep, b_i[i:i + 1, :] - b_i, NEG_BIG))
            rows.append((q_i[i:i + 1, :] * k_i) * e)
        p = jnp.concatenate(rows, axis=0)
        att = _dot(p.astype(BF16), ones_k)
        yield
        zv = att * jnp.concatenate([v_i] * SUB, axis=0)
        o_blk = _dot(sel, zv.astype(BF16))
        yield
        if reverse and blk < NSUB - 1:
            r1 = r0 + SUB
            ref = b[r1:r1 + 1, :]
            qt = q_i * jnp.exp(b_i - ref)
            kt = k[r1:] * jnp.exp(ref - b[r1:])
            a_off = _dot_nt(qt.astype(BF16), kt.astype(BF16))
            yield
            o_blk = o_blk + _dot(a_off.astype(BF16), v[r1:].astype(BF16))
        if (not reverse) and blk > 0:
            ref = b[r0 - 1:r0, :]
            qt = q_i * jnp.exp(b_i - ref)
            kt = k[:r0] * jnp.exp(ref - b[:r0])
            a_off = _dot_nt(qt.astype(BF16), kt.astype(BF16))
            yield
            o_blk = o_blk + _dot(a_off.astype(BF16), v[:r0].astype(BF16))
        outs.append(o_blk)
    return o_inter + jnp.concatenate(outs, axis=0), st_new


def _log_sigmoid(x):
    return -(jnp.maximum(-x, 0.0) + jnp.log1p(jnp.exp(-jnp.abs(x))))


def _head_norm_gate(o, g, gate):
    o = o * lax.rsqrt(jnp.mean(o * o, axis=-1, keepdims=True) + NORM_EPS)
    return (o * g) * (gate * jax.nn.sigmoid(gate))


def _even_kernel(*refs, mode, seq_len, with_state_out, has_s0):
    it = iter(refs)
    if mode == "gla":
        q_ref, k_ref, v_ref, lr_ref, gate_ref, wg_ref, bg_ref, gn_ref = (next(it) for _ in range(8))
    else:
        q_ref, ff_ref, fb_ref, v_ref, gate_ref, lb_ref, gn_ref = (next(it) for _ in range(7))
    s0_ref = next(it) if has_s0 else None
    m_ref = next(it)
    so_ref = next(it) if with_state_out else None
    of_scr, ob_scr = next(it), next(it)
    n_chunks = seq_len // CHUNK
    streams = [(hh, d) for hh in range(HPS) for d in range(2)]

    def chain(c, hh, d, st):
        rows = pl.ds(pl.multiple_of(c * CHUNK, CHUNK), CHUNK)
        cols = slice(hh * LANES, (hh + 1) * LANES)
        if mode == "gla":
            q = q_ref[rows, cols] * (GLA_DK ** -0.5)
            k = k_ref[rows, cols]
            pre = _dot_hi(lr_ref[rows, :], wg_ref[hh, d]) + bg_ref[hh, d]
            yield
            la = _log_sigmoid(pre) / GLA_TAU
        else:
            q = q_ref[rows, cols]
            f = (ff_ref, fb_ref)[d][rows, cols]
            lb = lb_ref[d:d + 1, cols]
            gate = lb + (1.0 - lb) * jax.nn.sigmoid(f)
            la = jnp.log(jnp.maximum(gate, GATE_FLOOR))
            k = (1.0 - lb) * jax.nn.sigmoid(-f)
        o, st_new = yield from _gla_chunk(q, k, v_ref[rows, cols], la, st, d == 1)
        return rows, cols, o, st_new

    def body(c, carry):
        res = _lockstep([chain(n_chunks - 1 - c if d else c, hh, d, carry[i])
                         for i, (hh, d) in enumerate(streams)])
        for (hh, d), (rows, cols, o, _) in zip(streams, res):
            (ob_scr if d else of_scr)[rows, cols] = o
        return tuple(r[3] for r in res)

    if has_s0:
        init = tuple(s0_ref[0, d, hh].T for hh, d in streams)
    else:
        init = tuple(jnp.zeros((LANES, LANES), F32) for _ in streams)
    fin = lax.fori_loop(0, n_chunks, body, init)
    for hh in range(HPS):
        cols = slice(hh * LANES, (hh + 1) * LANES)
        m_ref[:, cols] = _head_norm_gate(of_scr[:, cols] + ob_scr[:, cols], gn_ref[...],
                                         gate_ref[:, cols]).astype(m_ref.dtype)
    if with_state_out:
        for i, (hh, d) in enumerate(streams):
            so_ref[0, d, hh] = fin[i].T


def _even_params(w_gate, b_gate, gla_norm, hg_lb, hg_norm, e):
    wg = jnp.zeros((GLA_HEADS, 2, LANES, LANES), F32)
    bg = jnp.zeros((GLA_HEADS, 2, 1, LANES), F32)
    for h in range(GLA_HEADS):
        for d in range(2):
            wg = wg.at[h, d, d * GLA_RANK:(d + 1) * GLA_RANK, :GLA_DK].set(
                w_gate[d, :, h * GLA_DK:(h + 1) * GLA_DK].astype(F32))
            bg = bg.at[h, d, 0, :GLA_DK].set(b_gate[d, h * GLA_DK:(h + 1) * GLA_DK].astype(F32))
    lbp = jax.nn.softmax(hg_lb.astype(F32), axis=1)
    lb = (jnp.cumsum(lbp, axis=1) - lbp[:, :1])[:, e]
    return {"gla": (wg, bg, gla_norm.astype(F32).reshape(1, LANES)),
            "hgrn": (lb, hg_norm.astype(F32).reshape(1, LANES))}


def _even_call(z, mode, col, params, s0, seq_len, n_seq, row0, with_state_out):
    heads = 4
    blk0 = row0 // seq_len
    wide = HPS * LANES

    def tok(name):
        c0 = col[name] // HPS
        return pl.BlockSpec((seq_len, wide), lambda s, h: (blk0 + s, c0 + h))

    if mode == "gla":
        wg, bg, gn = params
        in_specs = [tok("gq"), tok("gk"), tok("gv"),
                    pl.BlockSpec((seq_len, LANES), lambda s, h: (blk0 + s, col["lr"])),
                    tok("gr"),
                    pl.BlockSpec((HPS, 2, LANES, LANES), lambda s, h: (h, 0, 0, 0)),
                    pl.BlockSpec((HPS, 2, 1, LANES), lambda s, h: (h, 0, 0, 0)),
                    pl.BlockSpec((1, LANES), lambda s, h: (0, 0))]
        args = [z, z, z, z, z, wg, bg, gn]
    else:
        lb, gn = params
        in_specs = [tok("hq"), tok("hf_f"), tok("hf_b"), tok("hi"), tok("hgate"),
                    pl.BlockSpec((2, wide), lambda s, h: (0, h)),
                    pl.BlockSpec((1, LANES), lambda s, h: (0, 0))]
        args = [z, z, z, z, z, lb, gn]
    st_spec = pl.BlockSpec((1, 2, HPS, LANES, LANES), lambda s, h: (s, 0, h, 0, 0))
    has_s0 = s0 is not None
    if has_s0:
        in_specs.append(st_spec)
        args.append(s0)
    out_specs = [pl.BlockSpec((seq_len, wide), lambda s, h: (s, h))]
    out_shape = [jax.ShapeDtypeStruct((n_seq * seq_len, heads * LANES), BF16)]
    if with_state_out:
        out_specs.append(st_spec)
        out_shape.append(jax.ShapeDtypeStruct((n_seq, 2, heads, LANES, LANES), F32))
    res = pl.pallas_call(
        functools.partial(_even_kernel, mode=mode, seq_len=seq_len,
                          with_state_out=with_state_out, has_s0=has_s0),
        grid=(n_seq, heads // HPS),
        in_specs=in_specs,
        out_specs=out_specs,
        out_shape=out_shape,
        scratch_shapes=[pltpu.VMEM((seq_len, wide), F32), pltpu.VMEM((seq_len, wide), F32)],
        compiler_params=_cparams(("parallel", "parallel")),
        name=f"mix_{mode}_{seq_len}",
    )(*args)
    return res


CONV_PAD = SUBLANES


def _conv_kernel(x_ref, w_ref, o_ref, pad_scr, *, seq_len):
    cb = pl.program_id(1)
    zeros = jnp.zeros((CONV_PAD, LANES), F32)
    pad_scr[0:CONV_PAD, :] = zeros
    pad_scr[CONV_PAD + seq_len:2 * CONV_PAD + seq_len, :] = zeros
    pad_scr[CONV_PAD:CONV_PAD + seq_len, :] = x_ref[...]
    acc = jnp.zeros((seq_len, LANES), F32)
    for w in range(DN_CONV):
        acc = acc + pad_scr[pl.ds(CONV_PAD + w - DN_CONV // 2, seq_len), :] * w_ref[w:w + 1, :]
    y = acc * jax.nn.sigmoid(acc)
    yn = y * lax.rsqrt(jnp.sum(y * y, axis=-1, keepdims=True) + NORM_EPS)
    is_q = cb < DN_HEADS
    is_qk = cb < 2 * DN_HEADS
    scale = jnp.where(is_q, DN_DK ** -0.5, 1.0)
    o_ref[...] = jnp.where(is_qk, yn * scale, y)


def _conv_call(z, col0, w, seq_len, n_seq, row0):
    blk0 = row0 // seq_len
    nblk = 3 * DN_HEADS
    return pl.pallas_call(
        functools.partial(_conv_kernel, seq_len=seq_len),
        grid=(n_seq, nblk),
        in_specs=[pl.BlockSpec((seq_len, LANES), lambda s, c: (blk0 + s, col0 + c)),
                  pl.BlockSpec((DN_CONV, LANES), lambda s, c: (0, c))],
        out_specs=pl.BlockSpec((seq_len, LANES), lambda s, c: (s, c)),
        out_shape=jax.ShapeDtypeStruct((n_seq * seq_len, nblk * LANES), F32),
        scratch_shapes=[pltpu.VMEM((seq_len + 2 * CONV_PAD, LANES), F32)],
        compiler_params=_cparams(("parallel", "parallel")),
        name=f"dn_conv_{seq_len}",
    )(z, w)


def _softplus(x):
    return jnp.maximum(x, 0.0) + jnp.log1p(jnp.exp(-jnp.abs(x)))


def _dn_chunk(q, k, v, g, beta, s, reverse):
    c = CHUNK
    lower = not reverse
    tri = jnp.where(_tri(c, lower), 1.0, 0.0).astype(BF16)
    gam = _dot_sel(tri, g)
    yield
    ones_c = jnp.ones((c, c), BF16)
    gam_row = _dot_sel(ones_c, jnp.where(_tri(c, reverse), g[:, :c], 0.0))
    incl = _tri(c, lower)
    strict = _tri(c, lower, strict=True)
    diff = gam[:, :c] - gam_row
    dec = jnp.where(incl, jnp.exp(jnp.where(incl, diff, 0.0)), 0.0)
    kb = k * beta
    k16 = k.astype(BF16)
    m = jnp.where(strict, _dot_nt(kb.astype(BF16), k16) * dec, 0.0)
    yield
    eg = jnp.exp(gam)
    x = jnp.concatenate([v * beta, kb * eg], axis=1)
    p = -m
    n_fac = int(np.log2(c))
    for j in range(n_fac):
        x = x + _dot_hi(p, x)
        if j + 1 < n_fac:
            p = _dot_hi(p, p)
        yield
    u, w = x[:, :LANES], x[:, LANES:]
    s16 = s.astype(BF16)
    v_new = u - _dot(w.astype(BF16), s16)
    att = _dot_nt(q.astype(BF16), k16) * dec
    yield
    o = _dot((q * eg).astype(BF16), s16) + _dot(att.astype(BF16), v_new.astype(BF16))
    last = 0 if reverse else c - 1
    gl = gam[last:last + 1, :]
    s_new = s * jnp.exp(gl) + _dot_tn((k * jnp.exp(gl - gam)).astype(BF16), v_new.astype(BF16))
    return o, s_new


def _dn_kernel(*refs, seq_len, with_state_out, has_s0):
    it = iter(refs)
    q_ref, k_ref, v_ref, ab_ref, gate_ref, alog_ref, dtb_ref, gn_ref = (next(it) for _ in range(8))
    s0_ref = next(it) if has_s0 else None
    m_ref = next(it)
    so_ref = next(it) if with_state_out else None
    of_scr, ob_scr = next(it), next(it)
    h0 = pl.program_id(1) * HPS
    n_chunks = seq_len // CHUNK
    srow = lax.broadcasted_iota(jnp.int32, (LANES, LANES), 0)
    streams = [(hh, d) for hh in range(HPS) for d in range(2)]

    def chain(c, hh, d, s):
        rows = pl.ds(pl.multiple_of(c * CHUNK, CHUNK), CHUNK)
        cols = slice(hh * LANES, (hh + 1) * LANES)
        ab = ab_ref[rows, :]
        sel_a = jnp.where(srow == DN_HEADS * d + h0 + hh, 1.0, 0.0).astype(BF16)
        sel_b = jnp.where(srow == 2 * DN_HEADS + DN_HEADS * d + h0 + hh, 1.0, 0.0).astype(BF16)
        a = _dot_x_sel(ab, sel_a)
        beta = jax.nn.sigmoid(_dot_x_sel(ab, sel_b))
        g = -jnp.exp(alog_ref[hh, d]) * _softplus(a + dtb_ref[hh, d])
        yield
        o, s_new = yield from _dn_chunk(q_ref[rows, cols], k_ref[rows, cols], v_ref[rows, cols],
                                        g, beta, s, d == 1)
        return rows, cols, o, s_new

    def body(c, carry):
        res = _lockstep([chain(n_chunks - 1 - c if d else c, hh, d, carry[i])
                         for i, (hh, d) in enumerate(streams)])
        for (hh, d), (rows, cols, o, _) in zip(streams, res):
            (ob_scr if d else of_scr)[rows, cols] = o
        return tuple(r[3] for r in res)

    if has_s0:
        init = tuple(s0_ref[0, d, hh] for hh, d in streams)
    else:
        init = tuple(jnp.zeros((LANES, LANES), F32) for _ in streams)
    fin = lax.fori_loop(0, n_chunks, body, init)
    for hh in range(HPS):
        cols = slice(hh * LANES, (hh + 1) * LANES)
        m_ref[:, cols] = _head_norm_gate(of_scr[:, cols] + ob_scr[:, cols], gn_ref[...],
                                         gate_ref[:, cols]).astype(m_ref.dtype)
    if with_state_out:
        for i, (hh, d) in enumerate(streams):
            so_ref[0, d, hh] = fin[i]


S5_TC = 128
S5_CB = 4
S5_SW = S5_G * S5_P // S5_CB


def _s5_params(lam_re, lam_im, log_dt, b_re, b_im, c_re, c_im):
    gpb = S5_G // S5_CB
    eye = jnp.eye(gpb, dtype=F32)
    n = jnp.arange(1, S5_TC + 1, dtype=F32)[:, None, None]
    bbs, ccs, aps = [], [], []
    for d in range(2):
        lr = jnp.minimum(lam_re[d].astype(F32), -1e-4)
        li = lam_im[d].astype(F32)
        dt = jnp.exp(log_dt[d].astype(F32))[:, None]
        mag = jnp.exp(lr * dt)
        ar, ai = mag * jnp.cos(li * dt), mag * jnp.sin(li * dt)
        den = lr * lr + li * li
        nr = ar - 1.0
        cr = (nr * lr + ai * li) / den
        ci = (ai * lr - nr * li) / den
        bbr = cr[..., None] * b_re - ci[..., None] * b_im
        bbi = cr[..., None] * b_im + ci[..., None] * b_re
        blk = lambda t: jnp.einsum('bgpc,gh->bgchp', t.reshape(S5_CB, gpb, S5_P, S5_GS), eye).reshape(
            S5_CB, gpb * S5_GS, gpb * S5_P)
        bbs.append(jnp.stack([blk(bbr), blk(bbi)]))
        cblk = lambda t: jnp.einsum('bgcp,gh->bgphc', t.reshape(S5_CB, gpb, S5_GS, S5_P), eye).reshape(
            S5_CB, gpb * S5_P, gpb * S5_GS)
        ccs.append(jnp.stack([cblk(c_re[d].astype(F32)), cblk(c_im[d].astype(F32))]))
        pm = jnp.exp(n * (lr * dt)[None])
        pr = (pm * jnp.cos(n * (li * dt)[None])).reshape(S5_TC, -1)
        pi = (pm * jnp.sin(n * (li * dt)[None])).reshape(S5_TC, -1)
        if d == 1:
            pr, pi = pr[::-1], pi[::-1]
        aps.append(jnp.stack([pr, pi]))
    bb = jnp.stack(bbs, axis=2).astype(BF16)
    cc = jnp.stack(ccs, axis=2).astype(BF16)
    apow = jnp.stack(aps, axis=1)
    return bb, cc, apow


def _s5_scan(xr, xi, pr, pi, reverse):
    tc = xr.shape[0]
    row = lax.broadcasted_iota(jnp.int32, xr.shape, 0)
    s = 1
    while s < tc:
        idx = tc - s if reverse else s - 1
        a_r, a_i = pr[idx:idx + 1, :], pi[idx:idx + 1, :]
        shift = tc - s if reverse else s
        keep = (row < tc - s) if reverse else (row >= s)
        sr = jnp.where(keep, pltpu.roll(xr, shift, 0), 0.0)
        si = jnp.where(keep, pltpu.roll(xi, shift, 0), 0.0)
        xr, xi = xr + (a_r * sr - a_i * si), xi + (a_r * si + a_i * sr)
        s *= 2
    return xr, xi


def _s5_kernel(*refs, seq_len, has_h0):
    it = iter(refs)
    u_ref, bb_ref, cc_ref, ap_ref, dsk_ref = (next(it) for _ in range(5))
    h0r_ref, h0i_ref = (next(it), next(it)) if has_h0 else (None, None)
    y_ref, fr_ref, fi_ref = next(it), next(it), next(it)
    yf_scr, yb_scr = next(it), next(it)
    n_chunks = seq_len // S5_TC

    def run(c, d, carry):
        cr, ci = carry
        rows = pl.ds(pl.multiple_of(c * S5_TC, S5_TC), S5_TC)
        ub = u_ref[rows, :].astype(BF16)
        pr, pi = ap_ref[0, d], ap_ref[1, d]
        xr, xi = _dot(ub, bb_ref[0, 0, d]), _dot(ub, bb_ref[1, 0, d])
        hr, hi = _s5_scan(xr, xi, pr, pi, d == 1)
        hr, hi = hr + (pr * cr - pi * ci), hi + (pr * ci + pi * cr)
        (yf_scr, yb_scr)[d][rows, :] = (_dot(hr.astype(BF16), cc_ref[0, 0, d])
                                         - _dot(hi.astype(BF16), cc_ref[1, 0, d]))
        last = 0 if d == 1 else S5_TC - 1
        return hr[last:last + 1, :], hi[last:last + 1, :]

    def body(c, carry):
        cf, cb = carry
        return run(c, 0, cf), run(n_chunks - 1 - c, 1, cb)

    if has_h0:
        init = tuple((h0r_ref[0, d:d + 1, :], h0i_ref[0, d:d + 1, :]) for d in range(2))
    else:
        z = jnp.zeros((1, S5_SW), F32)
        init = ((z, z), (z, z))
    fin = lax.fori_loop(0, n_chunks, body, init)
    y_ref[...] = dsk_ref[...] * u_ref[...] + (yf_scr[...] + yb_scr[...])
    for d in range(2):
        fr_ref[0, d:d + 1, :] = fin[d][0]
        fi_ref[0, d:d + 1, :] = fin[d][1]


def _s5_call(z, col0, bb, cc, apow, dsk, h0r, h0i, seq_len, n_seq, row0):
    blk0 = row0 // seq_len
    has_h0 = h0r is not None
    st_spec = pl.BlockSpec((1, 2, S5_SW), lambda s, c: (s, 0, c))
    in_specs = [pl.BlockSpec((seq_len, LANES), lambda s, c: (blk0 + s, col0 + c)),
                pl.BlockSpec((2, 1, 2, LANES, S5_SW), lambda s, c: (0, c, 0, 0, 0)),
                pl.BlockSpec((2, 1, 2, S5_SW, LANES), lambda s, c: (0, c, 0, 0, 0)),
                pl.BlockSpec((2, 2, S5_TC, S5_SW), lambda s, c: (0, 0, 0, c)),
                pl.BlockSpec((1, LANES), lambda s, c: (0, c))]
    args = [z, bb, cc, apow, dsk]
    if has_h0:
        in_specs += [st_spec, st_spec]
        args += [h0r, h0i]
    st_shape = jax.ShapeDtypeStruct((n_seq, 2, S5_G * S5_P), F32)
    return pl.pallas_call(
        functools.partial(_s5_kernel, seq_len=seq_len, has_h0=has_h0),
        grid=(n_seq, S5_CB),
        in_specs=in_specs,
        out_specs=[pl.BlockSpec((seq_len, LANES), lambda s, c: (s, c)), st_spec, st_spec],
        out_shape=[jax.ShapeDtypeStruct((n_seq * seq_len, MIX_W), F32), st_shape, st_shape],
        scratch_shapes=[pltpu.VMEM((seq_len, LANES), F32), pltpu.VMEM((seq_len, LANES), F32)],
        compiler_params=_cparams(("parallel", "parallel")),
        name=f"mix_s5_{seq_len}",
    )(*args)


def _glu_kernel(y_ref, w_ref, b_ref, o_ref):
    zz = jax.nn.gelu(y_ref[...])
    o_ref[...] = (zz * jax.nn.sigmoid(_dot(zz.astype(BF16), w_ref[...]) + b_ref[...])).astype(o_ref.dtype)


def _glu_call(y, w, b):
    t = y.shape[0]
    return pl.pallas_call(
        _glu_kernel,
        grid=(t // TM,),
        in_specs=[pl.BlockSpec((TM, MIX_W), lambda i: (i, 0)),
                  pl.BlockSpec((MIX_W, MIX_W), lambda i: (0, 0)),
                  pl.BlockSpec((1, MIX_W), lambda i: (0, 0))],
        out_specs=pl.BlockSpec((TM, MIX_W), lambda i: (i, 0)),
        out_shape=jax.ShapeDtypeStruct((t, MIX_W), BF16),
        compiler_params=_cparams(("parallel",)),
        name="s5_glu",
    )(y, w, b)


PK_HP = 2 * PK_HEADS
PK_SIDE = PK_KEYS
PK_PICKS = PK_HEADS * PK_TOPK
TK_TM = 128
WB_TM = 128
WB_GRP = 16
WB_STRIDE = PK_SIDE + SUBLANES
PM_TM = 512
PM_SLABS = 8


def _score_kernel(h_ref, wq_ref, keys_ref, s_ref):
    q = _dot(h_ref[...].astype(BF16), wq_ref[...])
    for hp in range(PK_HP):
        s_ref[hp] = _dot_nt(keys_ref[hp], q[:, hp * LANES:(hp + 1) * LANES].astype(BF16))


def _score_call(h, wq, keys):
    return pl.pallas_call(
        _score_kernel,
        grid=(h.shape[0] // TM,),
        in_specs=[pl.BlockSpec((TM, D_MODEL), lambda i: (i, 0)),
                  pl.BlockSpec((D_MODEL, PK_HP * LANES), lambda i: (0, 0)),
                  pl.BlockSpec((PK_HP, PK_KEYS, LANES), lambda i: (0, 0, 0))],
        out_specs=pl.BlockSpec((PK_HP, PK_KEYS, TM), lambda i: (0, 0, i)),
        out_shape=jax.ShapeDtypeStruct((PK_HP, PK_KEYS, h.shape[0]), F32),
        compiler_params=_cparams(("parallel",)),
        name="peer_scores",
    )(h, wq, keys)


def _top16_rows(s_a, s_b):
    row = lax.broadcasted_iota(jnp.int32, s_a.shape, 0).astype(F32)
    out_row = lax.broadcasted_iota(jnp.int32, (PK_TOPK, s_a.shape[1]), 0)

    def take(r, s, vals, idxs):
        m = jnp.max(s, axis=0, keepdims=True)
        am = jnp.min(jnp.where(s == m, row, float(s.shape[0])), axis=0, keepdims=True)
        at = out_row == r
        return jnp.where(row == am, -jnp.inf, s), jnp.where(at, m, vals), jnp.where(at, am, idxs)

    def body(r, carry):
        return take(r, *carry[:3]) + take(r, *carry[3:])

    zero = jnp.zeros((PK_TOPK, s_a.shape[1]), F32)
    res = lax.fori_loop(0, PK_TOPK, body, (s_a, zero, zero, s_b, zero, zero))
    return res[1], res[2], res[4], res[5]


def _pair_candidates(v1, v2):
    half = PK_TOPK // 2
    n = v1.shape[1]
    rows = [jnp.broadcast_to(v1[0:1], (PK_TOPK, n)) + v2]
    rows += [jnp.broadcast_to(v1[a:a + 1], (half, n)) + v2[:half] for a in range(1, half)]
    rows.append(v1[half:] + jnp.broadcast_to(v2[0:1], (half, n)))
    return jnp.concatenate(rows, axis=0)


def _topk_kernel(s_ref, i1_ref, i2_ref, g_ref, p1_scr, p2_scr, g_scr):
    n = TK_TM
    zero16 = jnp.zeros((PK_TOPK, n), F32)
    rank = lax.broadcasted_iota(jnp.int32, (PK_TOPK, n), 0)
    rank_f = rank.astype(F32)
    flat = _pair_candidates(rank_f * PK_TOPK, rank_f)
    out_row = rank

    def head(h, carry):
        v1, a1, v2, a2 = _top16_rows(s_ref[2 * h], s_ref[2 * h + 1])
        cand = _pair_candidates(v1, v2)
        c1 = _pair_candidates(a1, zero16)
        c2 = _pair_candidates(zero16, a2)

        def body(r, carry):
            cand, best, p1, p2 = carry
            m = jnp.max(cand, axis=0, keepdims=True)
            pos = jnp.min(jnp.where(cand == m, flat, float(PK_TOPK * PK_TOPK)), axis=0, keepdims=True)
            hit = flat == pos
            e1 = jnp.max(jnp.where(hit, c1, -1.0), axis=0, keepdims=True)
            e2 = jnp.max(jnp.where(hit, c2, -1.0), axis=0, keepdims=True)
            at = out_row == r
            return (jnp.where(hit, -jnp.inf, cand), jnp.where(at, m, best),
                    jnp.where(at, e1, p1), jnp.where(at, e2, p2))

        _, best, p1, p2 = lax.fori_loop(0, PK_TOPK, body, (cand, zero16, zero16, zero16))
        ex = jnp.exp(best - jnp.max(best, axis=0, keepdims=True))
        rows = pl.ds(pl.multiple_of(h * PK_TOPK, PK_TOPK), PK_TOPK)
        p1_scr[rows, :] = p1
        p2_scr[rows, :] = p2
        g_scr[rows, :] = ex / jnp.sum(ex, axis=0, keepdims=True)
        return carry

    lax.fori_loop(0, PK_HEADS, head, 0)
    i1_ref[...] = p1_scr[...].T
    i2_ref[...] = p2_scr[...].T
    g_ref[...] = g_scr[...].T


def _topk_call(s):
    t = s.shape[2]
    spec = pl.BlockSpec((TK_TM, PK_PICKS), lambda i: (i, 0))
    shp = jax.ShapeDtypeStruct((t, PK_PICKS), F32)
    return pl.pallas_call(
        _topk_kernel,
        grid=(t // TK_TM,),
        in_specs=[pl.BlockSpec((PK_HP, PK_KEYS, TK_TM), lambda i: (0, 0, i))],
        out_specs=[spec, spec, spec],
        out_shape=[shp, shp, shp],
        scratch_shapes=[pltpu.VMEM((PK_PICKS, TK_TM), F32)] * 3,
        compiler_params=_cparams(("parallel",)),
        name="peer_topk",
    )(s)


def _wbuild_kernel(i1_ref, i2_ref, g_ref, w_ref, stage):
    sub = lax.broadcasted_iota(jnp.int32, (PK_SIDE, PK_PICKS), 0).astype(F32)

    def group(gi, carry):
        t0 = pl.multiple_of(gi * WB_GRP, WB_GRP)
        for tt in range(WB_GRP):
            r1 = i1_ref[pl.ds(t0 + tt, 1), :]
            r2 = i2_ref[pl.ds(t0 + tt, 1), :]
            rg = g_ref[pl.ds(t0 + tt, 1), :]
            p1t = jnp.where(sub == r1, 1.0, 0.0).astype(BF16)
            q2t = jnp.where(sub == r2, rg, 0.0).astype(BF16)
            stage[tt * WB_STRIDE:tt * WB_STRIDE + PK_SIDE, :] = _dot_nt(p1t, q2t)
        for i1 in range(PK_SIDE):
            lo = stage[pl.ds(i1, SUBLANES, stride=WB_STRIDE), :]
            hi = stage[pl.ds(i1 + SUBLANES * WB_STRIDE, SUBLANES, stride=WB_STRIDE), :]
            w_ref[0, i1, pl.ds(t0, WB_GRP), :] = jnp.concatenate([lo, hi], axis=0).astype(BF16)
        return carry

    lax.fori_loop(0, WB_TM // WB_GRP, group, 0)


def _wbuild_call(i1, i2, g):
    spec = pl.BlockSpec((WB_TM, PK_PICKS), lambda i: (i, 0))
    return pl.pallas_call(
        _wbuild_kernel,
        grid=(i1.shape[0] // WB_TM,),
        in_specs=[spec, spec, spec],
        out_specs=pl.BlockSpec((1, PK_SIDE, WB_TM, PK_SIDE), lambda i: (i, 0, 0, 0)),
        out_shape=jax.ShapeDtypeStruct((i1.shape[0] // WB_TM, PK_SIDE, WB_TM, PK_SIDE), BF16),
        scratch_shapes=[pltpu.VMEM((WB_GRP * WB_STRIDE, PK_SIDE), F32)],
        compiler_params=_cparams(("parallel",)),
        name="peer_route",
    )(i1, i2, g)


def _peer_kernel(h_ref, u_ref, v_ref, w_ref, x_ref, g2_ref, o_ref, acc):
    j = pl.program_id(1)

    @pl.when(j == 0)
    def _():
        acc[...] = jnp.zeros_like(acc)

    a = _dot_nt(h_ref[...], u_ref[...])
    nsub = PM_TM // WB_TM
    w = jnp.concatenate(
        [jnp.concatenate([w_ref[n, sl] for sl in range(PM_SLABS)], axis=1) for n in range(nsub)], axis=0)
    acc[...] += _dot((jax.nn.gelu(a) * w.astype(F32)).astype(BF16), v_ref[...])

    @pl.when(j == pl.num_programs(1) - 1)
    def _():
        o_ref[...] = x_ref[...] + g2_ref[0] * acc[...]


def _peer_call(h16, u16, v16, w, x, mod3):
    ne = PM_SLABS * PK_SIDE
    tok = pl.BlockSpec((PM_TM, D_MODEL), lambda i, j: (i, 0))
    return pl.pallas_call(
        _peer_kernel,
        grid=(x.shape[0] // PM_TM, PK_SIDE // PM_SLABS),
        in_specs=[tok,
                  pl.BlockSpec((ne, D_MODEL), lambda i, j: (j, 0)),
                  pl.BlockSpec((ne, D_MODEL), lambda i, j: (j, 0)),
                  pl.BlockSpec((PM_TM // WB_TM, PM_SLABS, WB_TM, PK_SIDE), lambda i, j: (i, j, 0, 0)),
                  tok,
                  pl.BlockSpec((1, 1, D_MODEL), lambda i, j: (_cond_row(i) * 6 + 5, 0, 0))],
        out_specs=tok,
        out_shape=jax.ShapeDtypeStruct(x.shape, F32),
        scratch_shapes=[pltpu.VMEM((PM_TM, D_MODEL), F32)],
        compiler_params=_cparams(("parallel", "arbitrary")),
        name="peer_experts",
    )(h16, u16, v16, w, x, mod3)


def _final_norm_kernel(x_ref, g_ref, o_ref):
    x = x_ref[...]
    o_ref[...] = x * lax.rsqrt(jnp.mean(x * x, axis=-1, keepdims=True) + NORM_EPS) * g_ref[...]


def _final_norm_call(x, g):
    spec = pl.BlockSpec((TM, D_MODEL), lambda i: (i, 0))
    return pl.pallas_call(
        _final_norm_kernel,
        grid=(x.shape[0] // TM,),
        in_specs=[spec, pl.BlockSpec((1, D_MODEL), lambda i: (0, 0))],
        out_specs=spec,
        out_shape=jax.ShapeDtypeStruct(x.shape, F32),
        compiler_params=_cparams(("parallel",)),
        name="final_norm",
    )(x, g)


def _dn_params(conv_w, a_log, dt_bias, dn_norm):
    bc = lambda p: jnp.broadcast_to(p.astype(F32).T[:, :, None, None], (DN_HEADS, 2, 1, LANES))
    return {"conv": conv_w.astype(F32), "alog": bc(a_log), "dtb": bc(dt_bias),
            "gn": dn_norm.astype(F32).reshape(1, LANES)}


def _dn_call(z, qkv, col, alog, dtb, gn, s0, seq_len, n_seq, row0, with_state_out):
    blk0 = row0 // seq_len
    hd = DN_HEADS
    nhb = hd // HPS
    wide = HPS * LANES
    gate0 = col["gate"] // HPS
    in_specs = [pl.BlockSpec((seq_len, wide), lambda s, h: (s, h)),
                pl.BlockSpec((seq_len, wide), lambda s, h: (s, nhb + h)),
                pl.BlockSpec((seq_len, wide), lambda s, h: (s, 2 * nhb + h)),
                pl.BlockSpec((seq_len, LANES), lambda s, h: (blk0 + s, col["ab"])),
                pl.BlockSpec((seq_len, wide), lambda s, h: (blk0 + s, gate0 + h)),
                pl.BlockSpec((HPS, 2, 1, LANES), lambda s, h: (h, 0, 0, 0)),
                pl.BlockSpec((HPS, 2, 1, LANES), lambda s, h: (h, 0, 0, 0)),
                pl.BlockSpec((1, LANES), lambda s, h: (0, 0))]
    args = [qkv, qkv, qkv, z, z, alog, dtb, gn]
    st_spec = pl.BlockSpec((1, 2, HPS, LANES, LANES), lambda s, h: (s, 0, h, 0, 0))
    has_s0 = s0 is not None
    if has_s0:
        in_specs.append(st_spec)
        args.append(s0)
    out_specs = [pl.BlockSpec((seq_len, wide), lambda s, h: (s, h))]
    out_shape = [jax.ShapeDtypeStruct((n_seq * seq_len, hd * LANES), BF16)]
    if with_state_out:
        out_specs.append(st_spec)
        out_shape.append(jax.ShapeDtypeStruct((n_seq, 2, hd, LANES, LANES), F32))
    return pl.pallas_call(
        functools.partial(_dn_kernel, seq_len=seq_len, with_state_out=with_state_out, has_s0=has_s0),
        grid=(n_seq, nhb),
        in_specs=in_specs,
        out_specs=out_specs,
        out_shape=out_shape,
        scratch_shapes=[pltpu.VMEM((seq_len, wide), F32), pltpu.VMEM((seq_len, wide), F32)],
        compiler_params=_cparams(("parallel", "parallel")),
        name=f"mix_dn_{seq_len}",
    )(*args)


def _grid_col_major(t):
    n, d = t.shape
    rows = DEC_SEQ // GRID_W
    return t.reshape(DEC_BATCH, rows, GRID_W, d).transpose(0, 2, 1, 3).reshape(n, d)


def _grid_row_major(t):
    n, d = t.shape
    rows = DEC_SEQ // GRID_W
    return t.reshape(DEC_BATCH, GRID_W, rows, d).transpose(0, 2, 1, 3).reshape(n, d)


def _both_groups(fn, s0_sample):
    res_p = fn(None, SEQ, BATCH, 0, True)
    res_s = fn(s0_sample, DEC_SEQ, DEC_BATCH, T_PROMPT, False)
    return jnp.concatenate([res_p[0], res_s[0]], axis=0), res_p[1]


def kernel(x_prompt, x_sample, state_gla, state_hgrn, state_s5_re, state_s5_im, state_dn, c, c_ctx, w_ada, b_ada, norm1, norm2, w_in_even, w_gla_gate, b_gla_gate, gla_norm, hg_lb, hg_norm, w_in_odd, s5_lam_re, s5_lam_im, s5_log_dt, s5_b_re, s5_b_im, s5_c_re, s5_c_im, s5_d, s5_w_glu, s5_b_glu, dn_conv, dn_a_log, dn_dt_bias, dn_norm, w_out, pk_w_q, pk_keys, pk_u, pk_v, final_norm):
    x = jnp.concatenate([x_prompt.reshape(T_PROMPT, D_MODEL), x_sample.reshape(T_SAMPLE, D_MODEL)], axis=0)
    cond = jnp.zeros((N_COND, D_MODEL), F32).at[0].set(c_ctx.astype(F32)).at[1:1 + DEC_BATCH].set(c.astype(F32))
    mod = _ada_call(cond, w_ada.astype(F32), b_ada.astype(F32))

    out_g, out_h, out_re, out_im, out_dn = [], [], [], [], []
    for l in range(DEPTH):
        mod3 = mod[l].reshape(N_COND * 6, 1, D_MODEL)
        g1 = norm1[l].astype(F32).reshape(1, D_MODEL)
        if l % 2 == 0:
            e = l // 2
            w_in = _relayout_cols(w_in_even[e], EVEN_IDX).astype(BF16)
            z = _in_call(x, mod3, g1, w_in, 2)
            prm = _even_params(w_gla_gate[e], b_gla_gate[e], gla_norm[e], hg_lb, hg_norm[e], e)
            s0g = jnp.pad(state_gla[:, e].astype(F32), ((0, 0),) * 3 + ((0, HG_DK - GLA_DK), (0, 0)))
            m_g, st_g = _both_groups(
                lambda s0, L, n, r0, so: _even_call(z, "gla", EVEN_COL, prm["gla"], s0, L, n, r0, so), s0g)
            m_h, st_h = _both_groups(
                lambda s0, L, n, r0, so: _even_call(z, "hgrn", EVEN_COL, prm["hgrn"], s0, L, n, r0, so),
                state_hgrn[:, e].astype(F32))
            m = jnp.concatenate([m_g, m_h], axis=1)
            out_g.append(st_g[..., :GLA_DK, :])
            out_h.append(st_h)
        else:
            j = l // 2
            x_in = jnp.concatenate([x[:T_PROMPT], _grid_col_major(x[T_PROMPT:])], axis=0)
            w_in = _relayout_cols(w_in_odd[j], ODD_IDX).astype(BF16)
            z = _in_call(x_in, mod3, g1, w_in, 3)
            bb, cc, apow = _s5_params(s5_lam_re[j], s5_lam_im[j], s5_log_dt[j], s5_b_re[j], s5_b_im[j],
                                      s5_c_re[j], s5_c_im[j])
            dsk = s5_d[j].astype(F32).reshape(1, MIX_W)
            y_p, fr, fi = _s5_call(z, ODD_COL["u"], bb, cc, apow, dsk, None, None, SEQ, BATCH, 0)
            y_s, _, _ = _s5_call(z, ODD_COL["u"], bb, cc, apow, dsk,
                                 state_s5_re[:, j].astype(F32).reshape(DEC_BATCH, 2, -1),
                                 state_s5_im[:, j].astype(F32).reshape(DEC_BATCH, 2, -1),
                                 DEC_SEQ, DEC_BATCH, T_PROMPT)
            m_s5 = _glu_call(jnp.concatenate([y_p, y_s], axis=0), s5_w_glu[j].astype(BF16),
                             s5_b_glu[j].astype(F32).reshape(1, MIX_W))
            dp = _dn_params(dn_conv[j], dn_a_log[j], dn_dt_bias[j], dn_norm[j])

            def dn(s0, L, n, r0, so):
                qkv = _conv_call(z, ODD_COL["qkv"], dp["conv"], L, n, r0)
                return _dn_call(z, qkv, ODD_COL, dp["alog"], dp["dtb"], dp["gn"], s0, L, n, r0, so)

            m_dn, st_dn = _both_groups(dn, state_dn[:, j].astype(F32))
            m = jnp.concatenate([m_s5, m_dn], axis=1)
            m = jnp.concatenate([m[:T_PROMPT], _grid_row_major(m[T_PROMPT:])], axis=0)
            out_re.append(fr.reshape(BATCH, 2, S5_G, S5_P))
            out_im.append(fi.reshape(BATCH, 2, S5_G, S5_P))
            out_dn.append(st_dn)
        x, h2 = _out_call(x, m, w_out[l].astype(BF16), mod3, norm2[l].astype(F32).reshape(1, D_MODEL))
        s = _score_call(h2, pk_w_q[l].astype(BF16), pk_keys[l].reshape(PK_HP, PK_KEYS, PK_DQ // 2).astype(BF16))
        i1, i2, gate = _topk_call(s)
        w = _wbuild_call(i1, i2, gate)
        x = _peer_call(h2.astype(BF16), pk_u[l].astype(BF16), pk_v[l].astype(BF16), w, x, mod3)
    y = _final_norm_call(x, final_norm.astype(F32).reshape(1, D_MODEL))
    return (y[:T_PROMPT].reshape(BATCH, SEQ, D_MODEL), y[T_PROMPT:].reshape(DEC_BATCH, DEC_SEQ, D_MODEL),
            jnp.stack(out_g, axis=1), jnp.stack(out_h, axis=1), jnp.stack(out_re, axis=1),
            jnp.stack(out_im, axis=1), jnp.stack(out_dn, axis=1))
```

```python
import functools

import numpy as np
import jax
import jax.numpy as jnp
from jax import lax
from jax.experimental import pallas as pl
from jax.experimental.pallas import tpu as pltpu

F32 = jnp.float32
BF16 = jnp.bfloat16

D_MODEL = 1024
BATCH = 32
SEQ = 256
DEPTH = 4
DEC_BATCH = 8
DEC_SEQ = 2048
GRID_W = 64
N_EVEN = 2
N_ODD = 2
MIX_W = 512
GLA_HEADS = 4
GLA_DK = 64
GLA_DV = 128
GLA_RANK = 16
GLA_TAU = 16.0
HG_HEADS = 4
HG_DK = 128
HG_DV = 128
S5_GS = 16
S5_G = 32
S5_P = 64
DN_HEADS = 4
DN_DK = 128
DN_DV = 128
DN_CONV = 5
PK_HEADS = 8
PK_KEYS = 128
PK_DQ = 256
PK_TOPK = 16
NORM_EPS = 1e-6
GATE_FLOOR = 1e-30

T_PROMPT = BATCH * SEQ
T_SAMPLE = DEC_BATCH * DEC_SEQ
T_ALL = T_PROMPT + T_SAMPLE
N_COND = 16

LANES = 128
SUBLANES = 8
VMEM_LIMIT = 56 * 1024 * 1024

TM = 512


def _layout(pieces):
    idx, col = [], {}
    for name, src, width in pieces:
        assert len(idx) % LANES == 0 and width % LANES == 0
        col[name] = len(idx) // LANES
        src = list(src)
        idx += src + [-1] * (width - len(src))
    return np.asarray(idx, np.int32), col


def _even_layout():
    o = np.cumsum((0, 256, 256, 512, 16, 16, 512, 512, 512, 512, 512, 512))
    rng = lambda i: range(o[i], o[i + 1])
    pieces = []
    for nm, base in (("gq", o[0]), ("gk", o[1])):
        for h in range(GLA_HEADS):
            pieces.append((nm if h == 0 else f"{nm}{h}", range(base + 64 * h, base + 64 * h + 64), LANES))
    pieces += [("gv", rng(2), 512), ("gr", rng(5), 512),
               ("hq", rng(6), 512), ("hf_f", rng(7), 512), ("hf_b", rng(8), 512),
               ("hi", rng(9), 512), ("hgate", rng(10), 512),
               ("lr", list(rng(3)) + list(rng(4)), LANES), ("pad", [], LANES)]
    return _layout(pieces)


def _odd_layout():
    o = np.cumsum((0, 512, 1536, 4, 4, 4, 4, 512))
    rng = lambda i: range(o[i], o[i + 1])
    return _layout([("u", rng(0), 512), ("qkv", rng(1), 1536), ("gate", rng(6), 512),
                    ("ab", range(o[2], o[6]), LANES)])


EVEN_IDX, EVEN_COL = _even_layout()
ODD_IDX, ODD_COL = _odd_layout()


def _relayout_cols(w, idx):
    wz = jnp.concatenate([w, jnp.zeros(w.shape[:-1] + (1,), w.dtype)], axis=-1)
    return jnp.take(wz, jnp.where(idx < 0, w.shape[-1], idx), axis=-1)


def _cparams(sem):
    return pltpu.CompilerParams(dimension_semantics=sem, vmem_limit_bytes=VMEM_LIMIT)


def _cond_row(i, tm=TM):
    n_p = T_PROMPT // tm
    per_seq = DEC_SEQ // tm
    return jnp.where(i < n_p, 0, 1 + (i - n_p) // per_seq)


def _split3(x):
    hi = x.astype(BF16)
    r1 = x - hi.astype(F32)
    mid = r1.astype(BF16)
    lo = (r1 - mid.astype(F32)).astype(BF16)
    return hi, mid, lo


def _dot(a, b):
    return jnp.dot(a, b, preferred_element_type=F32)


def _dot_nt(a, b):
    return lax.dot_general(a, b, (((1,), (1,)), ((), ())), preferred_element_type=F32)


def _dot_tn(a, b):
    return lax.dot_general(a, b, (((0,), (0,)), ((), ())), preferred_element_type=F32)


def _dot_sel(sel_bf16, x):
    hi, mid, lo = _split3(x)
    return _dot(sel_bf16, hi) + _dot(sel_bf16, mid) + _dot(sel_bf16, lo)


def _dot_x_sel(x, sel_bf16):
    hi, mid, lo = _split3(x)
    return _dot(hi, sel_bf16) + _dot(mid, sel_bf16) + _dot(lo, sel_bf16)


def _dot_hi(a, b):
    ah = a.astype(BF16)
    al = (a - ah.astype(F32)).astype(BF16)
    bh = b.astype(BF16)
    bl = (b - bh.astype(F32)).astype(BF16)
    return _dot(ah, bh) + (_dot(ah, bl) + _dot(al, bh))


def _ada_kernel(c_ref, w_ref, b_ref, o_ref):
    c = c_ref[...]
    cs = c * jax.nn.sigmoid(c)
    o_ref[0] = _dot(cs, w_ref[0]) + b_ref[0]


def _ada_call(cond, w_ada, b_ada):
    nt = 6
    return pl.pallas_call(
        _ada_kernel,
        grid=(DEPTH, nt),
        in_specs=[
            pl.BlockSpec((N_COND, D_MODEL), lambda l, j: (0, 0)),
            pl.BlockSpec((1, D_MODEL, D_MODEL), lambda l, j: (l, 0, j)),
            pl.BlockSpec((1, 1, D_MODEL), lambda l, j: (l, 0, j)),
        ],
        out_specs=pl.BlockSpec((1, N_COND, D_MODEL), lambda l, j: (l, 0, j)),
        out_shape=jax.ShapeDtypeStruct((DEPTH, N_COND, 6 * D_MODEL), F32),
        compiler_params=_cparams(("parallel", "parallel")),
        name="ada_mod",
    )(cond, w_ada, b_ada.reshape(DEPTH, 1, 6 * D_MODEL))


def _modulated_norm(x, g, sc, sh):
    ms = jnp.mean(x * x, axis=-1, keepdims=True)
    y = x * lax.rsqrt(ms + NORM_EPS) * g
    return y * (1.0 + sc) + sh


def _in_kernel(x_ref, sh_ref, sc_ref, g_ref, w_ref, o_ref):
    h = _modulated_norm(x_ref[...], g_ref[...], sc_ref[0], sh_ref[0])
    o_ref[...] = _dot(h.astype(BF16), w_ref[...])


def _in_call(x, mod3, g, w, n_col_tiles):
    n = w.shape[1]
    tn = n // n_col_tiles
    return pl.pallas_call(
        _in_kernel,
        grid=(n_col_tiles, x.shape[0] // TM),
        in_specs=[
            pl.BlockSpec((TM, D_MODEL), lambda j, i: (i, 0)),
            pl.BlockSpec((1, 1, D_MODEL), lambda j, i: (_cond_row(i) * 6 + 0, 0, 0)),
            pl.BlockSpec((1, 1, D_MODEL), lambda j, i: (_cond_row(i) * 6 + 1, 0, 0)),
            pl.BlockSpec((1, D_MODEL), lambda j, i: (0, 0)),
            pl.BlockSpec((D_MODEL, tn), lambda j, i: (0, j)),
        ],
        out_specs=pl.BlockSpec((TM, tn), lambda j, i: (i, j)),
        out_shape=jax.ShapeDtypeStruct((x.shape[0], n), F32),
        compiler_params=_cparams(("parallel", "parallel")),
        name="in_proj",
    )(x, mod3, mod3, g, w)


def _out_kernel(x_ref, m_ref, w_ref, g1_ref, sh_ref, sc_ref, g_ref, xo_ref, h_ref):
    x = x_ref[...] + g1_ref[0] * _dot(m_ref[...], w_ref[...])
    xo_ref[...] = x
    h_ref[...] = _modulated_norm(x, g_ref[...], sc_ref[0], sh_ref[0])


def _out_call(x, m, w_out, mod3, g2):
    spec_tok = pl.BlockSpec((TM, D_MODEL), lambda i: (i, 0))

    def mod_spec(k):
        return pl.BlockSpec((1, 1, D_MODEL), lambda i: (_cond_row(i) * 6 + k, 0, 0))

    return pl.pallas_call(
        _out_kernel,
        grid=(x.shape[0] // TM,),
        in_specs=[
            spec_tok,
            spec_tok,
            pl.BlockSpec((D_MODEL, D_MODEL), lambda i: (0, 0)),
            mod_spec(2), mod_spec(3), mod_spec(4),
            pl.BlockSpec((1, D_MODEL), lambda i: (0, 0)),
        ],
        out_specs=[spec_tok, spec_tok],
        out_shape=[jax.ShapeDtypeStruct(x.shape, F32)] * 2,
        compiler_params=_cparams(("parallel",)),
        name="out_proj",
    )(x, m, w_out, mod3, mod3, mod3, g2)


CHUNK = 64
SUB = 16
NSUB = CHUNK // SUB
NEG_BIG = -1e30
HPS = 4


def _tri(n, lower, strict=False):
    r = lax.broadcasted_iota(jnp.int32, (n, n), 0)
    c = lax.broadcasted_iota(jnp.int32, (n, n), 1)
    if lower:
        return (c < r) if strict else (c <= r)
    return (c > r) if strict else (c >= r)


def _lockstep(gens):
    results = [None] * len(gens)
    active = list(enumerate(gens))
    while active:
        still = []
        for i, g in active:
            try:
                next(g)
                still.append((i, g))
            except StopIteration as stop:
                results[i] = stop.value
        active = still
    return results


def _gla_chunk(q, k, v, la, st, reverse):
    kdim = q.shape[1]
    tri = jnp.where(_tri(CHUNK, not reverse), 1.0, 0.0).astype(BF16)
    b = _dot_sel(tri, la)
    yield
    last = 0 if reverse else CHUNK - 1
    b_last = b[last:last + 1, :]
    o_inter = _dot_nt((q * jnp.exp(b)).astype(BF16), st.astype(BF16))
    kd = k * jnp.exp(b_last - b)
    st_new = st * jnp.exp(b_last) + _dot_tn(v.astype(BF16), kd.astype(BF16))
    yield

    ones_k = jnp.ones((kdim, LANES), BF16)
    jrow = lax.broadcasted_iota(jnp.int32, (SUB, kdim), 0)
    rsel = lax.broadcasted_iota(jnp.int32, (SUB, SUB * SUB), 0)
    csel = lax.broadcasted_iota(jnp.int32, (SUB, SUB * SUB), 1)
    in_grp = jnp.logical_and(csel >= rsel * SUB, csel < rsel * SUB + SUB)
    sel = jnp.where(in_grp, 1.0, 0.0).astype(BF16)
    outs = []
    for blk in range(NSUB):
        r0 = blk * SUB
        q_i, k_i, b_i, v_i = q[r0:r0 + SUB], k[r0:r0 + SUB], b[r0:r0 + SUB], v[r0:r0 + SUB]
        rows = []
        for i in range(SUB):
            keep = (jrow >= i) if reverse else (jrow <= i)
            e = jnp.exp(jnp.where(keep, b_i[i:i + 1, :] - b_i, NEG_BIG))
            rows.append((q_i[i:i + 1, :] * k_i) * e)
        p = jnp.concatenate(rows, axis=0)
        att = _dot(p.astype(BF16), ones_k)
        yield
        zv = att * jnp.concatenate([v_i] * SUB, axis=0)
        o_blk = _dot(sel, zv.astype(BF16))
        yield
        if reverse and blk < NSUB - 1:
            r1 = r0 + SUB
            ref = b[r1:r1 + 1, :]
            qt = q_i * jnp.exp(b_i - ref)
            kt = k[r1:] * jnp.exp(ref - b[r1:])
            a_off = _dot_nt(qt.astype(BF16), kt.astype(BF16))
            yield
            o_blk = o_blk + _dot(a_off.astype(BF16), v[r1:].astype(BF16))
        if (not reverse) and blk > 0:
            ref = b[r0 - 1:r0, :]
            qt = q_i * jnp.exp(b_i - ref)
            kt = k[:r0] * jnp.exp(ref - b[:r0])
            a_off = _dot_nt(qt.astype(BF16), kt.astype(BF16))
            yield
            o_blk = o_blk + _dot(a_off.astype(BF16), v[:r0].astype(BF16))
        outs.append(o_blk)
    return o_inter + jnp.concatenate(outs, axis=0), st_new


def _log_sigmoid(x):
    return -(jnp.maximum(-x, 0.0) + jnp.log1p(jnp.exp(-jnp.abs(x))))


def _head_norm_gate(o, g, gate):
    o = o * lax.rsqrt(jnp.mean(o * o, axis=-1, keepdims=True) + NORM_EPS)
    return (o * g) * (gate * jax.nn.sigmoid(gate))


def _even_kernel(*refs, mode, seq_len, with_state_out, has_s0):
    it = iter(refs)
    if mode == "gla":
        q_ref, k_ref, v_ref, lr_ref, gate_ref, wg_ref, bg_ref, gn_ref = (next(it) for _ in range(8))
    else:
        q_ref, ff_ref, fb_ref, v_ref, gate_ref, lb_ref, gn_ref = (next(it) for _ in range(7))
    s0_ref = next(it) if has_s0 else None
    m_ref = next(it)
    so_ref = next(it) if with_state_out else None
    of_scr, ob_scr = next(it), next(it)
    n_chunks = seq_len // CHUNK
    streams = [(hh, d) for hh in range(HPS) for d in range(2)]

    def chain(c, hh, d, st):
        rows = pl.ds(pl.multiple_of(c * CHUNK, CHUNK), CHUNK)
        cols = slice(hh * LANES, (hh + 1) * LANES)
        if mode == "gla":
            q = q_ref[rows, cols] * (GLA_DK ** -0.5)
            k = k_ref[rows, cols]
            pre = _dot_hi(lr_ref[rows, :], wg_ref[hh, d]) + bg_ref[hh, d]
            yield
            la = _log_sigmoid(pre) / GLA_TAU
        else:
            q = q_ref[rows, cols]
            f = (ff_ref, fb_ref)[d][rows, cols]
            lb = lb_ref[d:d + 1, cols]
            gate = lb + (1.0 - lb) * jax.nn.sigmoid(f)
            la = jnp.log(jnp.maximum(gate, GATE_FLOOR))
            k = (1.0 - lb) * jax.nn.sigmoid(-f)
        o, st_new = yield from _gla_chunk(q, k, v_ref[rows, cols], la, st, d == 1)
        return rows, cols, o, st_new

    def body(c, carry):
        res = _lockstep([chain(n_chunks - 1 - c if d else c, hh, d, carry[i])
                         for i, (hh, d) in enumerate(streams)])
        for (hh, d), (rows, cols, o, _) in zip(streams, res):
            (ob_scr if d else of_scr)[rows, cols] = o
        return tuple(r[3] for r in res)

    if has_s0:
        init = tuple(s0_ref[0, d, hh].T for hh, d in streams)
    else:
        init = tuple(jnp.zeros((LANES, LANES), F32) for _ in streams)
    fin = lax.fori_loop(0, n_chunks, body, init)
    for hh in range(HPS):
        cols = slice(hh * LANES, (hh + 1) * LANES)
        m_ref[:, cols] = _head_norm_gate(of_scr[:, cols] + ob_scr[:, cols], gn_ref[...],
                                         gate_ref[:, cols]).astype(m_ref.dtype)
    if with_state_out:
        for i, (hh, d) in enumerate(streams):
            so_ref[0, d, hh] = fin[i].T


def _even_params(w_gate, b_gate, gla_norm, hg_lb, hg_norm, e):
    wg = jnp.zeros((GLA_HEADS, 2, LANES, LANES), F32)
    bg = jnp.zeros((GLA_HEADS, 2, 1, LANES), F32)
    for h in range(GLA_HEADS):
        for d in range(2):
            wg = wg.at[h, d, d * GLA_RANK:(d + 1) * GLA_RANK, :GLA_DK].set(
                w_gate[d, :, h * GLA_DK:(h + 1) * GLA_DK].astype(F32))
            bg = bg.at[h, d, 0, :GLA_DK].set(b_gate[d, h * GLA_DK:(h + 1) * GLA_DK].astype(F32))
    lbp = jax.nn.softmax(hg_lb.astype(F32), axis=1)
    lb = (jnp.cumsum(lbp, axis=1) - lbp[:, :1])[:, e]
    return {"gla": (wg, bg, gla_norm.astype(F32).reshape(1, LANES)),
            "hgrn": (lb, hg_norm.astype(F32).reshape(1, LANES))}


def _even_call(z, mode, col, params, s0, seq_len, n_seq, row0, with_state_out):
    heads = 4
    blk0 = row0 // seq_len
    wide = HPS * LANES

    def tok(name):
        c0 = col[name] // HPS
        return pl.BlockSpec((seq_len, wide), lambda s, h: (blk0 + s, c0 + h), pipeline_mode=pl.Buffered(1))

    if mode == "gla":
        wg, bg, gn = params
        in_specs = [tok("gq"), tok("gk"), tok("gv"),
                    pl.BlockSpec((seq_len, LANES), lambda s, h: (blk0 + s, col["lr"])),
                    tok("gr"),
                    pl.BlockSpec((HPS, 2, LANES, LANES), lambda s, h: (h, 0, 0, 0)),
                    pl.BlockSpec((HPS, 2, 1, LANES), lambda s, h: (h, 0, 0, 0)),
                    pl.BlockSpec((1, LANES), lambda s, h: (0, 0))]
        args = [z, z, z, z, z, wg, bg, gn]
    else:
        lb, gn = params
        in_specs = [tok("hq"), tok("hf_f"), tok("hf_b"), tok("hi"), tok("hgate"),
                    pl.BlockSpec((2, wide), lambda s, h: (0, h)),
                    pl.BlockSpec((1, LANES), lambda s, h: (0, 0))]
        args = [z, z, z, z, z, lb, gn]
    st_spec = pl.BlockSpec((1, 2, HPS, LANES, LANES), lambda s, h: (s, 0, h, 0, 0))
    has_s0 = s0 is not None
    if has_s0:
        in_specs.append(st_spec)
        args.append(s0)
    out_specs = [pl.BlockSpec((seq_len, wide), lambda s, h: (s, h))]
    out_shape = [jax.ShapeDtypeStruct((n_seq * seq_len, heads * LANES), BF16)]
    if with_state_out:
        out_specs.append(st_spec)
        out_shape.append(jax.ShapeDtypeStruct((n_seq, 2, heads, LANES, LANES), F32))
    res = pl.pallas_call(
        functools.partial(_even_kernel, mode=mode, seq_len=seq_len,
                          with_state_out=with_state_out, has_s0=has_s0),
        grid=(n_seq, heads // HPS),
        in_specs=in_specs,
        out_specs=out_specs,
        out_shape=out_shape,
        scratch_shapes=[pltpu.VMEM((seq_len, wide), F32), pltpu.VMEM((seq_len, wide), F32)],
        compiler_params=_cparams(("parallel", "parallel")),
        name=f"mix_{mode}_{seq_len}",
    )(*args)
    return res


CONV_PAD = SUBLANES


def _conv_kernel(x_ref, w_ref, o_ref, pad_scr, *, seq_len):
    cb = pl.program_id(1)
    zeros = jnp.zeros((CONV_PAD, LANES), F32)
    pad_scr[0:CONV_PAD, :] = zeros
    pad_scr[CONV_PAD + seq_len:2 * CONV_PAD + seq_len, :] = zeros
    pad_scr[CONV_PAD:CONV_PAD + seq_len, :] = x_ref[...]
    acc = jnp.zeros((seq_len, LANES), F32)
    for w in range(DN_CONV):
        acc = acc + pad_scr[pl.ds(CONV_PAD + w - DN_CONV // 2, seq_len), :] * w_ref[w:w + 1, :]
    y = acc * jax.nn.sigmoid(acc)
    yn = y * lax.rsqrt(jnp.sum(y * y, axis=-1, keepdims=True) + NORM_EPS)
    is_q = cb < DN_HEADS
    is_qk = cb < 2 * DN_HEADS
    scale = jnp.where(is_q, DN_DK ** -0.5, 1.0)
    o_ref[...] = jnp.where(is_qk, yn * scale, y)


def _conv_call(z, col0, w, seq_len, n_seq, row0):
    blk0 = row0 // seq_len
    nblk = 3 * DN_HEADS
    return pl.pallas_call(
        functools.partial(_conv_kernel, seq_len=seq_len),
        grid=(n_seq, nblk),
        in_specs=[pl.BlockSpec((seq_len, LANES), lambda s, c: (blk0 + s, col0 + c)),
                  pl.BlockSpec((DN_CONV, LANES), lambda s, c: (0, c))],
        out_specs=pl.BlockSpec((seq_len, LANES), lambda s, c: (s, c)),
        out_shape=jax.ShapeDtypeStruct((n_seq * seq_len, nblk * LANES), F32),
        scratch_shapes=[pltpu.VMEM((seq_len + 2 * CONV_PAD, LANES), F32)],
        compiler_params=_cparams(("parallel", "parallel")),
        name=f"dn_conv_{seq_len}",
    )(z, w)


def _softplus(x):
    return jnp.maximum(x, 0.0) + jnp.log1p(jnp.exp(-jnp.abs(x)))


def _dn_chunk(q, k, v, g, beta, s, reverse):
    c = CHUNK
    lower = not reverse
    tri = jnp.where(_tri(c, lower), 1.0, 0.0).astype(BF16)
    gam = _dot_sel(tri, g)
    yield
    ones_c = jnp.ones((c, c), BF16)
    gam_row = _dot_sel(ones_c, jnp.where(_tri(c, reverse), g[:, :c], 0.0))
    incl = _tri(c, lower)
    strict = _tri(c, lower, strict=True)
    diff = gam[:, :c] - gam_row
    dec = jnp.where(incl, jnp.exp(jnp.where(incl, diff, 0.0)), 0.0)
    kb = k * beta
    k16 = k.astype(BF16)
    m = jnp.where(strict, _dot_nt(kb.astype(BF16), k16) * dec, 0.0)
    yield
    eg = jnp.exp(gam)
    x = jnp.concatenate([v * beta, kb * eg], axis=1)
    p = -m
    n_fac = int(np.log2(c))
    for j in range(n_fac):
        x = x + _dot_hi(p, x)
        if j + 1 < n_fac:
            p = _dot_hi(p, p)
        yield
    u, w = x[:, :LANES], x[:, LANES:]
    s16 = s.astype(BF16)
    v_new = u - _dot(w.astype(BF16), s16)
    att = _dot_nt(q.astype(BF16), k16) * dec
    yield
    o = _dot((q * eg).astype(BF16), s16) + _dot(att.astype(BF16), v_new.astype(BF16))
    last = 0 if reverse else c - 1
    gl = gam[last:last + 1, :]
    s_new = s * jnp.exp(gl) + _dot_tn((k * jnp.exp(gl - gam)).astype(BF16), v_new.astype(BF16))
    return o, s_new


def _dn_kernel(*refs, seq_len, with_state_out, has_s0):
    it = iter(refs)
    q_ref, k_ref, v_ref, ab_ref, gate_ref, alog_ref, dtb_ref, gn_ref = (next(it) for _ in range(8))
    s0_ref = next(it) if has_s0 else None
    m_ref = next(it)
    so_ref = next(it) if with_state_out else None
    of_scr, ob_scr = next(it), next(it)
    h0 = pl.program_id(1) * HPS
    n_chunks = seq_len // CHUNK
    srow = lax.broadcasted_iota(jnp.int32, (LANES, LANES), 0)
    streams = [(hh, d) for hh in range(HPS) for d in range(2)]

    def chain(c, hh, d, s):
        rows = pl.ds(pl.multiple_of(c * CHUNK, CHUNK), CHUNK)
        cols = slice(hh * LANES, (hh + 1) * LANES)
        ab = ab_ref[rows, :]
        sel_a = jnp.where(srow == DN_HEADS * d + h0 + hh, 1.0, 0.0).astype(BF16)
        sel_b = jnp.where(srow == 2 * DN_HEADS + DN_HEADS * d + h0 + hh, 1.0, 0.0).astype(BF16)
        a = _dot_x_sel(ab, sel_a)
        beta = jax.nn.sigmoid(_dot_x_sel(ab, sel_b))
        g = -jnp.exp(alog_ref[hh, d]) * _softplus(a + dtb_ref[hh, d])
        yield
        o, s_new = yield from _dn_chunk(q_ref[rows, cols], k_ref[rows, cols], v_ref[rows, cols],
                                        g, beta, s, d == 1)
        return rows, cols, o, s_new

    def body(c, carry):
        res = _lockstep([chain(n_chunks - 1 - c if d else c, hh, d, carry[i])
                         for i, (hh, d) in enumerate(streams)])
        for (hh, d), (rows, cols, o, _) in zip(streams, res):
            (ob_scr if d else of_scr)[rows, cols] = o
        return tuple(r[3] for r in res)

    if has_s0:
        init = tuple(s0_ref[0, d, hh] for hh, d in streams)
    else:
        init = tuple(jnp.zeros((LANES, LANES), F32) for _ in streams)
    fin = lax.fori_loop(0, n_chunks, body, init)
    for hh in range(HPS):
        cols = slice(hh * LANES, (hh + 1) * LANES)
        m_ref[:, cols] = _head_norm_gate(of_scr[:, cols] + ob_scr[:, cols], gn_ref[...],
                                         gate_ref[:, cols]).astype(m_ref.dtype)
    if with_state_out:
        for i, (hh, d) in enumerate(streams):
            so_ref[0, d, hh] = fin[i]


S5_TC = 128
S5_CB = 4
S5_SW = S5_G * S5_P // S5_CB


def _s5_params(lam_re, lam_im, log_dt, b_re, b_im, c_re, c_im):
    gpb = S5_G // S5_CB
    eye = jnp.eye(gpb, dtype=F32)
    n = jnp.arange(1, S5_TC + 1, dtype=F32)[:, None, None]
    bbs, ccs, aps = [], [], []
    for d in range(2):
        lr = jnp.minimum(lam_re[d].astype(F32), -1e-4)
        li = lam_im[d].astype(F32)
        dt = jnp.exp(log_dt[d].astype(F32))[:, None]
        mag = jnp.exp(lr * dt)
        ar, ai = mag * jnp.cos(li * dt), mag * jnp.sin(li * dt)
        den = lr * lr + li * li
        nr = ar - 1.0
        cr = (nr * lr + ai * li) / den
        ci = (ai * lr - nr * li) / den
        bbr = cr[..., None] * b_re - ci[..., None] * b_im
        bbi = cr[..., None] * b_im + ci[..., None] * b_re
        blk = lambda t: jnp.einsum('bgpc,gh->bgchp', t.reshape(S5_CB, gpb, S5_P, S5_GS), eye).reshape(
            S5_CB, gpb * S5_GS, gpb * S5_P)
        bbs.append(jnp.stack([blk(bbr), blk(bbi)]))
        cblk = lambda t: jnp.einsum('bgcp,gh->bgphc', t.reshape(S5_CB, gpb, S5_GS, S5_P), eye).reshape(
            S5_CB, gpb * S5_P, gpb * S5_GS)
        ccs.append(jnp.stack([cblk(c_re[d].astype(F32)), cblk(c_im[d].astype(F32))]))
        pm = jnp.exp(n * (lr * dt)[None])
        pr = (pm * jnp.cos(n * (li * dt)[None])).reshape(S5_TC, -1)
        pi = (pm * jnp.sin(n * (li * dt)[None])).reshape(S5_TC, -1)
        if d == 1:
            pr, pi = pr[::-1], pi[::-1]
        aps.append(jnp.stack([pr, pi]))
    bb = jnp.stack(bbs, axis=2).astype(BF16)
    cc = jnp.stack(ccs, axis=2).astype(BF16)
    apow = jnp.stack(aps, axis=1)
    return bb, cc, apow


def _s5_scan(xr, xi, pr, pi, reverse):
    tc = xr.shape[0]
    row = lax.broadcasted_iota(jnp.int32, xr.shape, 0)
    s = 1
    while s < tc:
        idx = tc - s if reverse else s - 1
        a_r, a_i = pr[idx:idx + 1, :], pi[idx:idx + 1, :]
        shift = tc - s if reverse else s
        keep = (row < tc - s) if reverse else (row >= s)
        sr = jnp.where(keep, pltpu.roll(xr, shift, 0), 0.0)
        si = jnp.where(keep, pltpu.roll(xi, shift, 0), 0.0)
        xr, xi = xr + (a_r * sr - a_i * si), xi + (a_r * si + a_i * sr)
        s *= 2
    return xr, xi


def _s5_kernel(*refs, seq_len, has_h0):
    it = iter(refs)
    u_ref, bb_ref, cc_ref, ap_ref, dsk_ref = (next(it) for _ in range(5))
    h0r_ref, h0i_ref = (next(it), next(it)) if has_h0 else (None, None)
    y_ref, fr_ref, fi_ref = next(it), next(it), next(it)
    yf_scr, yb_scr = next(it), next(it)
    n_chunks = seq_len // S5_TC

    def run(c, d, carry):
        cr, ci = carry
        rows = pl.ds(pl.multiple_of(c * S5_TC, S5_TC), S5_TC)
        ub = u_ref[rows, :].astype(BF16)
        pr, pi = ap_ref[0, d], ap_ref[1, d]
        xr, xi = _dot(ub, bb_ref[0, 0, d]), _dot(ub, bb_ref[1, 0, d])
        hr, hi = _s5_scan(xr, xi, pr, pi, d == 1)
        hr, hi = hr + (pr * cr - pi * ci), hi + (pr * ci + pi * cr)
        (yf_scr, yb_scr)[d][rows, :] = (_dot(hr.astype(BF16), cc_ref[0, 0, d])
                                         - _dot(hi.astype(BF16), cc_ref[1, 0, d]))
        last = 0 if d == 1 else S5_TC - 1
        return hr[last:last + 1, :], hi[last:last + 1, :]

    def body(c, carry):
        cf, cb = carry
        return run(c, 0, cf), run(n_chunks - 1 - c, 1, cb)

    if has_h0:
        init = tuple((h0r_ref[0, d:d + 1, :], h0i_ref[0, d:d + 1, :]) for d in range(2))
    else:
        z = jnp.zeros((1, S5_SW), F32)
        init = ((z, z), (z, z))
    fin = lax.fori_loop(0, n_chunks, body, init)
    y_ref[...] = dsk_ref[...] * u_ref[...] + (yf_scr[...] + yb_scr[...])
    for d in range(2):
        fr_ref[0, d:d + 1, :] = fin[d][0]
        fi_ref[0, d:d + 1, :] = fin[d][1]


def _s5_call(z, col0, bb, cc, apow, dsk, h0r, h0i, seq_len, n_seq, row0):
    blk0 = row0 // seq_len
    has_h0 = h0r is not None
    st_spec = pl.BlockSpec((1, 2, S5_SW), lambda s, c: (s, 0, c))
    in_specs = [pl.BlockSpec((seq_len, LANES), lambda s, c: (blk0 + s, col0 + c)),
                pl.BlockSpec((2, 1, 2, LANES, S5_SW), lambda s, c: (0, c, 0, 0, 0)),
                pl.BlockSpec((2, 1, 2, S5_SW, LANES), lambda s, c: (0, c, 0, 0, 0)),
                pl.BlockSpec((2, 2, S5_TC, S5_SW), lambda s, c: (0, 0, 0, c)),
                pl.BlockSpec((1, LANES), lambda s, c: (0, c))]
    args = [z, bb, cc, apow, dsk]
    if has_h0:
        in_specs += [st_spec, st_spec]
        args += [h0r, h0i]
    st_shape = jax.ShapeDtypeStruct((n_seq, 2, S5_G * S5_P), F32)
    return pl.pallas_call(
        functools.partial(_s5_kernel, seq_len=seq_len, has_h0=has_h0),
        grid=(n_seq, S5_CB),
        in_specs=in_specs,
        out_specs=[pl.BlockSpec((seq_len, LANES), lambda s, c: (s, c)), st_spec, st_spec],
        out_shape=[jax.ShapeDtypeStruct((n_seq * seq_len, MIX_W), F32), st_shape, st_shape],
        scratch_shapes=[pltpu.VMEM((seq_len, LANES), F32), pltpu.VMEM((seq_len, LANES), F32)],
        compiler_params=_cparams(("parallel", "parallel")),
        name=f"mix_s5_{seq_len}",
    )(*args)


def _glu_kernel(y_ref, w_ref, b_ref, o_ref):
    zz = jax.nn.gelu(y_ref[...])
    o_ref[...] = (zz * jax.nn.sigmoid(_dot(zz.astype(BF16), w_ref[...]) + b_ref[...])).astype(o_ref.dtype)


def _glu_call(y, w, b):
    t = y.shape[0]
    return pl.pallas_call(
        _glu_kernel,
        grid=(t // TM,),
        in_specs=[pl.BlockSpec((TM, MIX_W), lambda i: (i, 0)),
                  pl.BlockSpec((MIX_W, MIX_W), lambda i: (0, 0)),
                  pl.BlockSpec((1, MIX_W), lambda i: (0, 0))],
        out_specs=pl.BlockSpec((TM, MIX_W), lambda i: (i, 0)),
        out_shape=jax.ShapeDtypeStruct((t, MIX_W), BF16),
        compiler_params=_cparams(("parallel",)),
        name="s5_glu",
    )(y, w, b)


PK_HP = 2 * PK_HEADS
PK_SIDE = PK_KEYS
PK_PICKS = PK_HEADS * PK_TOPK
TK_TM = 128
TK_HG = 2
WB_TM = 128
WB_GRP = 16
WB_STRIDE = PK_SIDE + SUBLANES
PM_TM = 512
PM_SLABS = 8


def _score_kernel(h_ref, wq_ref, keys_ref, s_ref):
    q = _dot(h_ref[...].astype(BF16), wq_ref[...])
    for hp in range(PK_HP):
        s_ref[hp] = _dot_nt(keys_ref[hp], q[:, hp * LANES:(hp + 1) * LANES].astype(BF16))


def _score_call(h, wq, keys):
    return pl.pallas_call(
        _score_kernel,
        grid=(h.shape[0] // TM,),
        in_specs=[pl.BlockSpec((TM, D_MODEL), lambda i: (i, 0)),
                  pl.BlockSpec((D_MODEL, PK_HP * LANES), lambda i: (0, 0)),
                  pl.BlockSpec((PK_HP, PK_KEYS, LANES), lambda i: (0, 0, 0))],
        out_specs=pl.BlockSpec((PK_HP, PK_KEYS, TM), lambda i: (0, 0, i)),
        out_shape=jax.ShapeDtypeStruct((PK_HP, PK_KEYS, h.shape[0]), F32),
        compiler_params=_cparams(("parallel",)),
        name="peer_scores",
    )(h, wq, keys)


def _top16_rows(tiles):
    shape = tiles[0].shape
    row = lax.broadcasted_iota(jnp.int32, shape, 0).astype(F32)
    out_row = lax.broadcasted_iota(jnp.int32, (PK_TOPK, shape[1]), 0)

    def take(r, s, vals, idxs):
        m = jnp.max(s, axis=0, keepdims=True)
        am = jnp.min(jnp.where(s == m, row, float(shape[0])), axis=0, keepdims=True)
        at = out_row == r
        return jnp.where(row == am, -jnp.inf, s), jnp.where(at, m, vals), jnp.where(at, am, idxs)

    def body(r, carry):
        out = ()
        for i in range(len(tiles)):
            out += take(r, *carry[3 * i:3 * i + 3])
        return out

    zero = jnp.zeros((PK_TOPK, shape[1]), F32)
    init = ()
    for s in tiles:
        init += (s, zero, zero)
    res = lax.fori_loop(0, PK_TOPK, body, init)
    return [(res[3 * i + 1], res[3 * i + 2]) for i in range(len(tiles))]


def _pair_candidates(v1, v2):
    half = PK_TOPK // 2
    n = v1.shape[1]
    rows = [jnp.broadcast_to(v1[0:1], (PK_TOPK, n)) + v2]
    rows += [jnp.broadcast_to(v1[a:a + 1], (half, n)) + v2[:half] for a in range(1, half)]
    rows.append(v1[half:] + jnp.broadcast_to(v2[0:1], (half, n)))
    return jnp.concatenate(rows, axis=0)


def _topk_kernel(s_ref, i1_ref, i2_ref, g_ref, p1_scr, p2_scr, g_scr):
    n = TK_TM
    zero16 = jnp.zeros((PK_TOPK, n), F32)
    rank = lax.broadcasted_iota(jnp.int32, (PK_TOPK, n), 0)
    rank_f = rank.astype(F32)
    flat = _pair_candidates(rank_f * PK_TOPK, rank_f)
    out_row = rank

    def heads(hg, carry):
        tops = []
        for j in range(TK_HG):
            tops += _top16_rows([s_ref[2 * (TK_HG * hg + j) + i] for i in range(2)])
        c1s, c2s, init = [], [], ()
        for j in range(TK_HG):
            (v1, a1), (v2, a2) = tops[2 * j], tops[2 * j + 1]
            c1s.append(_pair_candidates(a1, zero16))
            c2s.append(_pair_candidates(zero16, a2))
            init += (_pair_candidates(v1, v2), zero16, zero16, zero16)

        def pick(r, c1, c2, cand, best, p1, p2):
            m = jnp.max(cand, axis=0, keepdims=True)
            pos = jnp.min(jnp.where(cand == m, flat, float(PK_TOPK * PK_TOPK)), axis=0, keepdims=True)
            hit = flat == pos
            e1 = jnp.max(jnp.where(hit, c1, -1.0), axis=0, keepdims=True)
            e2 = jnp.max(jnp.where(hit, c2, -1.0), axis=0, keepdims=True)
            at = out_row == r
            return (jnp.where(hit, -jnp.inf, cand), jnp.where(at, m, best),
                    jnp.where(at, e1, p1), jnp.where(at, e2, p2))

        def body(r, carry):
            out = ()
            for j in range(TK_HG):
                out += pick(r, c1s[j], c2s[j], *carry[4 * j:4 * j + 4])
            return out

        res = lax.fori_loop(0, PK_TOPK, body, init)
        for j in range(TK_HG):
            _, best, p1, p2 = res[4 * j:4 * j + 4]
            ex = jnp.exp(best - jnp.max(best, axis=0, keepdims=True))
            rows = pl.ds(pl.multiple_of((hg * TK_HG + j) * PK_TOPK, PK_TOPK), PK_TOPK)
            p1_scr[rows, :] = p1
            p2_scr[rows, :] = p2
            g_scr[rows, :] = ex / jnp.sum(ex, axis=0, keepdims=True)
        return carry

    lax.fori_loop(0, PK_HEADS // TK_HG, heads, 0)
    i1_ref[...] = p1_scr[...].T
    i2_ref[...] = p2_scr[...].T
    g_ref[...] = g_scr[...].T


def _topk_call(s):
    t = s.shape[2]
    spec = pl.BlockSpec((TK_TM, PK_PICKS), lambda i: (i, 0))
    shp = jax.ShapeDtypeStruct((t, PK_PICKS), F32)
    return pl.pallas_call(
        _topk_kernel,
        grid=(t // TK_TM,),
        in_specs=[pl.BlockSpec((PK_HP, PK_KEYS, TK_TM), lambda i: (0, 0, i))],
        out_specs=[spec, spec, spec],
        out_shape=[shp, shp, shp],
        scratch_shapes=[pltpu.VMEM((PK_PICKS, TK_TM), F32)] * 3,
        compiler_params=_cparams(("parallel",)),
        name="peer_topk",
    )(s)


def _wbuild_kernel(i1_ref, i2_ref, g_ref, w_ref, stage):
    sub = lax.broadcasted_iota(jnp.int32, (PK_SIDE, PK_PICKS), 0).astype(F32)

    def group(gi, carry):
        t0 = pl.multiple_of(gi * WB_GRP, WB_GRP)
        for tt in range(WB_GRP):
            r1 = i1_ref[pl.ds(t0 + tt, 1), :]
            r2 = i2_ref[pl.ds(t0 + tt, 1), :]
            rg = g_ref[pl.ds(t0 + tt, 1), :]
            p1t = jnp.where(sub == r1, 1.0, 0.0).astype(BF16)
            q2t = jnp.where(sub == r2, rg, 0.0).astype(BF16)
            stage[tt * WB_STRIDE:tt * WB_STRIDE + PK_SIDE, :] = _dot_nt(p1t, q2t)
        for i1 in range(PK_SIDE):
            lo = stage[pl.ds(i1, SUBLANES, stride=WB_STRIDE), :]
            hi = stage[pl.ds(i1 + SUBLANES * WB_STRIDE, SUBLANES, stride=WB_STRIDE), :]
            w_ref[0, i1, pl.ds(t0, WB_GRP), :] = jnp.concatenate([lo, hi], axis=0).astype(BF16)
        return carry

    lax.fori_loop(0, WB_TM // WB_GRP, group, 0)


def _wbuild_call(i1, i2, g):
    spec = pl.BlockSpec((WB_TM, PK_PICKS), lambda i: (i, 0))
    return pl.pallas_call(
        _wbuild_kernel,
        grid=(i1.shape[0] // WB_TM,),
        in_specs=[spec, spec, spec],
        out_specs=pl.BlockSpec((1, PK_SIDE, WB_TM, PK_SIDE), lambda i: (i, 0, 0, 0)),
        out_shape=jax.ShapeDtypeStruct((i1.shape[0] // WB_TM, PK_SIDE, WB_TM, PK_SIDE), BF16),
        scratch_shapes=[pltpu.VMEM((WB_GRP * WB_STRIDE, PK_SIDE), F32)],
        compiler_params=_cparams(("parallel",)),
        name="peer_route",
    )(i1, i2, g)


def _peer_kernel(h_ref, u_ref, v_ref, w_ref, x_ref, g2_ref, o_ref, acc):
    j = pl.program_id(1)

    @pl.when(j == 0)
    def _():
        acc[...] = jnp.zeros_like(acc)

    a = _dot_nt(h_ref[...], u_ref[...])
    nsub = PM_TM // WB_TM
    w = jnp.concatenate(
        [jnp.concatenate([w_ref[n, sl] for sl in range(PM_SLABS)], axis=1) for n in range(nsub)], axis=0)
    acc[...] += _dot((jax.nn.gelu(a) * w.astype(F32)).astype(BF16), v_ref[...])

    @pl.when(j == pl.num_programs(1) - 1)
    def _():
        o_ref[...] = x_ref[...] + g2_ref[0] * acc[...]


def _peer_call(h16, u16, v16, w, x, mod3):
    ne = PM_SLABS * PK_SIDE
    tok = pl.BlockSpec((PM_TM, D_MODEL), lambda i, j: (i, 0))
    return pl.pallas_call(
        _peer_kernel,
        grid=(x.shape[0] // PM_TM, PK_SIDE // PM_SLABS),
        in_specs=[tok,
                  pl.BlockSpec((ne, D_MODEL), lambda i, j: (j, 0)),
                  pl.BlockSpec((ne, D_MODEL), lambda i, j: (j, 0)),
                  pl.BlockSpec((PM_TM // WB_TM, PM_SLABS, WB_TM, PK_SIDE), lambda i, j: (i, j, 0, 0)),
                  tok,
                  pl.BlockSpec((1, 1, D_MODEL), lambda i, j: (_cond_row(i) * 6 + 5, 0, 0))],
        out_specs=tok,
        out_shape=jax.ShapeDtypeStruct(x.shape, F32),
        scratch_shapes=[pltpu.VMEM((PM_TM, D_MODEL), F32)],
        compiler_params=_cparams(("parallel", "arbitrary")),
        name="peer_experts",
    )(h16, u16, v16, w, x, mod3)


def _final_norm_kernel(x_ref, g_ref, o_ref):
    x = x_ref[...]
    o_ref[...] = x * lax.rsqrt(jnp.mean(x * x, axis=-1, keepdims=True) + NORM_EPS) * g_ref[...]


def _final_norm_call(x, g):
    spec = pl.BlockSpec((TM, D_MODEL), lambda i: (i, 0))
    return pl.pallas_call(
        _final_norm_kernel,
        grid=(x.shape[0] // TM,),
        in_specs=[spec, pl.BlockSpec((1, D_MODEL), lambda i: (0, 0))],
        out_specs=spec,
        out_shape=jax.ShapeDtypeStruct(x.shape, F32),
        compiler_params=_cparams(("parallel",)),
        name="final_norm",
    )(x, g)


def _dn_params(conv_w, a_log, dt_bias, dn_norm):
    bc = lambda p: jnp.broadcast_to(p.astype(F32).T[:, :, None, None], (DN_HEADS, 2, 1, LANES))
    return {"conv": conv_w.astype(F32), "alog": bc(a_log), "dtb": bc(dt_bias),
            "gn": dn_norm.astype(F32).reshape(1, LANES)}


def _dn_call(z, qkv, col, alog, dtb, gn, s0, seq_len, n_seq, row0, with_state_out):
    blk0 = row0 // seq_len
    hd = DN_HEADS
    nhb = hd // HPS
    wide = HPS * LANES
    gate0 = col["gate"] // HPS
    one = pl.Buffered(1)
    in_specs = [pl.BlockSpec((seq_len, wide), lambda s, h: (s, h), pipeline_mode=one),
                pl.BlockSpec((seq_len, wide), lambda s, h: (s, nhb + h), pipeline_mode=one),
                pl.BlockSpec((seq_len, wide), lambda s, h: (s, 2 * nhb + h), pipeline_mode=one),
                pl.BlockSpec((seq_len, LANES), lambda s, h: (blk0 + s, col["ab"])),
                pl.BlockSpec((seq_len, wide), lambda s, h: (blk0 + s, gate0 + h), pipeline_mode=one),
                pl.BlockSpec((HPS, 2, 1, LANES), lambda s, h: (h, 0, 0, 0)),
                pl.BlockSpec((HPS, 2, 1, LANES), lambda s, h: (h, 0, 0, 0)),
                pl.BlockSpec((1, LANES), lambda s, h: (0, 0))]
    args = [qkv, qkv, qkv, z, z, alog, dtb, gn]
    st_spec = pl.BlockSpec((1, 2, HPS, LANES, LANES), lambda s, h: (s, 0, h, 0, 0))
    has_s0 = s0 is not None
    if has_s0:
        in_specs.append(st_spec)
        args.append(s0)
    out_specs = [pl.BlockSpec((seq_len, wide), lambda s, h: (s, h))]
    out_shape = [jax.ShapeDtypeStruct((n_seq * seq_len, hd * LANES), BF16)]
    if with_state_out:
        out_specs.append(st_spec)
        out_shape.append(jax.ShapeDtypeStruct((n_seq, 2, hd, LANES, LANES), F32))
    return pl.pallas_call(
        functools.partial(_dn_kernel, seq_len=seq_len, with_state_out=with_state_out, has_s0=has_s0),
        grid=(n_seq, nhb),
        in_specs=in_specs,
        out_specs=out_specs,
        out_shape=out_shape,
        scratch_shapes=[pltpu.VMEM((seq_len, wide), F32), pltpu.VMEM((seq_len, wide), F32)],
        compiler_params=_cparams(("parallel", "parallel")),
        name=f"mix_dn_{seq_len}",
    )(*args)


def _grid_col_major(t):
    n, d = t.shape
    rows = DEC_SEQ // GRID_W
    return t.reshape(DEC_BATCH, rows, GRID_W, d).transpose(0, 2, 1, 3).reshape(n, d)


def _grid_row_major(t):
    n, d = t.shape
    rows = DEC_SEQ // GRID_W
    return t.reshape(DEC_BATCH, GRID_W, rows, d).transpose(0, 2, 1, 3).reshape(n, d)


def _both_groups(fn, s0_sample):
    res_p = fn(None, SEQ, BATCH, 0, True)
    res_s = fn(s0_sample, DEC_SEQ, DEC_BATCH, T_PROMPT, False)
    return jnp.concatenate([res_p[0], res_s[0]], axis=0), res_p[1]


def kernel(x_prompt, x_sample, state_gla, state_hgrn, state_s5_re, state_s5_im, state_dn, c, c_ctx, w_ada, b_ada, norm1, norm2, w_in_even, w_gla_gate, b_gla_gate, gla_norm, hg_lb, hg_norm, w_in_odd, s5_lam_re, s5_lam_im, s5_log_dt, s5_b_re, s5_b_im, s5_c_re, s5_c_im, s5_d, s5_w_glu, s5_b_glu, dn_conv, dn_a_log, dn_dt_bias, dn_norm, w_out, pk_w_q, pk_keys, pk_u, pk_v, final_norm):
    x = jnp.concatenate([x_prompt.reshape(T_PROMPT, D_MODEL), x_sample.reshape(T_SAMPLE, D_MODEL)], axis=0)
    cond = jnp.zeros((N_COND, D_MODEL), F32).at[0].set(c_ctx.astype(F32)).at[1:1 + DEC_BATCH].set(c.astype(F32))
    mod = _ada_call(cond, w_ada.astype(F32), b_ada.astype(F32))

    out_g, out_h, out_re, out_im, out_dn = [], [], [], [], []
    for l in range(DEPTH):
        mod3 = mod[l].reshape(N_COND * 6, 1, D_MODEL)
        g1 = norm1[l].astype(F32).reshape(1, D_MODEL)
        if l % 2 == 0:
            e = l // 2
            w_in = _relayout_cols(w_in_even[e], EVEN_IDX).astype(BF16)
            z = _in_call(x, mod3, g1, w_in, 2)
            prm = _even_params(w_gla_gate[e], b_gla_gate[e], gla_norm[e], hg_lb, hg_norm[e], e)
            s0g = jnp.pad(state_gla[:, e].astype(F32), ((0, 0),) * 3 + ((0, HG_DK - GLA_DK), (0, 0)))
            m_g, st_g = _both_groups(
                lambda s0, L, n, r0, so: _even_call(z, "gla", EVEN_COL, prm["gla"], s0, L, n, r0, so), s0g)
            m_h, st_h = _both_groups(
                lambda s0, L, n, r0, so: _even_call(z, "hgrn", EVEN_COL, prm["hgrn"], s0, L, n, r0, so),
                state_hgrn[:, e].astype(F32))
            m = jnp.concatenate([m_g, m_h], axis=1)
            out_g.append(st_g[..., :GLA_DK, :])
            out_h.append(st_h)
        else:
            j = l // 2
            x_in = jnp.concatenate([x[:T_PROMPT], _grid_col_major(x[T_PROMPT:])], axis=0)
            w_in = _relayout_cols(w_in_odd[j], ODD_IDX).astype(BF16)
            z = _in_call(x_in, mod3, g1, w_in, 3)
            bb, cc, apow = _s5_params(s5_lam_re[j], s5_lam_im[j], s5_log_dt[j], s5_b_re[j], s5_b_im[j],
                                      s5_c_re[j], s5_c_im[j])
            dsk = s5_d[j].astype(F32).reshape(1, MIX_W)
            y_p, fr, fi = _s5_call(z, ODD_COL["u"], bb, cc, apow, dsk, None, None, SEQ, BATCH, 0)
            y_s, _, _ = _s5_call(z, ODD_COL["u"], bb, cc, apow, dsk,
                                 state_s5_re[:, j].astype(F32).reshape(DEC_BATCH, 2, -1),
                                 state_s5_im[:, j].astype(F32).reshape(DEC_BATCH, 2, -1),
                                 DEC_SEQ, DEC_BATCH, T_PROMPT)
            m_s5 = _glu_call(jnp.concatenate([y_p, y_s], axis=0), s5_w_glu[j].astype(BF16),
                             s5_b_glu[j].astype(F32).reshape(1, MIX_W))
            dp = _dn_params(dn_conv[j], dn_a_log[j], dn_dt_bias[j], dn_norm[j])

            def dn(s0, L, n, r0, so):
                qkv = _conv_call(z, ODD_COL["qkv"], dp["conv"], L, n, r0)
                return _dn_call(z, qkv, ODD_COL, dp["alog"], dp["dtb"], dp["gn"], s0, L, n, r0, so)

            m_dn, st_dn = _both_groups(dn, state_dn[:, j].astype(F32))
            m = jnp.concatenate([m_s5, m_dn], axis=1)
            m = jnp.concatenate([m[:T_PROMPT], _grid_row_major(m[T_PROMPT:])], axis=0)
            out_re.append(fr.reshape(BATCH, 2, S5_G, S5_P))
            out_im.append(fi.reshape(BATCH, 2, S5_G, S5_P))
            out_dn.append(st_dn)
        x, h2 = _out_call(x, m, w_out[l].astype(BF16), mod3, norm2[l].astype(F32).reshape(1, D_MODEL))
        s = _score_call(h2, pk_w_q[l].astype(BF16), pk_keys[l].reshape(PK_HP, PK_KEYS, PK_DQ // 2).astype(BF16))
        i1, i2, gate = _topk_call(s)
        w = _wbuild_call(i1, i2, gate)
        x = _peer_call(h2.astype(BF16), pk_u[l].astype(BF16), pk_v[l].astype(BF16), w, x, mod3)
    y = _final_norm_call(x, final_norm.astype(F32).reshape(1, D_MODEL))
    return (y[:T_PROMPT].reshape(BATCH, SEQ, D_MODEL), y[T_PROMPT:].reshape(DEC_BATCH, DEC_SEQ, D_MODEL),
            jnp.stack(out_g, axis=1), jnp.stack(out_h, axis=1), jnp.stack(out_re, axis=1),
            jnp.stack(out_im, axis=1), jnp.stack(out_dn, axis=1))
```

```python
import functools

import numpy as np
import jax
import jax.numpy as jnp
from jax import lax
from jax.experimental import pallas as pl
from jax.experimental.pallas import tpu as pltpu

F32 = jnp.float32
BF16 = jnp.bfloat16

D_MODEL = 1024
BATCH = 32
SEQ = 256
DEPTH = 4
DEC_BATCH = 8
DEC_SEQ = 2048
GRID_W = 64
N_EVEN = 2
N_ODD = 2
MIX_W = 512
GLA_HEADS = 4
GLA_DK = 64
GLA_DV = 128
GLA_RANK = 16
GLA_TAU = 16.0
HG_HEADS = 4
HG_DK = 128
HG_DV = 128
S5_GS = 16
S5_G = 32
S5_P = 64
DN_HEADS = 4
DN_DK = 128
DN_DV = 128
DN_CONV = 5
PK_HEADS = 8
PK_KEYS = 128
PK_DQ = 256
PK_TOPK = 16
NORM_EPS = 1e-6
GATE_FLOOR = 1e-30

T_PROMPT = BATCH * SEQ
T_SAMPLE = DEC_BATCH * DEC_SEQ
T_ALL = T_PROMPT + T_SAMPLE
N_COND = 16

LANES = 128
SUBLANES = 8
VMEM_LIMIT = 56 * 1024 * 1024

TM = 512


def _layout(pieces):
    idx, col = [], {}
    for name, src, width in pieces:
        assert len(idx) % LANES == 0 and width % LANES == 0
        col[name] = len(idx) // LANES
        src = list(src)
        idx += src + [-1] * (width - len(src))
    return np.asarray(idx, np.int32), col


def _even_layout():
    o = np.cumsum((0, 256, 256, 512, 16, 16, 512, 512, 512, 512, 512, 512))
    rng = lambda i: range(o[i], o[i + 1])
    pieces = []
    for nm, base in (("gq", o[0]), ("gk", o[1])):
        for h in range(GLA_HEADS):
            pieces.append((nm if h == 0 else f"{nm}{h}", range(base + 64 * h, base + 64 * h + 64), LANES))
    pieces += [("gv", rng(2), 512), ("gr", rng(5), 512),
               ("hq", rng(6), 512), ("hf_f", rng(7), 512), ("hf_b", rng(8), 512),
               ("hi", rng(9), 512), ("hgate", rng(10), 512),
               ("lr", list(rng(3)) + list(rng(4)), LANES), ("pad", [], LANES)]
    return _layout(pieces)


def _odd_layout():
    o = np.cumsum((0, 512, 1536, 4, 4, 4, 4, 512))
    rng = lambda i: range(o[i], o[i + 1])
    return _layout([("u", rng(0), 512), ("qkv", rng(1), 1536), ("gate", rng(6), 512),
                    ("ab", range(o[2], o[6]), LANES)])


EVEN_IDX, EVEN_COL = _even_layout()
ODD_IDX, ODD_COL = _odd_layout()


def _relayout_cols(w, idx):
    wz = jnp.concatenate([w, jnp.zeros(w.shape[:-1] + (1,), w.dtype)], axis=-1)
    return jnp.take(wz, jnp.where(idx < 0, w.shape[-1], idx), axis=-1)


def _cparams(sem):
    return pltpu.CompilerParams(dimension_semantics=sem, vmem_limit_bytes=VMEM_LIMIT)


def _cond_row(i, tm=TM):
    n_p = T_PROMPT // tm
    per_seq = DEC_SEQ // tm
    return jnp.where(i < n_p, 0, 1 + (i - n_p) // per_seq)


def _split3(x):
    hi = x.astype(BF16)
    r1 = x - hi.astype(F32)
    mid = r1.astype(BF16)
    lo = (r1 - mid.astype(F32)).astype(BF16)
    return hi, mid, lo


def _dot(a, b):
    return jnp.dot(a, b, preferred_element_type=F32)


def _dot_nt(a, b):
    return lax.dot_general(a, b, (((1,), (1,)), ((), ())), preferred_element_type=F32)


def _dot_tn(a, b):
    return lax.dot_general(a, b, (((0,), (0,)), ((), ())), preferred_element_type=F32)


def _dot_sel(sel_bf16, x):
    hi, mid, lo = _split3(x)
    return _dot(sel_bf16, hi) + _dot(sel_bf16, mid) + _dot(sel_bf16, lo)


def _dot_x_sel(x, sel_bf16):
    hi, mid, lo = _split3(x)
    return _dot(hi, sel_bf16) + _dot(mid, sel_bf16) + _dot(lo, sel_bf16)


def _dot_hi(a, b):
    ah = a.astype(BF16)
    al = (a - ah.astype(F32)).astype(BF16)
    bh = b.astype(BF16)
    bl = (b - bh.astype(F32)).astype(BF16)
    return _dot(ah, bh) + (_dot(ah, bl) + _dot(al, bh))


def _ada_kernel(c_ref, w_ref, b_ref, o_ref):
    c = c_ref[...]
    cs = c * jax.nn.sigmoid(c)
    o_ref[0] = _dot(cs, w_ref[0]) + b_ref[0]


def _ada_call(cond, w_ada, b_ada):
    nt = 6
    return pl.pallas_call(
        _ada_kernel,
        grid=(DEPTH, nt),
        in_specs=[
            pl.BlockSpec((N_COND, D_MODEL), lambda l, j: (0, 0)),
            pl.BlockSpec((1, D_MODEL, D_MODEL), lambda l, j: (l, 0, j)),
            pl.BlockSpec((1, 1, D_MODEL), lambda l, j: (l, 0, j)),
        ],
        out_specs=pl.BlockSpec((1, N_COND, D_MODEL), lambda l, j: (l, 0, j)),
        out_shape=jax.ShapeDtypeStruct((DEPTH, N_COND, 6 * D_MODEL), F32),
        compiler_params=_cparams(("parallel", "parallel")),
        name="ada_mod",
    )(cond, w_ada, b_ada.reshape(DEPTH, 1, 6 * D_MODEL))


def _modulated_norm(x, g, sc, sh):
    ms = jnp.mean(x * x, axis=-1, keepdims=True)
    y = x * lax.rsqrt(ms + NORM_EPS) * g
    return y * (1.0 + sc) + sh


def _in_kernel(x_ref, sh_ref, sc_ref, g_ref, w_ref, o_ref):
    h = _modulated_norm(x_ref[...], g_ref[...], sc_ref[0], sh_ref[0])
    o_ref[...] = _dot(h.astype(BF16), w_ref[...])


def _in_call(x, mod3, g, w, n_col_tiles):
    n = w.shape[1]
    tn = n // n_col_tiles
    return pl.pallas_call(
        _in_kernel,
        grid=(n_col_tiles, x.shape[0] // TM),
        in_specs=[
            pl.BlockSpec((TM, D_MODEL), lambda j, i: (i, 0)),
            pl.BlockSpec((1, 1, D_MODEL), lambda j, i: (_cond_row(i) * 6 + 0, 0, 0)),
            pl.BlockSpec((1, 1, D_MODEL), lambda j, i: (_cond_row(i) * 6 + 1, 0, 0)),
            pl.BlockSpec((1, D_MODEL), lambda j, i: (0, 0)),
            pl.BlockSpec((D_MODEL, tn), lambda j, i: (0, j)),
        ],
        out_specs=pl.BlockSpec((TM, tn), lambda j, i: (i, j)),
        out_shape=jax.ShapeDtypeStruct((x.shape[0], n), F32),
        compiler_params=_cparams(("parallel", "parallel")),
        name="in_proj",
    )(x, mod3, mod3, g, w)


def _out_kernel(x_ref, m_ref, w_ref, g1_ref, sh_ref, sc_ref, g_ref, xo_ref, h_ref):
    x = x_ref[...] + g1_ref[0] * _dot(m_ref[...], w_ref[...])
    xo_ref[...] = x
    h_ref[...] = _modulated_norm(x, g_ref[...], sc_ref[0], sh_ref[0])


def _out_call(x, m, w_out, mod3, g2):
    spec_tok = pl.BlockSpec((TM, D_MODEL), lambda i: (i, 0))

    def mod_spec(k):
        return pl.BlockSpec((1, 1, D_MODEL), lambda i: (_cond_row(i) * 6 + k, 0, 0))

    return pl.pallas_call(
        _out_kernel,
        grid=(x.shape[0] // TM,),
        in_specs=[
            spec_tok,
            spec_tok,
            pl.BlockSpec((D_MODEL, D_MODEL), lambda i: (0, 0)),
            mod_spec(2), mod_spec(3), mod_spec(4),
            pl.BlockSpec((1, D_MODEL), lambda i: (0, 0)),
        ],
        out_specs=[spec_tok, spec_tok],
        out_shape=[jax.ShapeDtypeStruct(x.shape, F32)] * 2,
        compiler_params=_cparams(("parallel",)),
        name="out_proj",
    )(x, m, w_out, mod3, mod3, mod3, g2)


CHUNK = 64
SUB = 16
NSUB = CHUNK // SUB
NEG_BIG = -1e30
HPS = 4


def _tri(n, lower, strict=False):
    r = lax.broadcasted_iota(jnp.int32, (n, n), 0)
    c = lax.broadcasted_iota(jnp.int32, (n, n), 1)
    if lower:
        return (c < r) if strict else (c <= r)
    return (c > r) if strict else (c >= r)


def _lockstep(gens):
    results = [None] * len(gens)
    active = list(enumerate(gens))
    while active:
        still = []
        for i, g in active:
            try:
                next(g)
                still.append((i, g))
            except StopIteration as stop:
                results[i] = stop.value
        active = still
    return results


def _gla_chunk(q, k, v, la, st, reverse):
    kdim = q.shape[1]
    tri = jnp.where(_tri(CHUNK, not reverse), 1.0, 0.0).astype(BF16)
    b = _dot_sel(tri, la)
    yield
    last = 0 if reverse else CHUNK - 1
    b_last = b[last:last + 1, :]
    o_inter = _dot_nt((q * jnp.exp(b)).astype(BF16), st.astype(BF16))
    kd = k * jnp.exp(b_last - b)
    st_new = st * jnp.exp(b_last) + _dot_tn(v.astype(BF16), kd.astype(BF16))
    yield

    ones_k = jnp.ones((kdim, LANES), BF16)
    jrow = lax.broadcasted_iota(jnp.int32, (SUB, kdim), 0)
    rsel = lax.broadcasted_iota(jnp.int32, (SUB, SUB * SUB), 0)
    csel = lax.broadcasted_iota(jnp.int32, (SUB, SUB * SUB), 1)
    in_grp = jnp.logical_and(csel >= rsel * SUB, csel < rsel * SUB + SUB)
    sel = jnp.where(in_grp, 1.0, 0.0).astype(BF16)
    outs = []
    for blk in range(NSUB):
        r0 = blk * SUB
        q_i, k_i, b_i, v_i = q[r0:r0 + SUB], k[r0:r0 + SUB], b[r0:r0 + SUB], v[r0:r0 + SUB]
        rows = []
        for i in range(SUB):
            keep = (jrow >= i) if reverse else (jrow <= i)
            e = jnp.exp(jnp.where(keep, b_i[i:i + 1, :] - b_i, NEG_BIG))
            rows.append((q_i[i:i + 1, :] * k_i) * e)
        p = jnp.concatenate(rows, axis=0)
        att = _dot(p.astype(BF16), ones_k)
        yield
        zv = att * jnp.concatenate([v_i] * SUB, axis=0)
        o_blk = _dot(sel, zv.astype(BF16))
        yield
        if reverse and blk < NSUB - 1:
            r1 = r0 + SUB
            ref = b[r1:r1 + 1, :]
            qt = q_i * jnp.exp(b_i - ref)
            kt = k[r1:] * jnp.exp(ref - b[r1:])
            a_off = _dot_nt(qt.astype(BF16), kt.astype(BF16))
            yield
            o_blk = o_blk + _dot(a_off.astype(BF16), v[r1:].astype(BF16))
        if (not reverse) and blk > 0:
            ref = b[r0 - 1:r0, :]
            qt = q_i * jnp.exp(b_i - ref)
            kt = k[:r0] * jnp.exp(ref - b[:r0])
            a_off = _dot_nt(qt.astype(BF16), kt.astype(BF16))
            yield
            o_blk = o_blk + _dot(a_off.astype(BF16), v[:r0].astype(BF16))
        outs.append(o_blk)
    return o_inter + jnp.concatenate(outs, axis=0), st_new


def _log_sigmoid(x):
    return -(jnp.maximum(-x, 0.0) + jnp.log1p(jnp.exp(-jnp.abs(x))))


def _head_norm_gate(o, g, gate):
    o = o * lax.rsqrt(jnp.mean(o * o, axis=-1, keepdims=True) + NORM_EPS)
    return (o * g) * (gate * jax.nn.sigmoid(gate))


def _even_kernel(*refs, mode, seq_len, with_state_out, has_s0):
    it = iter(refs)
    if mode == "gla":
        q_ref, k_ref, v_ref, lr_ref, gate_ref, wg_ref, bg_ref, gn_ref = (next(it) for _ in range(8))
    else:
        q_ref, ff_ref, fb_ref, v_ref, gate_ref, lb_ref, gn_ref = (next(it) for _ in range(7))
    s0_ref = next(it) if has_s0 else None
    m_ref = next(it)
    so_ref = next(it) if with_state_out else None
    of_scr, ob_scr = next(it), next(it)
    n_chunks = seq_len // CHUNK
    streams = [(hh, d) for hh in range(HPS) for d in range(2)]

    def chain(c, hh, d, st):
        rows = pl.ds(pl.multiple_of(c * CHUNK, CHUNK), CHUNK)
        cols = slice(hh * LANES, (hh + 1) * LANES)
        if mode == "gla":
            q = q_ref[rows, cols] * (GLA_DK ** -0.5)
            k = k_ref[rows, cols]
            pre = _dot_hi(lr_ref[rows, :], wg_ref[hh, d]) + bg_ref[hh, d]
            yield
            la = _log_sigmoid(pre) / GLA_TAU
        else:
            q = q_ref[rows, cols]
            f = (ff_ref, fb_ref)[d][rows, cols]
            lb = lb_ref[d:d + 1, cols]
            gate = lb + (1.0 - lb) * jax.nn.sigmoid(f)
            la = jnp.log(jnp.maximum(gate, GATE_FLOOR))
            k = (1.0 - lb) * jax.nn.sigmoid(-f)
        o, st_new = yield from _gla_chunk(q, k, v_ref[rows, cols], la, st, d == 1)
        return rows, cols, o, st_new

    def body(c, carry):
        res = _lockstep([chain(n_chunks - 1 - c if d else c, hh, d, carry[i])
                         for i, (hh, d) in enumerate(streams)])
        for (hh, d), (rows, cols, o, _) in zip(streams, res):
            (ob_scr if d else of_scr)[rows, cols] = o
        return tuple(r[3] for r in res)

    if has_s0:
        init = tuple(s0_ref[0, d, hh].T for hh, d in streams)
    else:
        init = tuple(jnp.zeros((LANES, LANES), F32) for _ in streams)
    fin = lax.fori_loop(0, n_chunks, body, init)
    for hh in range(HPS):
        cols = slice(hh * LANES, (hh + 1) * LANES)
        m_ref[:, cols] = _head_norm_gate(of_scr[:, cols] + ob_scr[:, cols], gn_ref[...],
                                         gate_ref[:, cols]).astype(m_ref.dtype)
    if with_state_out:
        for i, (hh, d) in enumerate(streams):
            so_ref[0, d, hh] = fin[i].T


def _even_params(w_gate, b_gate, gla_norm, hg_lb, hg_norm, e):
    wg = jnp.zeros((GLA_HEADS, 2, LANES, LANES), F32)
    bg = jnp.zeros((GLA_HEADS, 2, 1, LANES), F32)
    for h in range(GLA_HEADS):
        for d in range(2):
            wg = wg.at[h, d, d * GLA_RANK:(d + 1) * GLA_RANK, :GLA_DK].set(
                w_gate[d, :, h * GLA_DK:(h + 1) * GLA_DK].astype(F32))
            bg = bg.at[h, d, 0, :GLA_DK].set(b_gate[d, h * GLA_DK:(h + 1) * GLA_DK].astype(F32))
    lbp = jax.nn.softmax(hg_lb.astype(F32), axis=1)
    lb = (jnp.cumsum(lbp, axis=1) - lbp[:, :1])[:, e]
    return {"gla": (wg, bg, gla_norm.astype(F32).reshape(1, LANES)),
            "hgrn": (lb, hg_norm.astype(F32).reshape(1, LANES))}


def _even_call(z, mode, col, params, s0, seq_len, n_seq, row0, with_state_out):
    heads = 4
    blk0 = row0 // seq_len
    wide = HPS * LANES

    def tok(name):
        c0 = col[name] // HPS
        return pl.BlockSpec((seq_len, wide), lambda s, h: (blk0 + s, c0 + h), pipeline_mode=pl.Buffered(1))

    if mode == "gla":
        wg, bg, gn = params
        in_specs = [tok("gq"), tok("gk"), tok("gv"),
                    pl.BlockSpec((seq_len, LANES), lambda s, h: (blk0 + s, col["lr"])),
                    tok("gr"),
                    pl.BlockSpec((HPS, 2, LANES, LANES), lambda s, h: (h, 0, 0, 0)),
                    pl.BlockSpec((HPS, 2, 1, LANES), lambda s, h: (h, 0, 0, 0)),
                    pl.BlockSpec((1, LANES), lambda s, h: (0, 0))]
        args = [z, z, z, z, z, wg, bg, gn]
    else:
        lb, gn = params
        in_specs = [tok("hq"), tok("hf_f"), tok("hf_b"), tok("hi"), tok("hgate"),
                    pl.BlockSpec((2, wide), lambda s, h: (0, h)),
                    pl.BlockSpec((1, LANES), lambda s, h: (0, 0))]
        args = [z, z, z, z, z, lb, gn]
    st_spec = pl.BlockSpec((1, 2, HPS, LANES, LANES), lambda s, h: (s, 0, h, 0, 0))
    has_s0 = s0 is not None
    if has_s0:
        in_specs.append(st_spec)
        args.append(s0)
    out_specs = [pl.BlockSpec((seq_len, wide), lambda s, h: (s, h))]
    out_shape = [jax.ShapeDtypeStruct((n_seq * seq_len, heads * LANES), BF16)]
    if with_state_out:
        out_specs.append(st_spec)
        out_shape.append(jax.ShapeDtypeStruct((n_seq, 2, heads, LANES, LANES), F32))
    res = pl.pallas_call(
        functools.partial(_even_kernel, mode=mode, seq_len=seq_len,
                          with_state_out=with_state_out, has_s0=has_s0),
        grid=(n_seq, heads // HPS),
        in_specs=in_specs,
        out_specs=out_specs,
        out_shape=out_shape,
        scratch_shapes=[pltpu.VMEM((seq_len, wide), F32), pltpu.VMEM((seq_len, wide), F32)],
        compiler_params=_cparams(("parallel", "parallel")),
        name=f"mix_{mode}_{seq_len}",
    )(*args)
    return res


CONV_PAD = SUBLANES


def _conv_kernel(x_ref, w_ref, o_ref, pad_scr, *, seq_len):
    cb = pl.program_id(1)
    wide = DN_HEADS * LANES
    zeros = jnp.zeros((CONV_PAD, wide), F32)
    pad_scr[0:CONV_PAD, :] = zeros
    pad_scr[CONV_PAD + seq_len:2 * CONV_PAD + seq_len, :] = zeros
    pad_scr[CONV_PAD:CONV_PAD + seq_len, :] = x_ref[...]
    is_q = cb == 0
    is_qk = cb < 2
    scale = jnp.where(is_q, DN_DK ** -0.5, 1.0)
    for hh in range(DN_HEADS):
        cols = slice(hh * LANES, (hh + 1) * LANES)
        acc = jnp.zeros((seq_len, LANES), F32)
        for w in range(DN_CONV):
            acc = acc + pad_scr[pl.ds(CONV_PAD + w - DN_CONV // 2, seq_len), cols] * w_ref[w:w + 1, cols]
        y = acc * jax.nn.sigmoid(acc)
        yn = y * lax.rsqrt(jnp.sum(y * y, axis=-1, keepdims=True) + NORM_EPS)
        o_ref[:, cols] = jnp.where(is_qk, yn * scale, y)


def _conv_call(z, col0, w, seq_len, n_seq, row0):
    blk0 = row0 // seq_len
    wide = DN_HEADS * LANES
    c0 = col0 // DN_HEADS
    return pl.pallas_call(
        functools.partial(_conv_kernel, seq_len=seq_len),
        grid=(n_seq, 3),
        in_specs=[pl.BlockSpec((seq_len, wide), lambda s, c: (blk0 + s, c0 + c)),
                  pl.BlockSpec((DN_CONV, wide), lambda s, c: (0, c))],
        out_specs=pl.BlockSpec((seq_len, wide), lambda s, c: (s, c)),
        out_shape=jax.ShapeDtypeStruct((n_seq * seq_len, 3 * wide), F32),
        scratch_shapes=[pltpu.VMEM((seq_len + 2 * CONV_PAD, wide), F32)],
        compiler_params=_cparams(("parallel", "parallel")),
        name=f"dn_conv_{seq_len}",
    )(z, w)


def _softplus(x):
    return jnp.maximum(x, 0.0) + jnp.log1p(jnp.exp(-jnp.abs(x)))


def _dn_chunk(q, k, v, g, beta, s, reverse):
    c = CHUNK
    lower = not reverse
    tri = jnp.where(_tri(c, lower), 1.0, 0.0).astype(BF16)
    gam = _dot_sel(tri, g)
    yield
    ones_c = jnp.ones((c, c), BF16)
    gam_row = _dot_sel(ones_c, jnp.where(_tri(c, reverse), g[:, :c], 0.0))
    incl = _tri(c, lower)
    strict = _tri(c, lower, strict=True)
    diff = gam[:, :c] - gam_row
    dec = jnp.where(incl, jnp.exp(jnp.where(incl, diff, 0.0)), 0.0)
    kb = k * beta
    k16 = k.astype(BF16)
    m = jnp.where(strict, _dot_nt(kb.astype(BF16), k16) * dec, 0.0)
    yield
    eg = jnp.exp(gam)
    x = jnp.concatenate([v * beta, kb * eg], axis=1)
    p = -m
    n_fac = int(np.log2(c))
    for j in range(n_fac):
        x = x + _dot_hi(p, x)
        if j + 1 < n_fac:
            p = _dot_hi(p, p)
        yield
    u, w = x[:, :LANES], x[:, LANES:]
    s16 = s.astype(BF16)
    v_new = u - _dot(w.astype(BF16), s16)
    att = _dot_nt(q.astype(BF16), k16) * dec
    yield
    o = _dot((q * eg).astype(BF16), s16) + _dot(att.astype(BF16), v_new.astype(BF16))
    last = 0 if reverse else c - 1
    gl = gam[last:last + 1, :]
    s_new = s * jnp.exp(gl) + _dot_tn((k * jnp.exp(gl - gam)).astype(BF16), v_new.astype(BF16))
    return o, s_new


def _dn_kernel(*refs, seq_len, with_state_out, has_s0):
    it = iter(refs)
    q_ref, k_ref, v_ref, ab_ref, gate_ref, alog_ref, dtb_ref, gn_ref = (next(it) for _ in range(8))
    s0_ref = next(it) if has_s0 else None
    m_ref = next(it)
    so_ref = next(it) if with_state_out else None
    of_scr, ob_scr = next(it), next(it)
    h0 = pl.program_id(1) * HPS
    n_chunks = seq_len // CHUNK
    srow = lax.broadcasted_iota(jnp.int32, (LANES, LANES), 0)
    streams = [(hh, d) for hh in range(HPS) for d in range(2)]

    def chain(c, hh, d, s):
        rows = pl.ds(pl.multiple_of(c * CHUNK, CHUNK), CHUNK)
        cols = slice(hh * LANES, (hh + 1) * LANES)
        ab = ab_ref[rows, :]
        sel_a = jnp.where(srow == DN_HEADS * d + h0 + hh, 1.0, 0.0).astype(BF16)
        sel_b = jnp.where(srow == 2 * DN_HEADS + DN_HEADS * d + h0 + hh, 1.0, 0.0).astype(BF16)
        a = _dot_x_sel(ab, sel_a)
        beta = jax.nn.sigmoid(_dot_x_sel(ab, sel_b))
        g = -jnp.exp(alog_ref[hh, d]) * _softplus(a + dtb_ref[hh, d])
        yield
        o, s_new = yield from _dn_chunk(q_ref[rows, cols], k_ref[rows, cols], v_ref[rows, cols],
                                        g, beta, s, d == 1)
        return rows, cols, o, s_new

    def body(c, carry):
        res = _lockstep([chain(n_chunks - 1 - c if d else c, hh, d, carry[i])
                         for i, (hh, d) in enumerate(streams)])
        for (hh, d), (rows, cols, o, _) in zip(streams, res):
            (ob_scr if d else of_scr)[rows, cols] = o
        return tuple(r[3] for r in res)

    if has_s0:
        init = tuple(s0_ref[0, d, hh] for hh, d in streams)
    else:
        init = tuple(jnp.zeros((LANES, LANES), F32) for _ in streams)
    fin = lax.fori_loop(0, n_chunks, body, init)
    for hh in range(HPS):
        cols = slice(hh * LANES, (hh + 1) * LANES)
        m_ref[:, cols] = _head_norm_gate(of_scr[:, cols] + ob_scr[:, cols], gn_ref[...],
                                         gate_ref[:, cols]).astype(m_ref.dtype)
    if with_state_out:
        for i, (hh, d) in enumerate(streams):
            so_ref[0, d, hh] = fin[i]


S5_TC = 128
S5_CB = 4
S5_SW = S5_G * S5_P // S5_CB


def _s5_params(lam_re, lam_im, log_dt, b_re, b_im, c_re, c_im):
    gpb = S5_G // S5_CB
    eye = jnp.eye(gpb, dtype=F32)
    n = jnp.arange(1, S5_TC + 1, dtype=F32)[:, None, None]
    bbs, ccs, aps = [], [], []
    for d in range(2):
        lr = jnp.minimum(lam_re[d].astype(F32), -1e-4)
        li = lam_im[d].astype(F32)
        dt = jnp.exp(log_dt[d].astype(F32))[:, None]
        mag = jnp.exp(lr * dt)
        ar, ai = mag * jnp.cos(li * dt), mag * jnp.sin(li * dt)
        den = lr * lr + li * li
        nr = ar - 1.0
        cr = (nr * lr + ai * li) / den
        ci = (ai * lr - nr * li) / den
        bbr = cr[..., None] * b_re - ci[..., None] * b_im
        bbi = cr[..., None] * b_im + ci[..., None] * b_re
        blk = lambda t: jnp.einsum('bgpc,gh->bgchp', t.reshape(S5_CB, gpb, S5_P, S5_GS), eye).reshape(
            S5_CB, gpb * S5_GS, gpb * S5_P)
        bbs.append(jnp.stack([blk(bbr), blk(bbi)]))
        cblk = lambda t: jnp.einsum('bgcp,gh->bgphc', t.reshape(S5_CB, gpb, S5_GS, S5_P), eye).reshape(
            S5_CB, gpb * S5_P, gpb * S5_GS)
        ccs.append(jnp.stack([cblk(c_re[d].astype(F32)), cblk(c_im[d].astype(F32))]))
        pm = jnp.exp(n * (lr * dt)[None])
        pr = (pm * jnp.cos(n * (li * dt)[None])).reshape(S5_TC, -1)
        pi = (pm * jnp.sin(n * (li * dt)[None])).reshape(S5_TC, -1)
        if d == 1:
            pr, pi = pr[::-1], pi[::-1]
        aps.append(jnp.stack([pr, pi]))
    bb = jnp.stack(bbs, axis=2).astype(BF16)
    cc = jnp.stack(ccs, axis=2).astype(BF16)
    apow = jnp.stack(aps, axis=1)
    return bb, cc, apow


def _s5_scan(xr, xi, pr, pi, cr, ci, reverse, loc, loc0):
    tc, w = xr.shape
    g8 = SUBLANES
    ng = tc // g8

    def power(n):
        idx = tc - n if reverse else n - 1
        return pr[idx:idx + 1, :], pi[idx:idx + 1, :]

    def step(xr, xi, s, unit):
        n = xr.shape[0]
        row = lax.broadcasted_iota(jnp.int32, xr.shape, 0)
        period = g8 if unit == 1 else n
        pos = jnp.bitwise_and(row, period - 1) if unit == 1 else row
        keep = (pos < period - s) if reverse else (pos >= s)
        shift = n - s if reverse else s
        a_r, a_i = power(s * unit)
        if unit == 1:
            rot = lambda x: pltpu.roll(x.reshape(n // g8, g8, w), g8 - s if reverse else s, 1).reshape(n, w)
        else:
            rot = lambda x: pltpu.roll(x, shift, 0)
        sr = jnp.where(keep, rot(xr), 0.0)
        si = jnp.where(keep, rot(xi), 0.0)
        return xr + (a_r * sr - a_i * si), xi + (a_r * si + a_i * sr)

    s = 1
    while s < g8:
        xr, xi = step(xr, xi, s, 1)
        s *= 2
    end = 0 if reverse else g8 - 1
    ers, eis = [], []
    for k in range(w // LANES):
        r0, i0 = (loc0 + k) * tc, (loc0 + w // LANES + k) * tc
        loc[r0:r0 + tc, :] = xr[:, k * LANES:(k + 1) * LANES]
        loc[i0:i0 + tc, :] = xi[:, k * LANES:(k + 1) * LANES]
        ers.append(loc[pl.ds(r0 + end, ng, stride=g8), :])
        eis.append(loc[pl.ds(i0 + end, ng, stride=g8), :])
    er = jnp.concatenate(ers, axis=1)
    ei = jnp.concatenate(eis, axis=1)
    s = 1
    while s < ng:
        er, ei = step(er, ei, s, g8)
        s *= 2
    grow = lax.broadcasted_iota(jnp.int32, (ng, w), 0)
    if reverse:
        pin_r = jnp.concatenate([pr[g8 * (g + 1):g8 * (g + 1) + 1] for g in range(ng - 1)] + [jnp.ones((1, w), F32)], 0)
        pin_i = jnp.concatenate([pi[g8 * (g + 1):g8 * (g + 1) + 1] for g in range(ng - 1)] + [jnp.zeros((1, w), F32)], 0)
        prev_r = jnp.where(grow < ng - 1, pltpu.roll(er, ng - 1, 0), 0.0)
        prev_i = jnp.where(grow < ng - 1, pltpu.roll(ei, ng - 1, 0), 0.0)
    else:
        pin_r = jnp.concatenate([jnp.ones((1, w), F32)] + [pr[g8 * g - 1:g8 * g] for g in range(1, ng)], 0)
        pin_i = jnp.concatenate([jnp.zeros((1, w), F32)] + [pi[g8 * g - 1:g8 * g] for g in range(1, ng)], 0)
        prev_r = jnp.where(grow >= 1, pltpu.roll(er, 1, 0), 0.0)
        prev_i = jnp.where(grow >= 1, pltpu.roll(ei, 1, 0), 0.0)
    in_r = prev_r + (pin_r * cr - pin_i * ci)
    in_i = prev_i + (pin_r * ci + pin_i * cr)
    in_r = jnp.concatenate([jnp.broadcast_to(in_r[g:g + 1], (g8, w)) for g in range(ng)], axis=0)
    in_i = jnp.concatenate([jnp.broadcast_to(in_i[g:g + 1], (g8, w)) for g in range(ng)], axis=0)
    p8r = pr[tc - g8:] if reverse else pr[:g8]
    p8i = pi[tc - g8:] if reverse else pi[:g8]
    a_r = jnp.concatenate([p8r] * ng, axis=0)
    a_i = jnp.concatenate([p8i] * ng, axis=0)
    return xr + (a_r * in_r - a_i * in_i), xi + (a_r * in_i + a_i * in_r)


def _s5_kernel(*refs, seq_len, has_h0):
    it = iter(refs)
    u_ref, bb_ref, cc_ref, ap_ref, dsk_ref = (next(it) for _ in range(5))
    h0r_ref, h0i_ref = (next(it), next(it)) if has_h0 else (None, None)
    y_ref, fr_ref, fi_ref = next(it), next(it), next(it)
    yf_scr, yb_scr, loc_scr = next(it), next(it), next(it)
    n_chunks = seq_len // S5_TC

    def run(c, d, carry):
        cr, ci = carry
        rows = pl.ds(pl.multiple_of(c * S5_TC, S5_TC), S5_TC)
        ub = u_ref[rows, :].astype(BF16)
        pr, pi = ap_ref[0, d], ap_ref[1, d]
        xr, xi = _dot(ub, bb_ref[0, 0, d]), _dot(ub, bb_ref[1, 0, d])
        hr, hi = _s5_scan(xr, xi, pr, pi, cr, ci, d == 1, loc_scr, d * 2 * (S5_SW // LANES))
        (yf_scr, yb_scr)[d][rows, :] = (_dot(hr.astype(BF16), cc_ref[0, 0, d])
                                         - _dot(hi.astype(BF16), cc_ref[1, 0, d]))
        last = 0 if d == 1 else S5_TC - 1
        return hr[last:last + 1, :], hi[last:last + 1, :]

    def body(c, carry):
        cf, cb = carry
        return run(c, 0, cf), run(n_chunks - 1 - c, 1, cb)

    if has_h0:
        init = tuple((h0r_ref[0, d:d + 1, :], h0i_ref[0, d:d + 1, :]) for d in range(2))
    else:
        z = jnp.zeros((1, S5_SW), F32)
        init = ((z, z), (z, z))
    fin = lax.fori_loop(0, n_chunks, body, init)
    y_ref[...] = dsk_ref[...] * u_ref[...] + (yf_scr[...] + yb_scr[...])
    for d in range(2):
        fr_ref[0, d:d + 1, :] = fin[d][0]
        fi_ref[0, d:d + 1, :] = fin[d][1]


def _s5_call(z, col0, bb, cc, apow, dsk, h0r, h0i, seq_len, n_seq, row0):
    blk0 = row0 // seq_len
    has_h0 = h0r is not None
    st_spec = pl.BlockSpec((1, 2, S5_SW), lambda s, c: (s, 0, c))
    in_specs = [pl.BlockSpec((seq_len, LANES), lambda s, c: (blk0 + s, col0 + c)),
                pl.BlockSpec((2, 1, 2, LANES, S5_SW), lambda s, c: (0, c, 0, 0, 0)),
                pl.BlockSpec((2, 1, 2, S5_SW, LANES), lambda s, c: (0, c, 0, 0, 0)),
                pl.BlockSpec((2, 2, S5_TC, S5_SW), lambda s, c: (0, 0, 0, c)),
                pl.BlockSpec((1, LANES), lambda s, c: (0, c))]
    args = [z, bb, cc, apow, dsk]
    if has_h0:
        in_specs += [st_spec, st_spec]
        args += [h0r, h0i]
    st_shape = jax.ShapeDtypeStruct((n_seq, 2, S5_G * S5_P), F32)
    return pl.pallas_call(
        functools.partial(_s5_kernel, seq_len=seq_len, has_h0=has_h0),
        grid=(n_seq, S5_CB),
        in_specs=in_specs,
        out_specs=[pl.BlockSpec((seq_len, LANES), lambda s, c: (s, c)), st_spec, st_spec],
        out_shape=[jax.ShapeDtypeStruct((n_seq * seq_len, MIX_W), F32), st_shape, st_shape],
        scratch_shapes=[pltpu.VMEM((seq_len, LANES), F32), pltpu.VMEM((seq_len, LANES), F32),
                        pltpu.VMEM((4 * (S5_SW // LANES) * S5_TC, LANES), F32)],
        compiler_params=_cparams(("parallel", "parallel")),
        name=f"mix_s5_{seq_len}",
    )(*args)


def _glu_kernel(y_ref, w_ref, b_ref, o_ref):
    zz = jax.nn.gelu(y_ref[...])
    o_ref[...] = (zz * jax.nn.sigmoid(_dot(zz.astype(BF16), w_ref[...]) + b_ref[...])).astype(o_ref.dtype)


def _glu_call(y, w, b):
    t = y.shape[0]
    return pl.pallas_call(
        _glu_kernel,
        grid=(t // TM,),
        in_specs=[pl.BlockSpec((TM, MIX_W), lambda i: (i, 0)),
                  pl.BlockSpec((MIX_W, MIX_W), lambda i: (0, 0)),
                  pl.BlockSpec((1, MIX_W), lambda i: (0, 0))],
        out_specs=pl.BlockSpec((TM, MIX_W), lambda i: (i, 0)),
        out_shape=jax.ShapeDtypeStruct((t, MIX_W), BF16),
        compiler_params=_cparams(("parallel",)),
        name="s5_glu",
    )(y, w, b)


PK_HP = 2 * PK_HEADS
PK_SIDE = PK_KEYS
PK_PICKS = PK_HEADS * PK_TOPK
TK_TM = 128
TK_HG = 2
WB_TM = 128
WB_GRP = 16
WB_STRIDE = PK_SIDE + SUBLANES
PM_TM = 1024
PM_SLABS = 8


def _score_kernel(h_ref, wq_ref, keys_ref, s_ref):
    q = _dot(h_ref[...].astype(BF16), wq_ref[...])
    for hp in range(PK_HP):
        s_ref[hp] = _dot_nt(keys_ref[hp], q[:, hp * LANES:(hp + 1) * LANES].astype(BF16))


def _score_call(h, wq, keys):
    return pl.pallas_call(
        _score_kernel,
        grid=(h.shape[0] // TM,),
        in_specs=[pl.BlockSpec((TM, D_MODEL), lambda i: (i, 0)),
                  pl.BlockSpec((D_MODEL, PK_HP * LANES), lambda i: (0, 0)),
                  pl.BlockSpec((PK_HP, PK_KEYS, LANES), lambda i: (0, 0, 0))],
        out_specs=pl.BlockSpec((PK_HP, PK_KEYS, TM), lambda i: (0, 0, i)),
        out_shape=jax.ShapeDtypeStruct((PK_HP, PK_KEYS, h.shape[0]), F32),
        compiler_params=_cparams(("parallel",)),
        name="peer_scores",
    )(h, wq, keys)


def _top16_rows(tiles):
    shape = tiles[0].shape
    row = lax.broadcasted_iota(jnp.int32, shape, 0).astype(F32)
    out_row = lax.broadcasted_iota(jnp.int32, (PK_TOPK, shape[1]), 0)

    def take(r, s, vals, idxs):
        m = jnp.max(s, axis=0, keepdims=True)
        am = jnp.min(jnp.where(s == m, row, float(shape[0])), axis=0, keepdims=True)
        at = out_row == r
        return jnp.where(row == am, -jnp.inf, s), jnp.where(at, m, vals), jnp.where(at, am, idxs)

    def body(r, carry):
        out = ()
        for i in range(len(tiles)):
            out += take(r, *carry[3 * i:3 * i + 3])
        return out

    zero = jnp.zeros((PK_TOPK, shape[1]), F32)
    init = ()
    for s in tiles:
        init += (s, zero, zero)
    res = lax.fori_loop(0, PK_TOPK, body, init)
    return [(res[3 * i + 1], res[3 * i + 2]) for i in range(len(tiles))]


def _pair_candidates(v1, v2):
    half = PK_TOPK // 2
    n = v1.shape[1]
    rows = [jnp.broadcast_to(v1[0:1], (PK_TOPK, n)) + v2]
    rows += [jnp.broadcast_to(v1[a:a + 1], (half, n)) + v2[:half] for a in range(1, half)]
    rows.append(v1[half:] + jnp.broadcast_to(v2[0:1], (half, n)))
    return jnp.concatenate(rows, axis=0)


def _topk_kernel(s_ref, i1_ref, i2_ref, g_ref, p1_scr, p2_scr, g_scr):
    n = TK_TM
    zero16 = jnp.zeros((PK_TOPK, n), F32)
    rank = lax.broadcasted_iota(jnp.int32, (PK_TOPK, n), 0)
    rank_f = rank.astype(F32)
    flat = _pair_candidates(rank_f * PK_TOPK, rank_f)
    out_row = rank

    def heads(hg, carry):
        tops = []
        for j in range(TK_HG):
            tops += _top16_rows([s_ref[2 * (TK_HG * hg + j) + i] for i in range(2)])
        c1s, c2s, init = [], [], ()
        for j in range(TK_HG):
            (v1, a1), (v2, a2) = tops[2 * j], tops[2 * j + 1]
            c1s.append(_pair_candidates(a1, zero16))
            c2s.append(_pair_candidates(zero16, a2))
            init += (_pair_candidates(v1, v2), zero16, zero16, zero16)

        def pick(r, c1, c2, cand, best, p1, p2):
            m = jnp.max(cand, axis=0, keepdims=True)
            pos = jnp.min(jnp.where(cand == m, flat, float(PK_TOPK * PK_TOPK)), axis=0, keepdims=True)
            hit = flat == pos
            e1 = jnp.max(jnp.where(hit, c1, -1.0), axis=0, keepdims=True)
            e2 = jnp.max(jnp.where(hit, c2, -1.0), axis=0, keepdims=True)
            at = out_row == r
            return (jnp.where(hit, -jnp.inf, cand), jnp.where(at, m, best),
                    jnp.where(at, e1, p1), jnp.where(at, e2, p2))

        def body(r, carry):
            out = ()
            for j in range(TK_HG):
                out += pick(r, c1s[j], c2s[j], *carry[4 * j:4 * j + 4])
            return out

        res = lax.fori_loop(0, PK_TOPK, body, init)
        for j in range(TK_HG):
            _, best, p1, p2 = res[4 * j:4 * j + 4]
            ex = jnp.exp(best - jnp.max(best, axis=0, keepdims=True))
            rows = pl.ds(pl.multiple_of((hg * TK_HG + j) * PK_TOPK, PK_TOPK), PK_TOPK)
            p1_scr[rows, :] = p1
            p2_scr[rows, :] = p2
            g_scr[rows, :] = ex / jnp.sum(ex, axis=0, keepdims=True)
        return carry

    lax.fori_loop(0, PK_HEADS // TK_HG, heads, 0)
    i1_ref[...] = p1_scr[...].T
    i2_ref[...] = p2_scr[...].T
    g_ref[...] = g_scr[...].T


def _topk_call(s):
    t = s.shape[2]
    spec = pl.BlockSpec((TK_TM, PK_PICKS), lambda i: (i, 0))
    shp = jax.ShapeDtypeStruct((t, PK_PICKS), F32)
    return pl.pallas_call(
        _topk_kernel,
        grid=(t // TK_TM,),
        in_specs=[pl.BlockSpec((PK_HP, PK_KEYS, TK_TM), lambda i: (0, 0, i))],
        out_specs=[spec, spec, spec],
        out_shape=[shp, shp, shp],
        scratch_shapes=[pltpu.VMEM((PK_PICKS, TK_TM), F32)] * 3,
        compiler_params=_cparams(("parallel",)),
        name="peer_topk",
    )(s)


def _wbuild_kernel(i1_ref, i2_ref, g_ref, w_ref, stage_a, stage_b):
    sub = lax.broadcasted_iota(jnp.int32, (PK_SIDE, PK_PICKS), 0).astype(F32).astype(BF16)
    one = jnp.ones((PK_SIDE, PK_PICKS), BF16)
    zero = jnp.zeros((PK_SIDE, PK_PICKS), BF16)

    def grids(t0, stage):
        for tt in range(WB_GRP):
            r1 = i1_ref[pl.ds(t0 + tt, 1), :].astype(BF16)
            r2 = i2_ref[pl.ds(t0 + tt, 1), :].astype(BF16)
            rg = g_ref[pl.ds(t0 + tt, 1), :].astype(BF16)
            p1t = jnp.where(sub == jnp.broadcast_to(r1, sub.shape), one, zero)
            q2t = jnp.where(sub == jnp.broadcast_to(r2, sub.shape), jnp.broadcast_to(rg, sub.shape), zero)
            stage[tt * WB_STRIDE:tt * WB_STRIDE + PK_SIDE, :] = _dot_nt(p1t, q2t)

    def regroup(t0, stage):
        for i1 in range(PK_SIDE):
            lo = stage[pl.ds(i1, SUBLANES, stride=WB_STRIDE), :]
            hi = stage[pl.ds(i1 + SUBLANES * WB_STRIDE, SUBLANES, stride=WB_STRIDE), :]
            w_ref[0, i1, pl.ds(t0, WB_GRP), :] = jnp.concatenate([lo, hi], axis=0).astype(BF16)

    def pair(gi, carry):
        t_a = pl.multiple_of(gi * 2 * WB_GRP, WB_GRP)
        t_b = pl.multiple_of(gi * 2 * WB_GRP + WB_GRP, WB_GRP)
        grids(t_a, stage_a)
        grids(t_b, stage_b)
        regroup(t_a, stage_a)
        regroup(t_b, stage_b)
        return carry

    lax.fori_loop(0, WB_TM // (2 * WB_GRP), pair, 0)


def _wbuild_call(i1, i2, g):
    spec = pl.BlockSpec((WB_TM, PK_PICKS), lambda i: (i, 0))
    return pl.pallas_call(
        _wbuild_kernel,
        grid=(i1.shape[0] // WB_TM,),
        in_specs=[spec, spec, spec],
        out_specs=pl.BlockSpec((1, PK_SIDE, WB_TM, PK_SIDE), lambda i: (i, 0, 0, 0)),
        out_shape=jax.ShapeDtypeStruct((i1.shape[0] // WB_TM, PK_SIDE, WB_TM, PK_SIDE), BF16),
        scratch_shapes=[pltpu.VMEM((WB_GRP * WB_STRIDE, PK_SIDE), F32)] * 2,
        compiler_params=_cparams(("parallel",)),
        name="peer_route",
    )(i1, i2, g)


def _peer_kernel(h_ref, u_ref, v_ref, w_ref, x_ref, g2_ref, o_ref, acc):
    j = pl.program_id(1)

    @pl.when(j == 0)
    def _():
        acc[...] = jnp.zeros_like(acc)

    a = _dot_nt(h_ref[...], u_ref[...])
    nsub = PM_TM // WB_TM
    w = jnp.concatenate(
        [jnp.concatenate([w_ref[n, sl] for sl in range(PM_SLABS)], axis=1) for n in range(nsub)], axis=0)
    acc[...] += _dot((jax.nn.gelu(a) * w.astype(F32)).astype(BF16), v_ref[...])

    @pl.when(j == pl.num_programs(1) - 1)
    def _():
        o_ref[...] = x_ref[...] + g2_ref[0] * acc[...]


def _peer_call(h16, u16, v16, w, x, mod3):
    ne = PM_SLABS * PK_SIDE
    tok = pl.BlockSpec((PM_TM, D_MODEL), lambda i, j: (i, 0))
    return pl.pallas_call(
        _peer_kernel,
        grid=(x.shape[0] // PM_TM, PK_SIDE // PM_SLABS),
        in_specs=[tok,
                  pl.BlockSpec((ne, D_MODEL), lambda i, j: (j, 0)),
                  pl.BlockSpec((ne, D_MODEL), lambda i, j: (j, 0)),
                  pl.BlockSpec((PM_TM // WB_TM, PM_SLABS, WB_TM, PK_SIDE), lambda i, j: (i, j, 0, 0)),
                  tok,
                  pl.BlockSpec((1, 1, D_MODEL), lambda i, j: (_cond_row(i, PM_TM) * 6 + 5, 0, 0))],
        out_specs=tok,
        out_shape=jax.ShapeDtypeStruct(x.shape, F32),
        scratch_shapes=[pltpu.VMEM((PM_TM, D_MODEL), F32)],
        compiler_params=_cparams(("parallel", "arbitrary")),
        name="peer_experts",
    )(h16, u16, v16, w, x, mod3)


def _final_norm_kernel(x_ref, g_ref, o_ref):
    x = x_ref[...]
    o_ref[...] = x * lax.rsqrt(jnp.mean(x * x, axis=-1, keepdims=True) + NORM_EPS) * g_ref[...]


def _final_norm_call(x, g):
    spec = pl.BlockSpec((TM, D_MODEL), lambda i: (i, 0))
    return pl.pallas_call(
        _final_norm_kernel,
        grid=(x.shape[0] // TM,),
        in_specs=[spec, pl.BlockSpec((1, D_MODEL), lambda i: (0, 0))],
        out_specs=spec,
        out_shape=jax.ShapeDtypeStruct(x.shape, F32),
        compiler_params=_cparams(("parallel",)),
        name="final_norm",
    )(x, g)


def _dn_params(conv_w, a_log, dt_bias, dn_norm):
    bc = lambda p: jnp.broadcast_to(p.astype(F32).T[:, :, None, None], (DN_HEADS, 2, 1, LANES))
    return {"conv": conv_w.astype(F32), "alog": bc(a_log), "dtb": bc(dt_bias),
            "gn": dn_norm.astype(F32).reshape(1, LANES)}


def _dn_call(z, qkv, col, alog, dtb, gn, s0, seq_len, n_seq, row0, with_state_out):
    blk0 = row0 // seq_len
    hd = DN_HEADS
    nhb = hd // HPS
    wide = HPS * LANES
    gate0 = col["gate"] // HPS
    one = pl.Buffered(1)
    in_specs = [pl.BlockSpec((seq_len, wide), lambda s, h: (s, h), pipeline_mode=one),
                pl.BlockSpec((seq_len, wide), lambda s, h: (s, nhb + h), pipeline_mode=one),
                pl.BlockSpec((seq_len, wide), lambda s, h: (s, 2 * nhb + h), pipeline_mode=one),
                pl.BlockSpec((seq_len, LANES), lambda s, h: (blk0 + s, col["ab"])),
                pl.BlockSpec((seq_len, wide), lambda s, h: (blk0 + s, gate0 + h), pipeline_mode=one),
                pl.BlockSpec((HPS, 2, 1, LANES), lambda s, h: (h, 0, 0, 0)),
                pl.BlockSpec((HPS, 2, 1, LANES), lambda s, h: (h, 0, 0, 0)),
                pl.BlockSpec((1, LANES), lambda s, h: (0, 0))]
    args = [qkv, qkv, qkv, z, z, alog, dtb, gn]
    st_spec = pl.BlockSpec((1, 2, HPS, LANES, LANES), lambda s, h: (s, 0, h, 0, 0))
    has_s0 = s0 is not None
    if has_s0:
        in_specs.append(st_spec)
        args.append(s0)
    out_specs = [pl.BlockSpec((seq_len, wide), lambda s, h: (s, h))]
    out_shape = [jax.ShapeDtypeStruct((n_seq * seq_len, hd * LANES), BF16)]
    if with_state_out:
        out_specs.append(st_spec)
        out_shape.append(jax.ShapeDtypeStruct((n_seq, 2, hd, LANES, LANES), F32))
    return pl.pallas_call(
        functools.partial(_dn_kernel, seq_len=seq_len, with_state_out=with_state_out, has_s0=has_s0),
        grid=(n_seq, nhb),
        in_specs=in_specs,
        out_specs=out_specs,
        out_shape=out_shape,
        scratch_shapes=[pltpu.VMEM((seq_len, wide), F32), pltpu.VMEM((seq_len, wide), F32)],
        compiler_params=_cparams(("parallel", "parallel")),
        name=f"mix_dn_{seq_len}",
    )(*args)


def _grid_col_major(t):
    n, d = t.shape
    rows = DEC_SEQ // GRID_W
    return t.reshape(DEC_BATCH, rows, GRID_W, d).transpose(0, 2, 1, 3).reshape(n, d)


def _grid_row_major(t):
    n, d = t.shape
    rows = DEC_SEQ // GRID_W
    return t.reshape(DEC_BATCH, GRID_W, rows, d).transpose(0, 2, 1, 3).reshape(n, d)


def _both_groups(fn, s0_sample):
    res_p = fn(None, SEQ, BATCH, 0, True)
    res_s = fn(s0_sample, DEC_SEQ, DEC_BATCH, T_PROMPT, False)
    return jnp.concatenate([res_p[0], res_s[0]], axis=0), res_p[1]


def kernel(x_prompt, x_sample, state_gla, state_hgrn, state_s5_re, state_s5_im, state_dn, c, c_ctx, w_ada, b_ada, norm1, norm2, w_in_even, w_gla_gate, b_gla_gate, gla_norm, hg_lb, hg_norm, w_in_odd, s5_lam_re, s5_lam_im, s5_log_dt, s5_b_re, s5_b_im, s5_c_re, s5_c_im, s5_d, s5_w_glu, s5_b_glu, dn_conv, dn_a_log, dn_dt_bias, dn_norm, w_out, pk_w_q, pk_keys, pk_u, pk_v, final_norm):
    x = jnp.concatenate([x_prompt.reshape(T_PROMPT, D_MODEL), x_sample.reshape(T_SAMPLE, D_MODEL)], axis=0)
    cond = jnp.zeros((N_COND, D_MODEL), F32).at[0].set(c_ctx.astype(F32)).at[1:1 + DEC_BATCH].set(c.astype(F32))
    mod = _ada_call(cond, w_ada.astype(F32), b_ada.astype(F32))

    out_g, out_h, out_re, out_im, out_dn = [], [], [], [], []
    for l in range(DEPTH):
        mod3 = mod[l].reshape(N_COND * 6, 1, D_MODEL)
        g1 = norm1[l].astype(F32).reshape(1, D_MODEL)
        if l % 2 == 0:
            e = l // 2
            w_in = _relayout_cols(w_in_even[e], EVEN_IDX).astype(BF16)
            z = _in_call(x, mod3, g1, w_in, 2)
            prm = _even_params(w_gla_gate[e], b_gla_gate[e], gla_norm[e], hg_lb, hg_norm[e], e)
            s0g = jnp.pad(state_gla[:, e].astype(F32), ((0, 0),) * 3 + ((0, HG_DK - GLA_DK), (0, 0)))
            m_g, st_g = _both_groups(
                lambda s0, L, n, r0, so: _even_call(z, "gla", EVEN_COL, prm["gla"], s0, L, n, r0, so), s0g)
            m_h, st_h = _both_groups(
                lambda s0, L, n, r0, so: _even_call(z, "hgrn", EVEN_COL, prm["hgrn"], s0, L, n, r0, so),
                state_hgrn[:, e].astype(F32))
            m = jnp.concatenate([m_g, m_h], axis=1)
            out_g.append(st_g[..., :GLA_DK, :])
            out_h.append(st_h)
        else:
            j = l // 2
            x_in = jnp.concatenate([x[:T_PROMPT], _grid_col_major(x[T_PROMPT:])], axis=0)
            w_in = _relayout_cols(w_in_odd[j], ODD_IDX).astype(BF16)
            z = _in_call(x_in, mod3, g1, w_in, 3)
            bb, cc, apow = _s5_params(s5_lam_re[j], s5_lam_im[j], s5_log_dt[j], s5_b_re[j], s5_b_im[j],
                                      s5_c_re[j], s5_c_im[j])
            dsk = s5_d[j].astype(F32).reshape(1, MIX_W)
            y_p, fr, fi = _s5_call(z, ODD_COL["u"], bb, cc, apow, dsk, None, None, SEQ, BATCH, 0)
            y_s, _, _ = _s5_call(z, ODD_COL["u"], bb, cc, apow, dsk,
                                 state_s5_re[:, j].astype(F32).reshape(DEC_BATCH, 2, -1),
                                 state_s5_im[:, j].astype(F32).reshape(DEC_BATCH, 2, -1),
                                 DEC_SEQ, DEC_BATCH, T_PROMPT)
            m_s5 = _glu_call(jnp.concatenate([y_p, y_s], axis=0), s5_w_glu[j].astype(BF16),
                             s5_b_glu[j].astype(F32).reshape(1, MIX_W))
            dp = _dn_params(dn_conv[j], dn_a_log[j], dn_dt_bias[j], dn_norm[j])

            def dn(s0, L, n, r0, so):
                qkv = _conv_call(z, ODD_COL["qkv"], dp["conv"], L, n, r0)
                return _dn_call(z, qkv, ODD_COL, dp["alog"], dp["dtb"], dp["gn"], s0, L, n, r0, so)

            m_dn, st_dn = _both_groups(dn, state_dn[:, j].astype(F32))
            m = jnp.concatenate([m_s5, m_dn], axis=1)
            m = jnp.concatenate([m[:T_PROMPT], _grid_row_major(m[T_PROMPT:])], axis=0)
            out_re.append(fr.reshape(BATCH, 2, S5_G, S5_P))
            out_im.append(fi.reshape(BATCH, 2, S5_G, S5_P))
            out_dn.append(st_dn)
        x, h2 = _out_call(x, m, w_out[l].astype(BF16), mod3, norm2[l].astype(F32).reshape(1, D_MODEL))
        s = _score_call(h2, pk_w_q[l].astype(BF16), pk_keys[l].reshape(PK_HP, PK_KEYS, PK_DQ // 2).astype(BF16))
        i1, i2, gate = _topk_call(s)
        w = _wbuild_call(i1, i2, gate)
        x = _peer_call(h2.astype(BF16), pk_u[l].astype(BF16), pk_v[l].astype(BF16), w, x, mod3)
    y = _final_norm_call(x, final_norm.astype(F32).reshape(1, D_MODEL))
    return (y[:T_PROMPT].reshape(BATCH, SEQ, D_MODEL), y[T_PROMPT:].reshape(DEC_BATCH, DEC_SEQ, D_MODEL),
            jnp.stack(out_g, axis=1), jnp.stack(out_h, axis=1), jnp.stack(out_re, axis=1),
            jnp.stack(out_im, axis=1), jnp.stack(out_dn, axis=1))
```

```python
import functools

import numpy as np
import jax
import jax.numpy as jnp
from jax import lax
from jax.experimental import pallas as pl
from jax.experimental.pallas import tpu as pltpu

F32 = jnp.float32
BF16 = jnp.bfloat16

D_MODEL = 1024
BATCH = 32
SEQ = 256
DEPTH = 4
DEC_BATCH = 8
DEC_SEQ = 2048
GRID_W = 64
N_EVEN = 2
N_ODD = 2
MIX_W = 512
GLA_HEADS = 4
GLA_DK = 64
GLA_DV = 128
GLA_RANK = 16
GLA_TAU = 16.0
HG_HEADS = 4
HG_DK = 128
HG_DV = 128
S5_GS = 16
S5_G = 32
S5_P = 64
DN_HEADS = 4
DN_DK = 128
DN_DV = 128
DN_CONV = 5
PK_HEADS = 8
PK_KEYS = 128
PK_DQ = 256
PK_TOPK = 16
NORM_EPS = 1e-6
GATE_FLOOR = 1e-30

T_PROMPT = BATCH * SEQ
T_SAMPLE = DEC_BATCH * DEC_SEQ
T_ALL = T_PROMPT + T_SAMPLE
N_COND = 16

LANES = 128
SUBLANES = 8
VMEM_LIMIT = 56 * 1024 * 1024

TM = 512


def _layout(pieces):
    idx, col = [], {}
    for name, src, width in pieces:
        assert len(idx) % LANES == 0 and width % LANES == 0
        col[name] = len(idx) // LANES
        src = list(src)
        idx += src + [-1] * (width - len(src))
    return np.asarray(idx, np.int32), col


def _even_layout():
    o = np.cumsum((0, 256, 256, 512, 16, 16, 512, 512, 512, 512, 512, 512))
    rng = lambda i: range(o[i], o[i + 1])
    pieces = []
    for nm, base in (("gq", o[0]), ("gk", o[1])):
        for h in range(GLA_HEADS):
            pieces.append((nm if h == 0 else f"{nm}{h}", range(base + 64 * h, base + 64 * h + 64), LANES))
    pieces += [("gv", rng(2), 512), ("gr", rng(5), 512),
               ("hq", rng(6), 512), ("hf_f", rng(7), 512), ("hf_b", rng(8), 512),
               ("hi", rng(9), 512), ("hgate", rng(10), 512),
               ("lr", list(rng(3)) + list(rng(4)), LANES), ("pad", [], LANES)]
    return _layout(pieces)


def _odd_layout():
    o = np.cumsum((0, 512, 1536, 4, 4, 4, 4, 512))
    rng = lambda i: range(o[i], o[i + 1])
    return _layout([("u", rng(0), 512), ("qkv", rng(1), 1536), ("gate", rng(6), 512),
                    ("ab", range(o[2], o[6]), LANES)])


EVEN_IDX, EVEN_COL = _even_layout()
ODD_IDX, ODD_COL = _odd_layout()


def _relayout_cols(w, idx):
    wz = jnp.concatenate([w, jnp.zeros(w.shape[:-1] + (1,), w.dtype)], axis=-1)
    return jnp.take(wz, jnp.where(idx < 0, w.shape[-1], idx), axis=-1)


def _cparams(sem):
    return pltpu.CompilerParams(dimension_semantics=sem, vmem_limit_bytes=VMEM_LIMIT)


def _cond_row(i, tm=TM):
    n_p = T_PROMPT // tm
    per_seq = DEC_SEQ // tm
    return jnp.where(i < n_p, 0, 1 + (i - n_p) // per_seq)


def _split3(x):
    hi = x.astype(BF16)
    r1 = x - hi.astype(F32)
    mid = r1.astype(BF16)
    lo = (r1 - mid.astype(F32)).astype(BF16)
    return hi, mid, lo


def _dot(a, b):
    return jnp.dot(a, b, preferred_element_type=F32)


def _dot_nt(a, b):
    return lax.dot_general(a, b, (((1,), (1,)), ((), ())), preferred_element_type=F32)


def _dot_tn(a, b):
    return lax.dot_general(a, b, (((0,), (0,)), ((), ())), preferred_element_type=F32)


def _dot_sel(sel_bf16, x):
    hi, mid, lo = _split3(x)
    return _dot(sel_bf16, hi) + _dot(sel_bf16, mid) + _dot(sel_bf16, lo)


def _dot_x_sel(x, sel_bf16):
    hi, mid, lo = _split3(x)
    return _dot(hi, sel_bf16) + _dot(mid, sel_bf16) + _dot(lo, sel_bf16)


def _dot_hi(a, b):
    ah = a.astype(BF16)
    al = (a - ah.astype(F32)).astype(BF16)
    bh = b.astype(BF16)
    bl = (b - bh.astype(F32)).astype(BF16)
    return _dot(ah, bh) + (_dot(ah, bl) + _dot(al, bh))


def _ada_kernel(c_ref, w_ref, b_ref, o_ref):
    c = c_ref[...]
    cs = c * jax.nn.sigmoid(c)
    o_ref[0] = _dot(cs, w_ref[0]) + b_ref[0]


def _ada_call(cond, w_ada, b_ada):
    nt = 6
    return pl.pallas_call(
        _ada_kernel,
        grid=(DEPTH, nt),
        in_specs=[
            pl.BlockSpec((N_COND, D_MODEL), lambda l, j: (0, 0)),
            pl.BlockSpec((1, D_MODEL, D_MODEL), lambda l, j: (l, 0, j)),
            pl.BlockSpec((1, 1, D_MODEL), lambda l, j: (l, 0, j)),
        ],
        out_specs=pl.BlockSpec((1, N_COND, D_MODEL), lambda l, j: (l, 0, j)),
        out_shape=jax.ShapeDtypeStruct((DEPTH, N_COND, 6 * D_MODEL), F32),
        compiler_params=_cparams(("parallel", "parallel")),
        name="ada_mod",
    )(cond, w_ada, b_ada.reshape(DEPTH, 1, 6 * D_MODEL))


def _modulated_norm(x, g, sc, sh):
    ms = jnp.mean(x * x, axis=-1, keepdims=True)
    y = x * lax.rsqrt(ms + NORM_EPS) * g
    return y * (1.0 + sc) + sh


def _in_kernel(x_ref, sh_ref, sc_ref, g_ref, w_ref, o_ref):
    h = _modulated_norm(x_ref[...], g_ref[...], sc_ref[0], sh_ref[0])
    o_ref[...] = _dot(h.astype(BF16), w_ref[...])


def _in_call(x, mod3, g, w, n_col_tiles):
    n = w.shape[1]
    tn = n // n_col_tiles
    return pl.pallas_call(
        _in_kernel,
        grid=(n_col_tiles, x.shape[0] // TM),
        in_specs=[
            pl.BlockSpec((TM, D_MODEL), lambda j, i: (i, 0)),
            pl.BlockSpec((1, 1, D_MODEL), lambda j, i: (_cond_row(i) * 6 + 0, 0, 0)),
            pl.BlockSpec((1, 1, D_MODEL), lambda j, i: (_cond_row(i) * 6 + 1, 0, 0)),
            pl.BlockSpec((1, D_MODEL), lambda j, i: (0, 0)),
            pl.BlockSpec((D_MODEL, tn), lambda j, i: (0, j)),
        ],
        out_specs=pl.BlockSpec((TM, tn), lambda j, i: (i, j)),
        out_shape=jax.ShapeDtypeStruct((x.shape[0], n), F32),
        compiler_params=_cparams(("parallel", "parallel")),
        name="in_proj",
    )(x, mod3, mod3, g, w)


def _out_kernel(x_ref, m_ref, w_ref, g1_ref, sh_ref, sc_ref, g_ref, xo_ref, h_ref):
    x = x_ref[...] + g1_ref[0] * _dot(m_ref[...], w_ref[...])
    xo_ref[...] = x
    h_ref[...] = _modulated_norm(x, g_ref[...], sc_ref[0], sh_ref[0])


def _out_call(x, m, w_out, mod3, g2):
    spec_tok = pl.BlockSpec((TM, D_MODEL), lambda i: (i, 0))

    def mod_spec(k):
        return pl.BlockSpec((1, 1, D_MODEL), lambda i: (_cond_row(i) * 6 + k, 0, 0))

    return pl.pallas_call(
        _out_kernel,
        grid=(x.shape[0] // TM,),
        in_specs=[
            spec_tok,
            spec_tok,
            pl.BlockSpec((D_MODEL, D_MODEL), lambda i: (0, 0)),
            mod_spec(2), mod_spec(3), mod_spec(4),
            pl.BlockSpec((1, D_MODEL), lambda i: (0, 0)),
        ],
        out_specs=[spec_tok, spec_tok],
        out_shape=[jax.ShapeDtypeStruct(x.shape, F32)] * 2,
        compiler_params=_cparams(("parallel",)),
        name="out_proj",
    )(x, m, w_out, mod3, mod3, mod3, g2)


CHUNK = 64
SUB = 16
NSUB = CHUNK // SUB
NEG_BIG = -1e30
HPS = 4


def _tri(n, lower, strict=False):
    r = lax.broadcasted_iota(jnp.int32, (n, n), 0)
    c = lax.broadcasted_iota(jnp.int32, (n, n), 1)
    if lower:
        return (c < r) if strict else (c <= r)
    return (c > r) if strict else (c >= r)


def _lockstep(gens):
    results = [None] * len(gens)
    active = list(enumerate(gens))
    while active:
        still = []
        for i, g in active:
            try:
                next(g)
                still.append((i, g))
            except StopIteration as stop:
                results[i] = stop.value
        active = still
    return results


def _gla_chunk(q, k, v, la, st, reverse):
    kdim = q.shape[1]
    tri = jnp.where(_tri(CHUNK, not reverse), 1.0, 0.0).astype(BF16)
    b = _dot_sel(tri, la)
    yield
    last = 0 if reverse else CHUNK - 1
    b_last = b[last:last + 1, :]
    o_inter = _dot_nt((q * jnp.exp(b)).astype(BF16), st.astype(BF16))
    kd = k * jnp.exp(b_last - b)
    st_new = st * jnp.exp(b_last) + _dot_tn(v.astype(BF16), kd.astype(BF16))
    yield

    ones_k = jnp.ones((kdim, LANES), BF16)
    jrow = lax.broadcasted_iota(jnp.int32, (SUB, kdim), 0)
    rsel = lax.broadcasted_iota(jnp.int32, (SUB, SUB * SUB), 0)
    csel = lax.broadcasted_iota(jnp.int32, (SUB, SUB * SUB), 1)
    in_grp = jnp.logical_and(csel >= rsel * SUB, csel < rsel * SUB + SUB)
    sel = jnp.where(in_grp, 1.0, 0.0).astype(BF16)
    outs = []
    for blk in range(NSUB):
        r0 = blk * SUB
        q_i, k_i, b_i, v_i = q[r0:r0 + SUB], k[r0:r0 + SUB], b[r0:r0 + SUB], v[r0:r0 + SUB]
        rows = []
        for i in range(SUB):
            keep = (jrow >= i) if reverse else (jrow <= i)
            e = jnp.exp(jnp.where(keep, b_i[i:i + 1, :] - b_i, NEG_BIG))
            rows.append((q_i[i:i + 1, :] * k_i) * e)
        p = jnp.concatenate(rows, axis=0)
        att = _dot(p.astype(BF16), ones_k)
        yield
        zv = att * jnp.concatenate([v_i] * SUB, axis=0)
        o_blk = _dot(sel, zv.astype(BF16))
        yield
        if reverse and blk < NSUB - 1:
            r1 = r0 + SUB
            ref = b[r1:r1 + 1, :]
            qt = q_i * jnp.exp(b_i - ref)
            kt = k[r1:] * jnp.exp(ref - b[r1:])
            a_off = _dot_nt(qt.astype(BF16), kt.astype(BF16))
            yield
            o_blk = o_blk + _dot(a_off.astype(BF16), v[r1:].astype(BF16))
        if (not reverse) and blk > 0:
            ref = b[r0 - 1:r0, :]
            qt = q_i * jnp.exp(b_i - ref)
            kt = k[:r0] * jnp.exp(ref - b[:r0])
            a_off = _dot_nt(qt.astype(BF16), kt.astype(BF16))
            yield
            o_blk = o_blk + _dot(a_off.astype(BF16), v[:r0].astype(BF16))
        outs.append(o_blk)
    return o_inter + jnp.concatenate(outs, axis=0), st_new


def _log_sigmoid(x):
    return -(jnp.maximum(-x, 0.0) + jnp.log1p(jnp.exp(-jnp.abs(x))))


def _head_norm_gate(o, g, gate):
    o = o * lax.rsqrt(jnp.mean(o * o, axis=-1, keepdims=True) + NORM_EPS)
    return (o * g) * (gate * jax.nn.sigmoid(gate))


def _even_kernel(*refs, mode, seq_len, with_state_out, has_s0):
    it = iter(refs)
    if mode == "gla":
        q_ref, k_ref, v_ref, lr_ref, gate_ref, wg_ref, bg_ref, gn_ref = (next(it) for _ in range(8))
    else:
        q_ref, ff_ref, fb_ref, v_ref, gate_ref, lb_ref, gn_ref = (next(it) for _ in range(7))
    s0_ref = next(it) if has_s0 else None
    m_ref = next(it)
    so_ref = next(it) if with_state_out else None
    of_scr, ob_scr = next(it), next(it)
    n_chunks = seq_len // CHUNK
    streams = [(hh, d) for hh in range(HPS) for d in range(2)]

    def chain(c, hh, d, st):
        rows = pl.ds(pl.multiple_of(c * CHUNK, CHUNK), CHUNK)
        cols = slice(hh * LANES, (hh + 1) * LANES)
        if mode == "gla":
            q = q_ref[rows, cols] * (GLA_DK ** -0.5)
            k = k_ref[rows, cols]
            pre = _dot_hi(lr_ref[rows, :], wg_ref[hh, d]) + bg_ref[hh, d]
            yield
            la = _log_sigmoid(pre) / GLA_TAU
        else:
            q = q_ref[rows, cols]
            f = (ff_ref, fb_ref)[d][rows, cols]
            lb = lb_ref[d:d + 1, cols]
            gate = lb + (1.0 - lb) * jax.nn.sigmoid(f)
            la = jnp.log(jnp.maximum(gate, GATE_FLOOR))
            k = (1.0 - lb) * jax.nn.sigmoid(-f)
        o, st_new = yield from _gla_chunk(q, k, v_ref[rows, cols], la, st, d == 1)
        return rows, cols, o, st_new

    def body(c, carry):
        res = _lockstep([chain(n_chunks - 1 - c if d else c, hh, d, carry[i])
                         for i, (hh, d) in enumerate(streams)])
        for (hh, d), (rows, cols, o, _) in zip(streams, res):
            (ob_scr if d else of_scr)[rows, cols] = o
        return tuple(r[3] for r in res)

    if has_s0:
        init = tuple(s0_ref[0, d, hh].T for hh, d in streams)
    else:
        init = tuple(jnp.zeros((LANES, LANES), F32) for _ in streams)
    fin = lax.fori_loop(0, n_chunks, body, init)
    for hh in range(HPS):
        cols = slice(hh * LANES, (hh + 1) * LANES)
        m_ref[:, cols] = _head_norm_gate(of_scr[:, cols] + ob_scr[:, cols], gn_ref[...],
                                         gate_ref[:, cols]).astype(m_ref.dtype)
    if with_state_out:
        for i, (hh, d) in enumerate(streams):
            so_ref[0, d, hh] = fin[i].T


def _even_params(w_gate, b_gate, gla_norm, hg_lb, hg_norm, e):
    wg = jnp.zeros((GLA_HEADS, 2, LANES, LANES), F32)
    bg = jnp.zeros((GLA_HEADS, 2, 1, LANES), F32)
    for h in range(GLA_HEADS):
        for d in range(2):
            wg = wg.at[h, d, d * GLA_RANK:(d + 1) * GLA_RANK, :GLA_DK].set(
                w_gate[d, :, h * GLA_DK:(h + 1) * GLA_DK].astype(F32))
            bg = bg.at[h, d, 0, :GLA_DK].set(b_gate[d, h * GLA_DK:(h + 1) * GLA_DK].astype(F32))
    lbp = jax.nn.softmax(hg_lb.astype(F32), axis=1)
    lb = (jnp.cumsum(lbp, axis=1) - lbp[:, :1])[:, e]
    return {"gla": (wg, bg, gla_norm.astype(F32).reshape(1, LANES)),
            "hgrn": (lb, hg_norm.astype(F32).reshape(1, LANES))}


def _even_call(z, mode, col, params, s0, seq_len, n_seq, row0, with_state_out):
    heads = 4
    blk0 = row0 // seq_len
    wide = HPS * LANES

    def tok(name):
        c0 = col[name] // HPS
        return pl.BlockSpec((seq_len, wide), lambda s, h: (blk0 + s, c0 + h), pipeline_mode=pl.Buffered(1))

    if mode == "gla":
        wg, bg, gn = params
        in_specs = [tok("gq"), tok("gk"), tok("gv"),
                    pl.BlockSpec((seq_len, LANES), lambda s, h: (blk0 + s, col["lr"])),
                    tok("gr"),
                    pl.BlockSpec((HPS, 2, LANES, LANES), lambda s, h: (h, 0, 0, 0)),
                    pl.BlockSpec((HPS, 2, 1, LANES), lambda s, h: (h, 0, 0, 0)),
                    pl.BlockSpec((1, LANES), lambda s, h: (0, 0))]
        args = [z, z, z, z, z, wg, bg, gn]
    else:
        lb, gn = params
        in_specs = [tok("hq"), tok("hf_f"), tok("hf_b"), tok("hi"), tok("hgate"),
                    pl.BlockSpec((2, wide), lambda s, h: (0, h)),
                    pl.BlockSpec((1, LANES), lambda s, h: (0, 0))]
        args = [z, z, z, z, z, lb, gn]
    st_spec = pl.BlockSpec((1, 2, HPS, LANES, LANES), lambda s, h: (s, 0, h, 0, 0))
    has_s0 = s0 is not None
    if has_s0:
        in_specs.append(st_spec)
        args.append(s0)
    out_specs = [pl.BlockSpec((seq_len, wide), lambda s, h: (s, h))]
    out_shape = [jax.ShapeDtypeStruct((n_seq * seq_len, heads * LANES), BF16)]
    if with_state_out:
        out_specs.append(st_spec)
        out_shape.append(jax.ShapeDtypeStruct((n_seq, 2, heads, LANES, LANES), F32))
    res = pl.pallas_call(
        functools.partial(_even_kernel, mode=mode, seq_len=seq_len,
                          with_state_out=with_state_out, has_s0=has_s0),
        grid=(n_seq, heads // HPS),
        in_specs=in_specs,
        out_specs=out_specs,
        out_shape=out_shape,
        scratch_shapes=[pltpu.VMEM((seq_len, wide), F32), pltpu.VMEM((seq_len, wide), F32)],
        compiler_params=_cparams(("parallel", "parallel")),
        name=f"mix_{mode}_{seq_len}",
    )(*args)
    return res


CONV_PAD = SUBLANES


def _conv_kernel(x_ref, w_ref, o_ref, pad_scr, *, seq_len):
    cb = pl.program_id(1)
    wide = DN_HEADS * LANES
    zeros = jnp.zeros((CONV_PAD, wide), F32)
    pad_scr[0:CONV_PAD, :] = zeros
    pad_scr[CONV_PAD + seq_len:2 * CONV_PAD + seq_len, :] = zeros
    pad_scr[CONV_PAD:CONV_PAD + seq_len, :] = x_ref[...]
    is_q = cb == 0
    is_qk = cb < 2
    scale = jnp.where(is_q, DN_DK ** -0.5, 1.0)
    for hh in range(DN_HEADS):
        cols = slice(hh * LANES, (hh + 1) * LANES)
        acc = jnp.zeros((seq_len, LANES), F32)
        for w in range(DN_CONV):
            acc = acc + pad_scr[pl.ds(CONV_PAD + w - DN_CONV // 2, seq_len), cols] * w_ref[w:w + 1, cols]
        y = acc * jax.nn.sigmoid(acc)
        yn = y * lax.rsqrt(jnp.sum(y * y, axis=-1, keepdims=True) + NORM_EPS)
        o_ref[:, cols] = jnp.where(is_qk, yn * scale, y)


def _conv_call(z, col0, w, seq_len, n_seq, row0):
    blk0 = row0 // seq_len
    wide = DN_HEADS * LANES
    c0 = col0 // DN_HEADS
    return pl.pallas_call(
        functools.partial(_conv_kernel, seq_len=seq_len),
        grid=(n_seq, 3),
        in_specs=[pl.BlockSpec((seq_len, wide), lambda s, c: (blk0 + s, c0 + c)),
                  pl.BlockSpec((DN_CONV, wide), lambda s, c: (0, c))],
        out_specs=pl.BlockSpec((seq_len, wide), lambda s, c: (s, c)),
        out_shape=jax.ShapeDtypeStruct((n_seq * seq_len, 3 * wide), F32),
        scratch_shapes=[pltpu.VMEM((seq_len + 2 * CONV_PAD, wide), F32)],
        compiler_params=_cparams(("parallel", "parallel")),
        name=f"dn_conv_{seq_len}",
    )(z, w)


def _softplus(x):
    return jnp.maximum(x, 0.0) + jnp.log1p(jnp.exp(-jnp.abs(x)))


def _dn_chunk(q, k, v, g, beta, s, reverse):
    c = CHUNK
    lower = not reverse
    tri = jnp.where(_tri(c, lower), 1.0, 0.0).astype(BF16)
    gam = _dot_sel(tri, g)
    yield
    ones_c = jnp.ones((c, c), BF16)
    gam_row = _dot_sel(ones_c, jnp.where(_tri(c, reverse), g[:, :c], 0.0))
    incl = _tri(c, lower)
    strict = _tri(c, lower, strict=True)
    diff = gam[:, :c] - gam_row
    dec = jnp.where(incl, jnp.exp(jnp.where(incl, diff, 0.0)), 0.0)
    kb = k * beta
    k16 = k.astype(BF16)
    m = jnp.where(strict, _dot_nt(kb.astype(BF16), k16) * dec, 0.0)
    yield
    eg = jnp.exp(gam)
    x = jnp.concatenate([v * beta, kb * eg], axis=1)
    p = -m
    n_fac = int(np.log2(c))
    for j in range(n_fac):
        x = x + _dot_hi(p, x)
        if j + 1 < n_fac:
            p = _dot_hi(p, p)
        yield
    u, w = x[:, :LANES], x[:, LANES:]
    s16 = s.astype(BF16)
    v_new = u - _dot(w.astype(BF16), s16)
    att = _dot_nt(q.astype(BF16), k16) * dec
    yield
    o = _dot((q * eg).astype(BF16), s16) + _dot(att.astype(BF16), v_new.astype(BF16))
    last = 0 if reverse else c - 1
    gl = gam[last:last + 1, :]
    s_new = s * jnp.exp(gl) + _dot_tn((k * jnp.exp(gl - gam)).astype(BF16), v_new.astype(BF16))
    return o, s_new


def _dn_kernel(*refs, seq_len, with_state_out, has_s0):
    it = iter(refs)
    q_ref, k_ref, v_ref, ab_ref, gate_ref, alog_ref, dtb_ref, gn_ref = (next(it) for _ in range(8))
    s0_ref = next(it) if has_s0 else None
    m_ref = next(it)
    so_ref = next(it) if with_state_out else None
    of_scr, ob_scr = next(it), next(it)
    h0 = pl.program_id(1) * HPS
    n_chunks = seq_len // CHUNK
    srow = lax.broadcasted_iota(jnp.int32, (LANES, LANES), 0)
    streams = [(hh, d) for hh in range(HPS) for d in range(2)]

    def chain(c, hh, d, s):
        rows = pl.ds(pl.multiple_of(c * CHUNK, CHUNK), CHUNK)
        cols = slice(hh * LANES, (hh + 1) * LANES)
        ab = ab_ref[rows, :]
        sel_a = jnp.where(srow == DN_HEADS * d + h0 + hh, 1.0, 0.0).astype(BF16)
        sel_b = jnp.where(srow == 2 * DN_HEADS + DN_HEADS * d + h0 + hh, 1.0, 0.0).astype(BF16)
        a = _dot_x_sel(ab, sel_a)
        beta = jax.nn.sigmoid(_dot_x_sel(ab, sel_b))
        g = -jnp.exp(alog_ref[hh, d]) * _softplus(a + dtb_ref[hh, d])
        yield
        o, s_new = yield from _dn_chunk(q_ref[rows, cols], k_ref[rows, cols], v_ref[rows, cols],
                                        g, beta, s, d == 1)
        return rows, cols, o, s_new

    def body(c, carry):
        res = _lockstep([chain(n_chunks - 1 - c if d else c, hh, d, carry[i])
                         for i, (hh, d) in enumerate(streams)])
        for (hh, d), (rows, cols, o, _) in zip(streams, res):
            (ob_scr if d else of_scr)[rows, cols] = o
        return tuple(r[3] for r in res)

    if has_s0:
        init = tuple(s0_ref[0, d, hh] for hh, d in streams)
    else:
        init = tuple(jnp.zeros((LANES, LANES), F32) for _ in streams)
    fin = lax.fori_loop(0, n_chunks, body, init)
    for hh in range(HPS):
        cols = slice(hh * LANES, (hh + 1) * LANES)
        m_ref[:, cols] = _head_norm_gate(of_scr[:, cols] + ob_scr[:, cols], gn_ref[...],
                                         gate_ref[:, cols]).astype(m_ref.dtype)
    if with_state_out:
        for i, (hh, d) in enumerate(streams):
            so_ref[0, d, hh] = fin[i]


S5_TC = 128
S5_CB = 4
S5_SW = S5_G * S5_P // S5_CB


def _s5_params(lam_re, lam_im, log_dt, b_re, b_im, c_re, c_im):
    gpb = S5_G // S5_CB
    eye = jnp.eye(gpb, dtype=F32)
    n = jnp.arange(1, S5_TC + 1, dtype=F32)[:, None, None]
    bbs, ccs, aps = [], [], []
    for d in range(2):
        lr = jnp.minimum(lam_re[d].astype(F32), -1e-4)
        li = lam_im[d].astype(F32)
        dt = jnp.exp(log_dt[d].astype(F32))[:, None]
        mag = jnp.exp(lr * dt)
        ar, ai = mag * jnp.cos(li * dt), mag * jnp.sin(li * dt)
        den = lr * lr + li * li
        nr = ar - 1.0
        cr = (nr * lr + ai * li) / den
        ci = (ai * lr - nr * li) / den
        bbr = cr[..., None] * b_re - ci[..., None] * b_im
        bbi = cr[..., None] * b_im + ci[..., None] * b_re
        blk = lambda t: jnp.einsum('bgpc,gh->bgchp', t.reshape(S5_CB, gpb, S5_P, S5_GS), eye).reshape(
            S5_CB, gpb * S5_GS, gpb * S5_P)
        bbs.append(jnp.stack([blk(bbr), blk(bbi)]))
        cblk = lambda t: jnp.einsum('bgcp,gh->bgphc', t.reshape(S5_CB, gpb, S5_GS, S5_P), eye).reshape(
            S5_CB, gpb * S5_P, gpb * S5_GS)
        ccs.append(jnp.stack([cblk(c_re[d].astype(F32)), cblk(c_im[d].astype(F32))]))
        pm = jnp.exp(n * (lr * dt)[None])
        pr = (pm * jnp.cos(n * (li * dt)[None])).reshape(S5_TC, -1)
        pi = (pm * jnp.sin(n * (li * dt)[None])).reshape(S5_TC, -1)
        if d == 1:
            pr, pi = pr[::-1], pi[::-1]
        aps.append(jnp.stack([pr, pi]))
    bb = jnp.stack(bbs, axis=2).astype(BF16)
    cc = jnp.stack(ccs, axis=2).astype(BF16)
    apow = jnp.stack(aps, axis=1)
    return bb, cc, apow


def _s5_scan(xr, xi, pr, pi, cr, ci, reverse, loc, loc0):
    tc, w = xr.shape
    g8 = SUBLANES
    ng = tc // g8

    def power(n):
        idx = tc - n if reverse else n - 1
        return pr[idx:idx + 1, :], pi[idx:idx + 1, :]

    def step(xr, xi, s, unit):
        n = xr.shape[0]
        row = lax.broadcasted_iota(jnp.int32, xr.shape, 0)
        period = g8 if unit == 1 else n
        pos = jnp.bitwise_and(row, period - 1) if unit == 1 else row
        keep = (pos < period - s) if reverse else (pos >= s)
        shift = n - s if reverse else s
        a_r, a_i = power(s * unit)
        if unit == 1:
            rot = lambda x: pltpu.roll(x.reshape(n // g8, g8, w), g8 - s if reverse else s, 1).reshape(n, w)
        else:
            rot = lambda x: pltpu.roll(x, shift, 0)
        sr = jnp.where(keep, rot(xr), 0.0)
        si = jnp.where(keep, rot(xi), 0.0)
        return xr + (a_r * sr - a_i * si), xi + (a_r * si + a_i * sr)

    s = 1
    while s < g8:
        xr, xi = step(xr, xi, s, 1)
        s *= 2
    end = 0 if reverse else g8 - 1
    ers, eis = [], []
    for k in range(w // LANES):
        r0, i0 = (loc0 + k) * tc, (loc0 + w // LANES + k) * tc
        loc[r0:r0 + tc, :] = xr[:, k * LANES:(k + 1) * LANES]
        loc[i0:i0 + tc, :] = xi[:, k * LANES:(k + 1) * LANES]
        ers.append(loc[pl.ds(r0 + end, ng, stride=g8), :])
        eis.append(loc[pl.ds(i0 + end, ng, stride=g8), :])
    er = jnp.concatenate(ers, axis=1)
    ei = jnp.concatenate(eis, axis=1)
    s = 1
    while s < ng:
        er, ei = step(er, ei, s, g8)
        s *= 2
    grow = lax.broadcasted_iota(jnp.int32, (ng, w), 0)
    if reverse:
        pin_r = jnp.concatenate([pr[g8 * (g + 1):g8 * (g + 1) + 1] for g in range(ng - 1)] + [jnp.ones((1, w), F32)], 0)
        pin_i = jnp.concatenate([pi[g8 * (g + 1):g8 * (g + 1) + 1] for g in range(ng - 1)] + [jnp.zeros((1, w), F32)], 0)
        prev_r = jnp.where(grow < ng - 1, pltpu.roll(er, ng - 1, 0), 0.0)
        prev_i = jnp.where(grow < ng - 1, pltpu.roll(ei, ng - 1, 0), 0.0)
    else:
        pin_r = jnp.concatenate([jnp.ones((1, w), F32)] + [pr[g8 * g - 1:g8 * g] for g in range(1, ng)], 0)
        pin_i = jnp.concatenate([jnp.zeros((1, w), F32)] + [pi[g8 * g - 1:g8 * g] for g in range(1, ng)], 0)
        prev_r = jnp.where(grow >= 1, pltpu.roll(er, 1, 0), 0.0)
        prev_i = jnp.where(grow >= 1, pltpu.roll(ei, 1, 0), 0.0)
    in_r = prev_r + (pin_r * cr - pin_i * ci)
    in_i = prev_i + (pin_r * ci + pin_i * cr)
    in_r = jnp.concatenate([jnp.broadcast_to(in_r[g:g + 1], (g8, w)) for g in range(ng)], axis=0)
    in_i = jnp.concatenate([jnp.broadcast_to(in_i[g:g + 1], (g8, w)) for g in range(ng)], axis=0)
    p8r = pr[tc - g8:] if reverse else pr[:g8]
    p8i = pi[tc - g8:] if reverse else pi[:g8]
    a_r = jnp.concatenate([p8r] * ng, axis=0)
    a_i = jnp.concatenate([p8i] * ng, axis=0)
    return xr + (a_r * in_r - a_i * in_i), xi + (a_r * in_i + a_i * in_r)


def _s5_kernel(*refs, seq_len, has_h0):
    it = iter(refs)
    u_ref, bb_ref, cc_ref, ap_ref, dsk_ref = (next(it) for _ in range(5))
    h0r_ref, h0i_ref = (next(it), next(it)) if has_h0 else (None, None)
    y_ref, fr_ref, fi_ref = next(it), next(it), next(it)
    yf_scr, yb_scr, loc_scr = next(it), next(it), next(it)
    n_chunks = seq_len // S5_TC

    def run(c, d, carry):
        cr, ci = carry
        rows = pl.ds(pl.multiple_of(c * S5_TC, S5_TC), S5_TC)
        ub = u_ref[rows, :].astype(BF16)
        pr, pi = ap_ref[0, d], ap_ref[1, d]
        xr, xi = _dot(ub, bb_ref[0, 0, d]), _dot(ub, bb_ref[1, 0, d])
        hr, hi = _s5_scan(xr, xi, pr, pi, cr, ci, d == 1, loc_scr, d * 2 * (S5_SW // LANES))
        (yf_scr, yb_scr)[d][rows, :] = (_dot(hr.astype(BF16), cc_ref[0, 0, d])
                                         - _dot(hi.astype(BF16), cc_ref[1, 0, d]))
        last = 0 if d == 1 else S5_TC - 1
        return hr[last:last + 1, :], hi[last:last + 1, :]

    def body(c, carry):
        cf, cb = carry
        return run(c, 0, cf), run(n_chunks - 1 - c, 1, cb)

    if has_h0:
        init = tuple((h0r_ref[0, d:d + 1, :], h0i_ref[0, d:d + 1, :]) for d in range(2))
    else:
        z = jnp.zeros((1, S5_SW), F32)
        init = ((z, z), (z, z))
    fin = lax.fori_loop(0, n_chunks, body, init)
    y_ref[...] = dsk_ref[...] * u_ref[...] + (yf_scr[...] + yb_scr[...])
    for d in range(2):
        fr_ref[0, d:d + 1, :] = fin[d][0]
        fi_ref[0, d:d + 1, :] = fin[d][1]


def _s5_call(z, col0, bb, cc, apow, dsk, h0r, h0i, seq_len, n_seq, row0):
    blk0 = row0 // seq_len
    has_h0 = h0r is not None
    st_spec = pl.BlockSpec((1, 2, S5_SW), lambda s, c: (s, 0, c))
    in_specs = [pl.BlockSpec((seq_len, LANES), lambda s, c: (blk0 + s, col0 + c)),
                pl.BlockSpec((2, 1, 2, LANES, S5_SW), lambda s, c: (0, c, 0, 0, 0)),
                pl.BlockSpec((2, 1, 2, S5_SW, LANES), lambda s, c: (0, c, 0, 0, 0)),
                pl.BlockSpec((2, 2, S5_TC, S5_SW), lambda s, c: (0, 0, 0, c)),
                pl.BlockSpec((1, LANES), lambda s, c: (0, c))]
    args = [z, bb, cc, apow, dsk]
    if has_h0:
        in_specs += [st_spec, st_spec]
        args += [h0r, h0i]
    st_shape = jax.ShapeDtypeStruct((n_seq, 2, S5_G * S5_P), F32)
    return pl.pallas_call(
        functools.partial(_s5_kernel, seq_len=seq_len, has_h0=has_h0),
        grid=(n_seq, S5_CB),
        in_specs=in_specs,
        out_specs=[pl.BlockSpec((seq_len, LANES), lambda s, c: (s, c)), st_spec, st_spec],
        out_shape=[jax.ShapeDtypeStruct((n_seq * seq_len, MIX_W), F32), st_shape, st_shape],
        scratch_shapes=[pltpu.VMEM((seq_len, LANES), F32), pltpu.VMEM((seq_len, LANES), F32),
                        pltpu.VMEM((4 * (S5_SW // LANES) * S5_TC, LANES), F32)],
        compiler_params=_cparams(("parallel", "parallel")),
        name=f"mix_s5_{seq_len}",
    )(*args)


def _glu_kernel(y_ref, w_ref, b_ref, o_ref):
    zz = jax.nn.gelu(y_ref[...])
    o_ref[...] = (zz * jax.nn.sigmoid(_dot(zz.astype(BF16), w_ref[...]) + b_ref[...])).astype(o_ref.dtype)


def _glu_call(y, w, b):
    t = y.shape[0]
    return pl.pallas_call(
        _glu_kernel,
        grid=(t // TM,),
        in_specs=[pl.BlockSpec((TM, MIX_W), lambda i: (i, 0)),
                  pl.BlockSpec((MIX_W, MIX_W), lambda i: (0, 0)),
                  pl.BlockSpec((1, MIX_W), lambda i: (0, 0))],
        out_specs=pl.BlockSpec((TM, MIX_W), lambda i: (i, 0)),
        out_shape=jax.ShapeDtypeStruct((t, MIX_W), BF16),
        compiler_params=_cparams(("parallel",)),
        name="s5_glu",
    )(y, w, b)


PK_HP = 2 * PK_HEADS
PK_SIDE = PK_KEYS
PK_PICKS = PK_HEADS * PK_TOPK
TK_TM = 128
TK_HG = 2
WB_TM = 128
WB_GRP = 16
WB_STRIDE = PK_SIDE + SUBLANES
PM_TM = 1024
PM_SLABS = 8


def _score_kernel(h_ref, wq_ref, keys_ref, s_ref):
    q = _dot(h_ref[...].astype(BF16), wq_ref[...])
    for hp in range(PK_HP):
        s_ref[hp] = _dot_nt(keys_ref[hp], q[:, hp * LANES:(hp + 1) * LANES].astype(BF16))


def _score_call(h, wq, keys):
    return pl.pallas_call(
        _score_kernel,
        grid=(h.shape[0] // TM,),
        in_specs=[pl.BlockSpec((TM, D_MODEL), lambda i: (i, 0)),
                  pl.BlockSpec((D_MODEL, PK_HP * LANES), lambda i: (0, 0)),
                  pl.BlockSpec((PK_HP, PK_KEYS, LANES), lambda i: (0, 0, 0))],
        out_specs=pl.BlockSpec((PK_HP, PK_KEYS, TM), lambda i: (0, 0, i)),
        out_shape=jax.ShapeDtypeStruct((PK_HP, PK_KEYS, h.shape[0]), F32),
        compiler_params=_cparams(("parallel",)),
        name="peer_scores",
    )(h, wq, keys)


def _top16_sorted(s):
    n_grp = s.shape[0] // SUBLANES
    n = s.shape[1]
    base = lax.broadcasted_iota(jnp.int32, (SUBLANES, n), 0).astype(F32)
    vals = [s[SUBLANES * g:SUBLANES * (g + 1), :] for g in range(n_grp)]
    keys = [base + float(SUBLANES * g) for g in range(n_grp)]
    for rnd in range(n_grp):
        for k in range(rnd % 2, n_grp - 1, 2):
            a, b, ka, kb = vals[k], vals[k + 1], keys[k], keys[k + 1]
            swap = b > a
            vals[k], vals[k + 1] = jnp.maximum(a, b), jnp.minimum(a, b)
            keys[k], keys[k + 1] = jnp.where(swap, kb, ka), jnp.where(swap, ka, kb)
    rank = lax.broadcasted_iota(jnp.int32, (PK_TOPK, n), 0)
    top_v = jnp.zeros((PK_TOPK, n), F32)
    top_k = jnp.zeros((PK_TOPK, n), F32)
    for r in range(PK_TOPK):
        m = jnp.max(vals[0], axis=0, keepdims=True)
        am = jnp.min(jnp.where(vals[0] == m, keys[0], float(s.shape[0])), axis=0, keepdims=True)
        top_v = jnp.where(rank == r, m, top_v)
        top_k = jnp.where(rank == r, am, top_k)
        hit = keys[0] == am
        for k in range(min(n_grp - 1, PK_TOPK - 1 - r)):
            vals[k] = jnp.where(hit, vals[k + 1], vals[k])
            keys[k] = jnp.where(hit, keys[k + 1], keys[k])
    return top_v, top_k


def _top16_rows(tiles):
    shape = tiles[0].shape
    row = lax.broadcasted_iota(jnp.int32, shape, 0).astype(F32)
    out_row = lax.broadcasted_iota(jnp.int32, (PK_TOPK, shape[1]), 0)

    def take(r, s, vals, idxs):
        m = jnp.max(s, axis=0, keepdims=True)
        am = jnp.min(jnp.where(s == m, row, float(shape[0])), axis=0, keepdims=True)
        at = out_row == r
        return jnp.where(row == am, -jnp.inf, s), jnp.where(at, m, vals), jnp.where(at, am, idxs)

    def body(r, carry):
        out = ()
        for i in range(len(tiles)):
            out += take(r, *carry[3 * i:3 * i + 3])
        return out

    zero = jnp.zeros((PK_TOPK, shape[1]), F32)
    init = ()
    for s in tiles:
        init += (s, zero, zero)
    res = lax.fori_loop(0, PK_TOPK, body, init)
    return [(res[3 * i + 1], res[3 * i + 2]) for i in range(len(tiles))]


def _pair_candidates(v1, v2):
    half = PK_TOPK // 2
    n = v1.shape[1]
    rows = [jnp.broadcast_to(v1[0:1], (PK_TOPK, n)) + v2]
    rows += [jnp.broadcast_to(v1[a:a + 1], (half, n)) + v2[:half] for a in range(1, half)]
    rows.append(v1[half:] + jnp.broadcast_to(v2[0:1], (half, n)))
    return jnp.concatenate(rows, axis=0)


def _topk_kernel(s_ref, i1_ref, i2_ref, g_ref, p1_scr, p2_scr, g_scr):
    n = TK_TM
    zero16 = jnp.zeros((PK_TOPK, n), F32)
    rank = lax.broadcasted_iota(jnp.int32, (PK_TOPK, n), 0)
    rank_f = rank.astype(F32)
    flat = _pair_candidates(rank_f * PK_TOPK, rank_f)
    out_row = rank

    def heads(hg, carry):
        tops = [_top16_sorted(s_ref[2 * TK_HG * hg + i]) for i in range(2 * TK_HG)]
        c1s, c2s, init = [], [], ()
        for j in range(TK_HG):
            (v1, a1), (v2, a2) = tops[2 * j], tops[2 * j + 1]
            c1s.append(_pair_candidates(a1, zero16))
            c2s.append(_pair_candidates(zero16, a2))
            init += (_pair_candidates(v1, v2), zero16, zero16, zero16)

        def pick(r, c1, c2, cand, best, p1, p2):
            m = jnp.max(cand, axis=0, keepdims=True)
            pos = jnp.min(jnp.where(cand == m, flat, float(PK_TOPK * PK_TOPK)), axis=0, keepdims=True)
            hit = flat == pos
            e1 = jnp.max(jnp.where(hit, c1, -1.0), axis=0, keepdims=True)
            e2 = jnp.max(jnp.where(hit, c2, -1.0), axis=0, keepdims=True)
            at = out_row == r
            return (jnp.where(hit, -jnp.inf, cand), jnp.where(at, m, best),
                    jnp.where(at, e1, p1), jnp.where(at, e2, p2))

        def body(r, carry):
            out = ()
            for j in range(TK_HG):
                out += pick(r, c1s[j], c2s[j], *carry[4 * j:4 * j + 4])
            return out

        res = lax.fori_loop(0, PK_TOPK, body, init)
        for j in range(TK_HG):
            _, best, p1, p2 = res[4 * j:4 * j + 4]
            ex = jnp.exp(best - jnp.max(best, axis=0, keepdims=True))
            rows = pl.ds(pl.multiple_of((hg * TK_HG + j) * PK_TOPK, PK_TOPK), PK_TOPK)
            p1_scr[rows, :] = p1
            p2_scr[rows, :] = p2
            g_scr[rows, :] = ex / jnp.sum(ex, axis=0, keepdims=True)
        return carry

    lax.fori_loop(0, PK_HEADS // TK_HG, heads, 0)
    i1_ref[...] = p1_scr[...].T
    i2_ref[...] = p2_scr[...].T
    g_ref[...] = g_scr[...].T


def _topk_call(s):
    t = s.shape[2]
    spec = pl.BlockSpec((TK_TM, PK_PICKS), lambda i: (i, 0))
    shp = jax.ShapeDtypeStruct((t, PK_PICKS), F32)
    return pl.pallas_call(
        _topk_kernel,
        grid=(t // TK_TM,),
        in_specs=[pl.BlockSpec((PK_HP, PK_KEYS, TK_TM), lambda i: (0, 0, i))],
        out_specs=[spec, spec, spec],
        out_shape=[shp, shp, shp],
        scratch_shapes=[pltpu.VMEM((PK_PICKS, TK_TM), F32)] * 3,
        compiler_params=_cparams(("parallel",)),
        name="peer_topk",
    )(s)


def _wbuild_kernel(i1_ref, i2_ref, g_ref, w_ref, stage_a, stage_b):
    sub = lax.broadcasted_iota(jnp.int32, (PK_SIDE, PK_PICKS), 0).astype(F32).astype(BF16)
    one = jnp.ones((PK_SIDE, PK_PICKS), BF16)
    zero = jnp.zeros((PK_SIDE, PK_PICKS), BF16)

    def grids(t0, stage):
        for tt in range(WB_GRP):
            r1 = i1_ref[pl.ds(t0 + tt, 1), :].astype(BF16)
            r2 = i2_ref[pl.ds(t0 + tt, 1), :].astype(BF16)
            rg = g_ref[pl.ds(t0 + tt, 1), :].astype(BF16)
            p1t = jnp.where(sub == jnp.broadcast_to(r1, sub.shape), one, zero)
            q2t = jnp.where(sub == jnp.broadcast_to(r2, sub.shape), jnp.broadcast_to(rg, sub.shape), zero)
            stage[tt * WB_STRIDE:tt * WB_STRIDE + PK_SIDE, :] = _dot_nt(p1t, q2t)

    def regroup(t0, stage):
        for i1 in range(PK_SIDE):
            lo = stage[pl.ds(i1, SUBLANES, stride=WB_STRIDE), :]
            hi = stage[pl.ds(i1 + SUBLANES * WB_STRIDE, SUBLANES, stride=WB_STRIDE), :]
            w_ref[0, i1, pl.ds(t0, WB_GRP), :] = jnp.concatenate([lo, hi], axis=0).astype(BF16)

    def pair(gi, carry):
        t_a = pl.multiple_of(gi * 2 * WB_GRP, WB_GRP)
        t_b = pl.multiple_of(gi * 2 * WB_GRP + WB_GRP, WB_GRP)
        grids(t_a, stage_a)
        grids(t_b, stage_b)
        regroup(t_a, stage_a)
        regroup(t_b, stage_b)
        return carry

    lax.fori_loop(0, WB_TM // (2 * WB_GRP), pair, 0)


def _wbuild_call(i1, i2, g):
    spec = pl.BlockSpec((WB_TM, PK_PICKS), lambda i: (i, 0))
    return pl.pallas_call(
        _wbuild_kernel,
        grid=(i1.shape[0] // WB_TM,),
        in_specs=[spec, spec, spec],
        out_specs=pl.BlockSpec((1, PK_SIDE, WB_TM, PK_SIDE), lambda i: (i, 0, 0, 0)),
        out_shape=jax.ShapeDtypeStruct((i1.shape[0] // WB_TM, PK_SIDE, WB_TM, PK_SIDE), BF16),
        scratch_shapes=[pltpu.VMEM((WB_GRP * WB_STRIDE, PK_SIDE), F32)] * 2,
        compiler_params=_cparams(("parallel",)),
        name="peer_route",
    )(i1, i2, g)


def _peer_kernel(h_ref, u_ref, v_ref, w_ref, x_ref, g2_ref, o_ref, acc):
    j = pl.program_id(1)

    @pl.when(j == 0)
    def _():
        acc[...] = jnp.zeros_like(acc)

    a = _dot_nt(h_ref[...], u_ref[...].astype(BF16))
    nsub = PM_TM // WB_TM
    w = jnp.concatenate(
        [jnp.concatenate([w_ref[n, sl] for sl in range(PM_SLABS)], axis=1) for n in range(nsub)], axis=0)
    acc[...] += _dot((jax.nn.gelu(a) * w.astype(F32)).astype(BF16), v_ref[...].astype(BF16))

    @pl.when(j == pl.num_programs(1) - 1)
    def _():
        o_ref[...] = x_ref[...] + g2_ref[0] * acc[...]


def _peer_call(h16, u_tab, v_tab, w, x, mod3):
    ne = PM_SLABS * PK_SIDE
    tok = pl.BlockSpec((PM_TM, D_MODEL), lambda i, j: (i, 0))
    return pl.pallas_call(
        _peer_kernel,
        grid=(x.shape[0] // PM_TM, PK_SIDE // PM_SLABS),
        in_specs=[tok,
                  pl.BlockSpec((ne, D_MODEL), lambda i, j: (j, 0)),
                  pl.BlockSpec((ne, D_MODEL), lambda i, j: (j, 0)),
                  pl.BlockSpec((PM_TM // WB_TM, PM_SLABS, WB_TM, PK_SIDE), lambda i, j: (i, j, 0, 0)),
                  tok,
                  pl.BlockSpec((1, 1, D_MODEL), lambda i, j: (_cond_row(i, PM_TM) * 6 + 5, 0, 0))],
        out_specs=tok,
        out_shape=jax.ShapeDtypeStruct(x.shape, F32),
        scratch_shapes=[pltpu.VMEM((PM_TM, D_MODEL), F32)],
        compiler_params=_cparams(("parallel", "arbitrary")),
        name="peer_experts",
    )(h16, u_tab, v_tab, w, x, mod3)


def _final_norm_kernel(x_ref, g_ref, o_ref):
    x = x_ref[...]
    o_ref[...] = x * lax.rsqrt(jnp.mean(x * x, axis=-1, keepdims=True) + NORM_EPS) * g_ref[...]


def _final_norm_call(x, g):
    spec = pl.BlockSpec((TM, D_MODEL), lambda i: (i, 0))
    return pl.pallas_call(
        _final_norm_kernel,
        grid=(x.shape[0] // TM,),
        in_specs=[spec, pl.BlockSpec((1, D_MODEL), lambda i: (0, 0))],
        out_specs=spec,
        out_shape=jax.ShapeDtypeStruct(x.shape, F32),
        compiler_params=_cparams(("parallel",)),
        name="final_norm",
    )(x, g)


def _dn_params(conv_w, a_log, dt_bias, dn_norm):
    bc = lambda p: jnp.broadcast_to(p.astype(F32).T[:, :, None, None], (DN_HEADS, 2, 1, LANES))
    return {"conv": conv_w.astype(F32), "alog": bc(a_log), "dtb": bc(dt_bias),
            "gn": dn_norm.astype(F32).reshape(1, LANES)}


def _dn_call(z, qkv, col, alog, dtb, gn, s0, seq_len, n_seq, row0, with_state_out):
    blk0 = row0 // seq_len
    hd = DN_HEADS
    nhb = hd // HPS
    wide = HPS * LANES
    gate0 = col["gate"] // HPS
    one = pl.Buffered(1)
    in_specs = [pl.BlockSpec((seq_len, wide), lambda s, h: (s, h), pipeline_mode=one),
                pl.BlockSpec((seq_len, wide), lambda s, h: (s, nhb + h), pipeline_mode=one),
                pl.BlockSpec((seq_len, wide), lambda s, h: (s, 2 * nhb + h), pipeline_mode=one),
                pl.BlockSpec((seq_len, LANES), lambda s, h: (blk0 + s, col["ab"])),
                pl.BlockSpec((seq_len, wide), lambda s, h: (blk0 + s, gate0 + h), pipeline_mode=one),
                pl.BlockSpec((HPS, 2, 1, LANES), lambda s, h: (h, 0, 0, 0)),
                pl.BlockSpec((HPS, 2, 1, LANES), lambda s, h: (h, 0, 0, 0)),
                pl.BlockSpec((1, LANES), lambda s, h: (0, 0))]
    args = [qkv, qkv, qkv, z, z, alog, dtb, gn]
    st_spec = pl.BlockSpec((1, 2, HPS, LANES, LANES), lambda s, h: (s, 0, h, 0, 0))
    has_s0 = s0 is not None
    if has_s0:
        in_specs.append(st_spec)
        args.append(s0)
    out_specs = [pl.BlockSpec((seq_len, wide), lambda s, h: (s, h))]
    out_shape = [jax.ShapeDtypeStruct((n_seq * seq_len, hd * LANES), BF16)]
    if with_state_out:
        out_specs.append(st_spec)
        out_shape.append(jax.ShapeDtypeStruct((n_seq, 2, hd, LANES, LANES), F32))
    return pl.pallas_call(
        functools.partial(_dn_kernel, seq_len=seq_len, with_state_out=with_state_out, has_s0=has_s0),
        grid=(n_seq, nhb),
        in_specs=in_specs,
        out_specs=out_specs,
        out_shape=out_shape,
        scratch_shapes=[pltpu.VMEM((seq_len, wide), F32), pltpu.VMEM((seq_len, wide), F32)],
        compiler_params=_cparams(("parallel", "parallel")),
        name=f"mix_dn_{seq_len}",
    )(*args)


def _grid_col_major(t):
    n, d = t.shape
    rows = DEC_SEQ // GRID_W
    return t.reshape(DEC_BATCH, rows, GRID_W, d).transpose(0, 2, 1, 3).reshape(n, d)


def _grid_row_major(t):
    n, d = t.shape
    rows = DEC_SEQ // GRID_W
    return t.reshape(DEC_BATCH, GRID_W, rows, d).transpose(0, 2, 1, 3).reshape(n, d)


def _both_groups(fn, s0_sample):
    res_p = fn(None, SEQ, BATCH, 0, True)
    res_s = fn(s0_sample, DEC_SEQ, DEC_BATCH, T_PROMPT, False)
    return jnp.concatenate([res_p[0], res_s[0]], axis=0), res_p[1]


def kernel(x_prompt, x_sample, state_gla, state_hgrn, state_s5_re, state_s5_im, state_dn, c, c_ctx, w_ada, b_ada, norm1, norm2, w_in_even, w_gla_gate, b_gla_gate, gla_norm, hg_lb, hg_norm, w_in_odd, s5_lam_re, s5_lam_im, s5_log_dt, s5_b_re, s5_b_im, s5_c_re, s5_c_im, s5_d, s5_w_glu, s5_b_glu, dn_conv, dn_a_log, dn_dt_bias, dn_norm, w_out, pk_w_q, pk_keys, pk_u, pk_v, final_norm):
    x = jnp.concatenate([x_prompt.reshape(T_PROMPT, D_MODEL), x_sample.reshape(T_SAMPLE, D_MODEL)], axis=0)
    cond = jnp.zeros((N_COND, D_MODEL), F32).at[0].set(c_ctx.astype(F32)).at[1:1 + DEC_BATCH].set(c.astype(F32))
    mod = _ada_call(cond, w_ada.astype(F32), b_ada.astype(F32))

    out_g, out_h, out_re, out_im, out_dn = [], [], [], [], []
    for l in range(DEPTH):
        mod3 = mod[l].reshape(N_COND * 6, 1, D_MODEL)
        g1 = norm1[l].astype(F32).reshape(1, D_MODEL)
        if l % 2 == 0:
            e = l // 2
            w_in = _relayout_cols(w_in_even[e], EVEN_IDX).astype(BF16)
            z = _in_call(x, mod3, g1, w_in, 2)
            prm = _even_params(w_gla_gate[e], b_gla_gate[e], gla_norm[e], hg_lb, hg_norm[e], e)
            s0g = jnp.pad(state_gla[:, e].astype(F32), ((0, 0),) * 3 + ((0, HG_DK - GLA_DK), (0, 0)))
            m_g, st_g = _both_groups(
                lambda s0, L, n, r0, so: _even_call(z, "gla", EVEN_COL, prm["gla"], s0, L, n, r0, so), s0g)
            m_h, st_h = _both_groups(
                lambda s0, L, n, r0, so: _even_call(z, "hgrn", EVEN_COL, prm["hgrn"], s0, L, n, r0, so),
                state_hgrn[:, e].astype(F32))
            m = jnp.concatenate([m_g, m_h], axis=1)
            out_g.append(st_g[..., :GLA_DK, :])
            out_h.append(st_h)
        else:
            j = l // 2
            x_in = jnp.concatenate([x[:T_PROMPT], _grid_col_major(x[T_PROMPT:])], axis=0)
            w_in = _relayout_cols(w_in_odd[j], ODD_IDX).astype(BF16)
            z = _in_call(x_in, mod3, g1, w_in, 3)
            bb, cc, apow = _s5_params(s5_lam_re[j], s5_lam_im[j], s5_log_dt[j], s5_b_re[j], s5_b_im[j],
                                      s5_c_re[j], s5_c_im[j])
            dsk = s5_d[j].astype(F32).reshape(1, MIX_W)
            y_p, fr, fi = _s5_call(z, ODD_COL["u"], bb, cc, apow, dsk, None, None, SEQ, BATCH, 0)
            y_s, _, _ = _s5_call(z, ODD_COL["u"], bb, cc, apow, dsk,
                                 state_s5_re[:, j].astype(F32).reshape(DEC_BATCH, 2, -1),
                                 state_s5_im[:, j].astype(F32).reshape(DEC_BATCH, 2, -1),
                                 DEC_SEQ, DEC_BATCH, T_PROMPT)
            m_s5 = _glu_call(jnp.concatenate([y_p, y_s], axis=0), s5_w_glu[j].astype(BF16),
                             s5_b_glu[j].astype(F32).reshape(1, MIX_W))
            dp = _dn_params(dn_conv[j], dn_a_log[j], dn_dt_bias[j], dn_norm[j])

            def dn(s0, L, n, r0, so):
                qkv = _conv_call(z, ODD_COL["qkv"], dp["conv"], L, n, r0)
                return _dn_call(z, qkv, ODD_COL, dp["alog"], dp["dtb"], dp["gn"], s0, L, n, r0, so)

            m_dn, st_dn = _both_groups(dn, state_dn[:, j].astype(F32))
            m = jnp.concatenate([m_s5, m_dn], axis=1)
            m = jnp.concatenate([m[:T_PROMPT], _grid_row_major(m[T_PROMPT:])], axis=0)
            out_re.append(fr.reshape(BATCH, 2, S5_G, S5_P))
            out_im.append(fi.reshape(BATCH, 2, S5_G, S5_P))
            out_dn.append(st_dn)
        x, h2 = _out_call(x, m, w_out[l].astype(BF16), mod3, norm2[l].astype(F32).reshape(1, D_MODEL))
        s = _score_call(h2, pk_w_q[l].astype(BF16), pk_keys[l].reshape(PK_HP, PK_KEYS, PK_DQ // 2).astype(BF16))
        i1, i2, gate = _topk_call(s)
        w = _wbuild_call(i1, i2, gate)
        x = _peer_call(h2.astype(BF16), pk_u[l], pk_v[l], w, x, mod3)
    y = _final_norm_call(x, final_norm.astype(F32).reshape(1, D_MODEL))
    return (y[:T_PROMPT].reshape(BATCH, SEQ, D_MODEL), y[T_PROMPT:].reshape(DEC_BATCH, DEC_SEQ, D_MODEL),
            jnp.stack(out_g, axis=1), jnp.stack(out_h, axis=1), jnp.stack(out_re, axis=1),
            jnp.stack(out_im, axis=1), jnp.stack(out_dn, axis=1))
```

```python
import functools

import numpy as np
import jax
import jax.numpy as jnp
from jax import lax
from jax.experimental import pallas as pl
from jax.experimental.pallas import tpu as pltpu

F32 = jnp.float32
BF16 = jnp.bfloat16

D_MODEL = 1024
BATCH = 32
SEQ = 256
DEPTH = 4
DEC_BATCH = 8
DEC_SEQ = 2048
GRID_W = 64
N_EVEN = 2
N_ODD = 2
MIX_W = 512
GLA_HEADS = 4
GLA_DK = 64
GLA_DV = 128
GLA_RANK = 16
GLA_TAU = 16.0
HG_HEADS = 4
HG_DK = 128
HG_DV = 128
S5_GS = 16
S5_G = 32
S5_P = 64
DN_HEADS = 4
DN_DK = 128
DN_DV = 128
DN_CONV = 5
PK_HEADS = 8
PK_KEYS = 128
PK_DQ = 256
PK_TOPK = 16
NORM_EPS = 1e-6
GATE_FLOOR = 1e-30

T_PROMPT = BATCH * SEQ
T_SAMPLE = DEC_BATCH * DEC_SEQ
T_ALL = T_PROMPT + T_SAMPLE
N_COND = 16

LANES = 128
SUBLANES = 8
VMEM_LIMIT = 56 * 1024 * 1024

TM = 512


def _layout(pieces):
    idx, col = [], {}
    for name, src, width in pieces:
        assert len(idx) % LANES == 0 and width % LANES == 0
        col[name] = len(idx) // LANES
        src = list(src)
        idx += src + [-1] * (width - len(src))
    return np.asarray(idx, np.int32), col


def _even_layout():
    o = np.cumsum((0, 256, 256, 512, 16, 16, 512, 512, 512, 512, 512, 512))
    rng = lambda i: range(o[i], o[i + 1])
    pieces = []
    for nm, base in (("gq", o[0]), ("gk", o[1])):
        for h in range(GLA_HEADS):
            pieces.append((nm if h == 0 else f"{nm}{h}", range(base + 64 * h, base + 64 * h + 64), LANES))
    pieces += [("gv", rng(2), 512), ("gr", rng(5), 512),
               ("hq", rng(6), 512), ("hf_f", rng(7), 512), ("hf_b", rng(8), 512),
               ("hi", rng(9), 512), ("hgate", rng(10), 512),
               ("lr", list(rng(3)) + list(rng(4)), LANES), ("pad", [], LANES)]
    return _layout(pieces)


def _odd_layout():
    o = np.cumsum((0, 512, 1536, 4, 4, 4, 4, 512))
    rng = lambda i: range(o[i], o[i + 1])
    return _layout([("u", rng(0), 512), ("qkv", rng(1), 1536), ("gate", rng(6), 512),
                    ("ab", range(o[2], o[6]), LANES)])


EVEN_IDX, EVEN_COL = _even_layout()
ODD_IDX, ODD_COL = _odd_layout()


def _relayout_cols(w, idx):
    runs = []
    for v in (int(v) for v in idx):
        grows = runs and ((v < 0 and runs[-1][0] < 0) or (v >= 0 and runs[-1][0] >= 0 and runs[-1][0] + runs[-1][1] == v))
        if grows:
            runs[-1][1] += 1
        else:
            runs.append([max(v, -1), 1])
    pieces = [jnp.zeros(w.shape[:-1] + (n,), w.dtype) if c < 0 else w[..., c:c + n] for c, n in runs]
    return jnp.concatenate(pieces, axis=-1)


def _cparams(sem):
    return pltpu.CompilerParams(dimension_semantics=sem, vmem_limit_bytes=VMEM_LIMIT)


def _cond_row(i, tm=TM):
    n_p = T_PROMPT // tm
    per_seq = DEC_SEQ // tm
    return jnp.where(i < n_p, 0, 1 + (i - n_p) // per_seq)


def _split3(x):
    hi = x.astype(BF16)
    r1 = x - hi.astype(F32)
    mid = r1.astype(BF16)
    lo = (r1 - mid.astype(F32)).astype(BF16)
    return hi, mid, lo


def _dot(a, b):
    return jnp.dot(a, b, preferred_element_type=F32)


def _dot_nt(a, b):
    return lax.dot_general(a, b, (((1,), (1,)), ((), ())), preferred_element_type=F32)


def _dot_tn(a, b):
    return lax.dot_general(a, b, (((0,), (0,)), ((), ())), preferred_element_type=F32)


def _dot_sel(sel_bf16, x):
    hi, mid, lo = _split3(x)
    return _dot(sel_bf16, hi) + _dot(sel_bf16, mid) + _dot(sel_bf16, lo)


def _dot_x_sel(x, sel_bf16):
    hi, mid, lo = _split3(x)
    return _dot(hi, sel_bf16) + _dot(mid, sel_bf16) + _dot(lo, sel_bf16)


def _dot_hi(a, b):
    ah = a.astype(BF16)
    al = (a - ah.astype(F32)).astype(BF16)
    bh = b.astype(BF16)
    bl = (b - bh.astype(F32)).astype(BF16)
    return _dot(ah, bh) + (_dot(ah, bl) + _dot(al, bh))


def _ada_kernel(c_ref, w_ref, b_ref, o_ref):
    c = c_ref[...]
    cs = c * jax.nn.sigmoid(c)
    o_ref[0] = _dot(cs, w_ref[0]) + b_ref[0]


def _ada_call(cond, w_ada, b_ada):
    nt = 6
    return pl.pallas_call(
        _ada_kernel,
        grid=(DEPTH, nt),
        in_specs=[
            pl.BlockSpec((N_COND, D_MODEL), lambda l, j: (0, 0)),
            pl.BlockSpec((1, D_MODEL, D_MODEL), lambda l, j: (l, 0, j)),
            pl.BlockSpec((1, 1, D_MODEL), lambda l, j: (l, 0, j)),
        ],
        out_specs=pl.BlockSpec((1, N_COND, D_MODEL), lambda l, j: (l, 0, j)),
        out_shape=jax.ShapeDtypeStruct((DEPTH, N_COND, 6 * D_MODEL), F32),
        compiler_params=_cparams(("parallel", "parallel")),
        name="ada_mod",
    )(cond, w_ada, b_ada.reshape(DEPTH, 1, 6 * D_MODEL))


def _modulated_norm(x, g, sc, sh):
    ms = jnp.mean(x * x, axis=-1, keepdims=True)
    y = x * lax.rsqrt(ms + NORM_EPS) * g
    return y * (1.0 + sc) + sh


def _in_kernel(x_ref, sh_ref, sc_ref, g_ref, w_ref, o_ref):
    h = _modulated_norm(x_ref[...], g_ref[...], sc_ref[0], sh_ref[0])
    o_ref[...] = _dot(h.astype(BF16), w_ref[...])


def _in_call(x, mod3, g, w, n_col_tiles):
    n = w.shape[1]
    tn = n // n_col_tiles
    return pl.pallas_call(
        _in_kernel,
        grid=(n_col_tiles, x.shape[0] // TM),
        in_specs=[
            pl.BlockSpec((TM, D_MODEL), lambda j, i: (i, 0)),
            pl.BlockSpec((1, 1, D_MODEL), lambda j, i: (_cond_row(i) * 6 + 0, 0, 0)),
            pl.BlockSpec((1, 1, D_MODEL), lambda j, i: (_cond_row(i) * 6 + 1, 0, 0)),
            pl.BlockSpec((1, D_MODEL), lambda j, i: (0, 0)),
            pl.BlockSpec((D_MODEL, tn), lambda j, i: (0, j)),
        ],
        out_specs=pl.BlockSpec((TM, tn), lambda j, i: (i, j)),
        out_shape=jax.ShapeDtypeStruct((x.shape[0], n), F32),
        compiler_params=_cparams(("parallel", "parallel")),
        name="in_proj",
    )(x, mod3, mod3, g, w)


def _out_kernel(x_ref, m_ref, w_ref, g1_ref, sh_ref, sc_ref, g_ref, xo_ref, h_ref):
    x = x_ref[...] + g1_ref[0] * _dot(m_ref[...], w_ref[...])
    xo_ref[...] = x
    h_ref[...] = _modulated_norm(x, g_ref[...], sc_ref[0], sh_ref[0]).astype(h_ref.dtype)


def _out_call(x, m, w_out, mod3, g2):
    spec_tok = pl.BlockSpec((TM, D_MODEL), lambda i: (i, 0))

    def mod_spec(k):
        return pl.BlockSpec((1, 1, D_MODEL), lambda i: (_cond_row(i) * 6 + k, 0, 0))

    return pl.pallas_call(
        _out_kernel,
        grid=(x.shape[0] // TM,),
        in_specs=[
            spec_tok,
            spec_tok,
            pl.BlockSpec((D_MODEL, D_MODEL), lambda i: (0, 0)),
            mod_spec(2), mod_spec(3), mod_spec(4),
            pl.BlockSpec((1, D_MODEL), lambda i: (0, 0)),
        ],
        out_specs=[spec_tok, spec_tok],
        out_shape=[jax.ShapeDtypeStruct(x.shape, F32), jax.ShapeDtypeStruct(x.shape, BF16)],
        compiler_params=_cparams(("parallel",)),
        name="out_proj",
    )(x, m, w_out, mod3, mod3, mod3, g2)


CHUNK = 64
SUB = 16
NSUB = CHUNK // SUB
NEG_BIG = -1e30
HPS = 4


def _tri(n, lower, strict=False):
    r = lax.broadcasted_iota(jnp.int32, (n, n), 0)
    c = lax.broadcasted_iota(jnp.int32, (n, n), 1)
    if lower:
        return (c < r) if strict else (c <= r)
    return (c > r) if strict else (c >= r)


def _lockstep(gens):
    results = [None] * len(gens)
    active = list(enumerate(gens))
    while active:
        still = []
        for i, g in active:
            try:
                next(g)
                still.append((i, g))
            except StopIteration as stop:
                results[i] = stop.value
        active = still
    return results


def _gla_chunk(q, k, v, la, st, reverse):
    kdim = q.shape[1]
    tri = jnp.where(_tri(CHUNK, not reverse), 1.0, 0.0).astype(BF16)
    b = _dot_sel(tri, la)
    yield
    last = 0 if reverse else CHUNK - 1
    b_last = b[last:last + 1, :]
    o_inter = _dot_nt((q * jnp.exp(b)).astype(BF16), st.astype(BF16))
    kd = k * jnp.exp(b_last - b)
    st_new = st * jnp.exp(b_last) + _dot_tn(v.astype(BF16), kd.astype(BF16))
    yield

    ones_k = jnp.ones((kdim, LANES), BF16)
    jrow = lax.broadcasted_iota(jnp.int32, (SUB, kdim), 0)
    rsel = lax.broadcasted_iota(jnp.int32, (SUB, SUB * SUB), 0)
    csel = lax.broadcasted_iota(jnp.int32, (SUB, SUB * SUB), 1)
    in_grp = jnp.logical_and(csel >= rsel * SUB, csel < rsel * SUB + SUB)
    sel = jnp.where(in_grp, 1.0, 0.0).astype(BF16)
    outs = []
    for blk in range(NSUB):
        r0 = blk * SUB
        q_i, k_i, b_i, v_i = q[r0:r0 + SUB], k[r0:r0 + SUB], b[r0:r0 + SUB], v[r0:r0 + SUB]
        rows = []
        for i in range(SUB):
            keep = (jrow >= i) if reverse else (jrow <= i)
            e = jnp.exp(jnp.where(keep, b_i[i:i + 1, :] - b_i, NEG_BIG))
            rows.append((q_i[i:i + 1, :] * k_i) * e)
        p = jnp.concatenate(rows, axis=0)
        att = _dot(p.astype(BF16), ones_k)
        yield
        zv = att * jnp.concatenate([v_i] * SUB, axis=0)
        o_blk = _dot(sel, zv.astype(BF16))
        yield
        if reverse and blk < NSUB - 1:
            r1 = r0 + SUB
            ref = b[r1:r1 + 1, :]
            qt = q_i * jnp.exp(b_i - ref)
            kt = k[r1:] * jnp.exp(ref - b[r1:])
            a_off = _dot_nt(qt.astype(BF16), kt.astype(BF16))
            yield
            o_blk = o_blk + _dot(a_off.astype(BF16), v[r1:].astype(BF16))
        if (not reverse) and blk > 0:
            ref = b[r0 - 1:r0, :]
            qt = q_i * jnp.exp(b_i - ref)
            kt = k[:r0] * jnp.exp(ref - b[:r0])
            a_off = _dot_nt(qt.astype(BF16), kt.astype(BF16))
            yield
            o_blk = o_blk + _dot(a_off.astype(BF16), v[:r0].astype(BF16))
        outs.append(o_blk)
    return o_inter + jnp.concatenate(outs, axis=0), st_new


def _log_sigmoid(x):
    return -(jnp.maximum(-x, 0.0) + jnp.log1p(jnp.exp(-jnp.abs(x))))


def _head_norm_gate(o, g, gate):
    o = o * lax.rsqrt(jnp.mean(o * o, axis=-1, keepdims=True) + NORM_EPS)
    return (o * g) * (gate * jax.nn.sigmoid(gate))


def _even_kernel(*refs, mode, seq_len, with_state_out, has_s0):
    it = iter(refs)
    if mode == "gla":
        q_ref, k_ref, v_ref, lr_ref, gate_ref, wg_ref, bg_ref, gn_ref = (next(it) for _ in range(8))
    else:
        q_ref, ff_ref, fb_ref, v_ref, gate_ref, lb_ref, gn_ref = (next(it) for _ in range(7))
    s0_ref = next(it) if has_s0 else None
    m_ref = next(it)
    so_ref = next(it) if with_state_out else None
    of_scr, ob_scr = next(it), next(it)
    n_chunks = seq_len // CHUNK
    streams = [(hh, d) for hh in range(HPS) for d in range(2)]

    def chain(c, hh, d, st):
        rows = pl.ds(pl.multiple_of(c * CHUNK, CHUNK), CHUNK)
        cols = slice(hh * LANES, (hh + 1) * LANES)
        if mode == "gla":
            q = q_ref[rows, cols] * (GLA_DK ** -0.5)
            k = k_ref[rows, cols]
            pre = _dot_hi(lr_ref[rows, :], wg_ref[hh, d]) + bg_ref[hh, d]
            yield
            la = _log_sigmoid(pre) / GLA_TAU
        else:
            q = q_ref[rows, cols]
            f = (ff_ref, fb_ref)[d][rows, cols]
            lb = lb_ref[d:d + 1, cols]
            gate = lb + (1.0 - lb) * jax.nn.sigmoid(f)
            la = jnp.log(jnp.maximum(gate, GATE_FLOOR))
            k = (1.0 - lb) * jax.nn.sigmoid(-f)
        o, st_new = yield from _gla_chunk(q, k, v_ref[rows, cols], la, st, d == 1)
        return rows, cols, o, st_new

    def body(c, carry):
        res = _lockstep([chain(n_chunks - 1 - c if d else c, hh, d, carry[i])
                         for i, (hh, d) in enumerate(streams)])
        for (hh, d), (rows, cols, o, _) in zip(streams, res):
            (ob_scr if d else of_scr)[rows, cols] = o
        return tuple(r[3] for r in res)

    if has_s0:
        init = tuple(s0_ref[0, d, hh].T for hh, d in streams)
    else:
        init = tuple(jnp.zeros((LANES, LANES), F32) for _ in streams)
    fin = lax.fori_loop(0, n_chunks, body, init)
    for hh in range(HPS):
        cols = slice(hh * LANES, (hh + 1) * LANES)
        m_ref[:, cols] = _head_norm_gate(of_scr[:, cols] + ob_scr[:, cols], gn_ref[...],
                                         gate_ref[:, cols]).astype(m_ref.dtype)
    if with_state_out:
        for i, (hh, d) in enumerate(streams):
            so_ref[0, d, hh] = fin[i].T


def _even_params(w_gate, b_gate, gla_norm, hg_lb, hg_norm, e):
    wg = jnp.zeros((GLA_HEADS, 2, LANES, LANES), F32)
    bg = jnp.zeros((GLA_HEADS, 2, 1, LANES), F32)
    for h in range(GLA_HEADS):
        for d in range(2):
            wg = wg.at[h, d, d * GLA_RANK:(d + 1) * GLA_RANK, :GLA_DK].set(
                w_gate[d, :, h * GLA_DK:(h + 1) * GLA_DK].astype(F32))
            bg = bg.at[h, d, 0, :GLA_DK].set(b_gate[d, h * GLA_DK:(h + 1) * GLA_DK].astype(F32))
    lbp = jax.nn.softmax(hg_lb.astype(F32), axis=1)
    lb = (jnp.cumsum(lbp, axis=1) - lbp[:, :1])[:, e]
    return {"gla": (wg, bg, gla_norm.astype(F32).reshape(1, LANES)),
            "hgrn": (lb, hg_norm.astype(F32).reshape(1, LANES))}


def _even_call(z, mode, col, params, s0, seq_len, n_seq, row0, with_state_out):
    heads = 4
    blk0 = row0 // seq_len
    wide = HPS * LANES

    def tok(name):
        c0 = col[name] // HPS
        return pl.BlockSpec((seq_len, wide), lambda s, h: (blk0 + s, c0 + h), pipeline_mode=pl.Buffered(1))

    if mode == "gla":
        wg, bg, gn = params
        in_specs = [tok("gq"), tok("gk"), tok("gv"),
                    pl.BlockSpec((seq_len, LANES), lambda s, h: (blk0 + s, col["lr"])),
                    tok("gr"),
                    pl.BlockSpec((HPS, 2, LANES, LANES), lambda s, h: (h, 0, 0, 0)),
                    pl.BlockSpec((HPS, 2, 1, LANES), lambda s, h: (h, 0, 0, 0)),
                    pl.BlockSpec((1, LANES), lambda s, h: (0, 0))]
        args = [z, z, z, z, z, wg, bg, gn]
    else:
        lb, gn = params
        in_specs = [tok("hq"), tok("hf_f"), tok("hf_b"), tok("hi"), tok("hgate"),
                    pl.BlockSpec((2, wide), lambda s, h: (0, h)),
                    pl.BlockSpec((1, LANES), lambda s, h: (0, 0))]
        args = [z, z, z, z, z, lb, gn]
    st_spec = pl.BlockSpec((1, 2, HPS, LANES, LANES), lambda s, h: (s, 0, h, 0, 0))
    has_s0 = s0 is not None
    if has_s0:
        in_specs.append(st_spec)
        args.append(s0)
    out_specs = [pl.BlockSpec((seq_len, wide), lambda s, h: (s, h))]
    out_shape = [jax.ShapeDtypeStruct((n_seq * seq_len, heads * LANES), BF16)]
    if with_state_out:
        out_specs.append(st_spec)
        out_shape.append(jax.ShapeDtypeStruct((n_seq, 2, heads, LANES, LANES), F32))
    res = pl.pallas_call(
        functools.partial(_even_kernel, mode=mode, seq_len=seq_len,
                          with_state_out=with_state_out, has_s0=has_s0),
        grid=(n_seq, heads // HPS),
        in_specs=in_specs,
        out_specs=out_specs,
        out_shape=out_shape,
        scratch_shapes=[pltpu.VMEM((seq_len, wide), F32), pltpu.VMEM((seq_len, wide), F32)],
        compiler_params=_cparams(("parallel", "parallel")),
        name=f"mix_{mode}_{seq_len}",
    )(*args)
    return res


CONV_PAD = SUBLANES


def _conv_kernel(x_ref, w_ref, o_ref, pad_scr, *, seq_len):
    cb = pl.program_id(1)
    wide = DN_HEADS * LANES
    zeros = jnp.zeros((CONV_PAD, wide), F32)
    pad_scr[0:CONV_PAD, :] = zeros
    pad_scr[CONV_PAD + seq_len:2 * CONV_PAD + seq_len, :] = zeros
    pad_scr[CONV_PAD:CONV_PAD + seq_len, :] = x_ref[...]
    is_q = cb == 0
    is_qk = cb < 2
    scale = jnp.where(is_q, DN_DK ** -0.5, 1.0)
    for hh in range(DN_HEADS):
        cols = slice(hh * LANES, (hh + 1) * LANES)
        acc = jnp.zeros((seq_len, LANES), F32)
        for w in range(DN_CONV):
            acc = acc + pad_scr[pl.ds(CONV_PAD + w - DN_CONV // 2, seq_len), cols] * w_ref[w:w + 1, cols]
        y = acc * jax.nn.sigmoid(acc)
        yn = y * lax.rsqrt(jnp.sum(y * y, axis=-1, keepdims=True) + NORM_EPS)
        o_ref[:, cols] = jnp.where(is_qk, yn * scale, y)


def _conv_call(z, col0, w, seq_len, n_seq, row0):
    blk0 = row0 // seq_len
    wide = DN_HEADS * LANES
    c0 = col0 // DN_HEADS
    return pl.pallas_call(
        functools.partial(_conv_kernel, seq_len=seq_len),
        grid=(n_seq, 3),
        in_specs=[pl.BlockSpec((seq_len, wide), lambda s, c: (blk0 + s, c0 + c)),
                  pl.BlockSpec((DN_CONV, wide), lambda s, c: (0, c))],
        out_specs=pl.BlockSpec((seq_len, wide), lambda s, c: (s, c)),
        out_shape=jax.ShapeDtypeStruct((n_seq * seq_len, 3 * wide), F32),
        scratch_shapes=[pltpu.VMEM((seq_len + 2 * CONV_PAD, wide), F32)],
        compiler_params=_cparams(("parallel", "parallel")),
        name=f"dn_conv_{seq_len}",
    )(z, w)


def _softplus(x):
    return jnp.maximum(x, 0.0) + jnp.log1p(jnp.exp(-jnp.abs(x)))


def _dn_chunk(q, k, v, g, beta, s, reverse):
    c = CHUNK
    lower = not reverse
    tri = jnp.where(_tri(c, lower), 1.0, 0.0).astype(BF16)
    gam = _dot_sel(tri, g)
    yield
    ones_c = jnp.ones((c, c), BF16)
    gam_row = _dot_sel(ones_c, jnp.where(_tri(c, reverse), g[:, :c], 0.0))
    incl = _tri(c, lower)
    strict = _tri(c, lower, strict=True)
    diff = gam[:, :c] - gam_row
    dec = jnp.where(incl, jnp.exp(jnp.where(incl, diff, 0.0)), 0.0)
    kb = k * beta
    k16 = k.astype(BF16)
    m = jnp.where(strict, _dot_nt(kb.astype(BF16), k16) * dec, 0.0)
    yield
    eg = jnp.exp(gam)
    x = jnp.concatenate([v * beta, kb * eg], axis=1)
    p = -m
    n_fac = int(np.log2(c))
    for j in range(n_fac):
        x = x + _dot_hi(p, x)
        if j + 1 < n_fac:
            p = _dot_hi(p, p)
        yield
    u, w = x[:, :LANES], x[:, LANES:]
    s16 = s.astype(BF16)
    v_new = u - _dot(w.astype(BF16), s16)
    att = _dot_nt(q.astype(BF16), k16) * dec
    yield
    o = _dot((q * eg).astype(BF16), s16) + _dot(att.astype(BF16), v_new.astype(BF16))
    last = 0 if reverse else c - 1
    gl = gam[last:last + 1, :]
    s_new = s * jnp.exp(gl) + _dot_tn((k * jnp.exp(gl - gam)).astype(BF16), v_new.astype(BF16))
    return o, s_new


def _dn_kernel(*refs, seq_len, with_state_out, has_s0):
    it = iter(refs)
    q_ref, k_ref, v_ref, ab_ref, gate_ref, alog_ref, dtb_ref, gn_ref = (next(it) for _ in range(8))
    s0_ref = next(it) if has_s0 else None
    m_ref = next(it)
    so_ref = next(it) if with_state_out else None
    of_scr, ob_scr = next(it), next(it)
    h0 = pl.program_id(1) * HPS
    n_chunks = seq_len // CHUNK
    srow = lax.broadcasted_iota(jnp.int32, (LANES, LANES), 0)
    streams = [(hh, d) for hh in range(HPS) for d in range(2)]

    def chain(c, hh, d, s):
        rows = pl.ds(pl.multiple_of(c * CHUNK, CHUNK), CHUNK)
        cols = slice(hh * LANES, (hh + 1) * LANES)
        ab = ab_ref[rows, :]
        sel_a = jnp.where(srow == DN_HEADS * d + h0 + hh, 1.0, 0.0).astype(BF16)
        sel_b = jnp.where(srow == 2 * DN_HEADS + DN_HEADS * d + h0 + hh, 1.0, 0.0).astype(BF16)
        a = _dot_x_sel(ab, sel_a)
        beta = jax.nn.sigmoid(_dot_x_sel(ab, sel_b))
        g = -jnp.exp(alog_ref[hh, d]) * _softplus(a + dtb_ref[hh, d])
        yield
        o, s_new = yield from _dn_chunk(q_ref[rows, cols], k_ref[rows, cols], v_ref[rows, cols],
                                        g, beta, s, d == 1)
        return rows, cols, o, s_new

    def body(c, carry):
        res = _lockstep([chain(n_chunks - 1 - c if d else c, hh, d, carry[i])
                         for i, (hh, d) in enumerate(streams)])
        for (hh, d), (rows, cols, o, _) in zip(streams, res):
            (ob_scr if d else of_scr)[rows, cols] = o
        return tuple(r[3] for r in res)

    if has_s0:
        init = tuple(s0_ref[0, d, hh] for hh, d in streams)
    else:
        init = tuple(jnp.zeros((LANES, LANES), F32) for _ in streams)
    fin = lax.fori_loop(0, n_chunks, body, init)
    for hh in range(HPS):
        cols = slice(hh * LANES, (hh + 1) * LANES)
        m_ref[:, cols] = _head_norm_gate(of_scr[:, cols] + ob_scr[:, cols], gn_ref[...],
                                         gate_ref[:, cols]).astype(m_ref.dtype)
    if with_state_out:
        for i, (hh, d) in enumerate(streams):
            so_ref[0, d, hh] = fin[i]


S5_TC = 128
S5_CB = 4
S5_SW = S5_G * S5_P // S5_CB


def _s5_params(lam_re, lam_im, log_dt, b_re, b_im, c_re, c_im):
    gpb = S5_G // S5_CB
    eye = jnp.eye(gpb, dtype=F32)
    n = jnp.arange(1, S5_TC + 1, dtype=F32)[:, None, None]
    bbs, ccs, aps = [], [], []
    for d in range(2):
        lr = jnp.minimum(lam_re[d].astype(F32), -1e-4)
        li = lam_im[d].astype(F32)
        dt = jnp.exp(log_dt[d].astype(F32))[:, None]
        mag = jnp.exp(lr * dt)
        ar, ai = mag * jnp.cos(li * dt), mag * jnp.sin(li * dt)
        den = lr * lr + li * li
        nr = ar - 1.0
        cr = (nr * lr + ai * li) / den
        ci = (ai * lr - nr * li) / den
        bbr = cr[..., None] * b_re - ci[..., None] * b_im
        bbi = cr[..., None] * b_im + ci[..., None] * b_re
        blk = lambda t: jnp.einsum('bgpc,gh->bgchp', t.reshape(S5_CB, gpb, S5_P, S5_GS), eye).reshape(
            S5_CB, gpb * S5_GS, gpb * S5_P)
        bbs.append(jnp.stack([blk(bbr), blk(bbi)]))
        cblk = lambda t: jnp.einsum('bgcp,gh->bgphc', t.reshape(S5_CB, gpb, S5_GS, S5_P), eye).reshape(
            S5_CB, gpb * S5_P, gpb * S5_GS)
        ccs.append(jnp.stack([cblk(c_re[d].astype(F32)), cblk(c_im[d].astype(F32))]))
        pm = jnp.exp(n * (lr * dt)[None])
        pr = (pm * jnp.cos(n * (li * dt)[None])).reshape(S5_TC, -1)
        pi = (pm * jnp.sin(n * (li * dt)[None])).reshape(S5_TC, -1)
        if d == 1:
            pr, pi = pr[::-1], pi[::-1]
        aps.append(jnp.stack([pr, pi]))
    bb = jnp.stack(bbs, axis=2).astype(BF16)
    cc = jnp.stack(ccs, axis=2).astype(BF16)
    apow = jnp.stack(aps, axis=1)
    return bb, cc, apow


def _s5_scan(xr, xi, pr, pi, cr, ci, reverse, loc, loc0):
    tc, w = xr.shape
    g8 = SUBLANES
    ng = tc // g8

    def power(n):
        idx = tc - n if reverse else n - 1
        return pr[idx:idx + 1, :], pi[idx:idx + 1, :]

    def step(xr, xi, s, unit):
        n = xr.shape[0]
        row = lax.broadcasted_iota(jnp.int32, xr.shape, 0)
        period = g8 if unit == 1 else n
        pos = jnp.bitwise_and(row, period - 1) if unit == 1 else row
        keep = (pos < period - s) if reverse else (pos >= s)
        shift = n - s if reverse else s
        a_r, a_i = power(s * unit)
        if unit == 1:
            rot = lambda x: pltpu.roll(x.reshape(n // g8, g8, w), g8 - s if reverse else s, 1).reshape(n, w)
        else:
            rot = lambda x: pltpu.roll(x, shift, 0)
        sr = jnp.where(keep, rot(xr), 0.0)
        si = jnp.where(keep, rot(xi), 0.0)
        return xr + (a_r * sr - a_i * si), xi + (a_r * si + a_i * sr)

    s = 1
    while s < g8:
        xr, xi = step(xr, xi, s, 1)
        s *= 2
    end = 0 if reverse else g8 - 1
    ers, eis = [], []
    for k in range(w // LANES):
        r0, i0 = (loc0 + k) * tc, (loc0 + w // LANES + k) * tc
        loc[r0:r0 + tc, :] = xr[:, k * LANES:(k + 1) * LANES]
        loc[i0:i0 + tc, :] = xi[:, k * LANES:(k + 1) * LANES]
        ers.append(loc[pl.ds(r0 + end, ng, stride=g8), :])
        eis.append(loc[pl.ds(i0 + end, ng, stride=g8), :])
    er = jnp.concatenate(ers, axis=1)
    ei = jnp.concatenate(eis, axis=1)
    s = 1
    while s < ng:
        er, ei = step(er, ei, s, g8)
        s *= 2
    grow = lax.broadcasted_iota(jnp.int32, (ng, w), 0)
    if reverse:
        pin_r = jnp.concatenate([pr[g8 * (g + 1):g8 * (g + 1) + 1] for g in range(ng - 1)] + [jnp.ones((1, w), F32)], 0)
        pin_i = jnp.concatenate([pi[g8 * (g + 1):g8 * (g + 1) + 1] for g in range(ng - 1)] + [jnp.zeros((1, w), F32)], 0)
        prev_r = jnp.where(grow < ng - 1, pltpu.roll(er, ng - 1, 0), 0.0)
        prev_i = jnp.where(grow < ng - 1, pltpu.roll(ei, ng - 1, 0), 0.0)
    else:
        pin_r = jnp.concatenate([jnp.ones((1, w), F32)] + [pr[g8 * g - 1:g8 * g] for g in range(1, ng)], 0)
        pin_i = jnp.concatenate([jnp.zeros((1, w), F32)] + [pi[g8 * g - 1:g8 * g] for g in range(1, ng)], 0)
        prev_r = jnp.where(grow >= 1, pltpu.roll(er, 1, 0), 0.0)
        prev_i = jnp.where(grow >= 1, pltpu.roll(ei, 1, 0), 0.0)
    in_r = prev_r + (pin_r * cr - pin_i * ci)
    in_i = prev_i + (pin_r * ci + pin_i * cr)
    in_r = jnp.concatenate([jnp.broadcast_to(in_r[g:g + 1], (g8, w)) for g in range(ng)], axis=0)
    in_i = jnp.concatenate([jnp.broadcast_to(in_i[g:g + 1], (g8, w)) for g in range(ng)], axis=0)
    p8r = pr[tc - g8:] if reverse else pr[:g8]
    p8i = pi[tc - g8:] if reverse else pi[:g8]
    a_r = jnp.concatenate([p8r] * ng, axis=0)
    a_i = jnp.concatenate([p8i] * ng, axis=0)
    return xr + (a_r * in_r - a_i * in_i), xi + (a_r * in_i + a_i * in_r)


def _s5_kernel(*refs, seq_len, has_h0):
    it = iter(refs)
    u_ref, bb_ref, cc_ref, ap_ref, dsk_ref = (next(it) for _ in range(5))
    h0r_ref, h0i_ref = (next(it), next(it)) if has_h0 else (None, None)
    y_ref, fr_ref, fi_ref = next(it), next(it), next(it)
    yf_scr, yb_scr, loc_scr = next(it), next(it), next(it)
    n_chunks = seq_len // S5_TC

    def run(c, d, carry):
        cr, ci = carry
        rows = pl.ds(pl.multiple_of(c * S5_TC, S5_TC), S5_TC)
        ub = u_ref[rows, :].astype(BF16)
        pr, pi = ap_ref[0, d], ap_ref[1, d]
        xr, xi = _dot(ub, bb_ref[0, 0, d]), _dot(ub, bb_ref[1, 0, d])
        hr, hi = _s5_scan(xr, xi, pr, pi, cr, ci, d == 1, loc_scr, d * 2 * (S5_SW // LANES))
        (yf_scr, yb_scr)[d][rows, :] = (_dot(hr.astype(BF16), cc_ref[0, 0, d])
                                         - _dot(hi.astype(BF16), cc_ref[1, 0, d]))
        last = 0 if d == 1 else S5_TC - 1
        return hr[last:last + 1, :], hi[last:last + 1, :]

    def body(c, carry):
        cf, cb = carry
        return run(c, 0, cf), run(n_chunks - 1 - c, 1, cb)

    if has_h0:
        init = tuple((h0r_ref[0, d:d + 1, :], h0i_ref[0, d:d + 1, :]) for d in range(2))
    else:
        z = jnp.zeros((1, S5_SW), F32)
        init = ((z, z), (z, z))
    fin = lax.fori_loop(0, n_chunks, body, init)
    y_ref[...] = dsk_ref[...] * u_ref[...] + (yf_scr[...] + yb_scr[...])
    for d in range(2):
        fr_ref[0, d:d + 1, :] = fin[d][0]
        fi_ref[0, d:d + 1, :] = fin[d][1]


def _s5_call(z, col0, bb, cc, apow, dsk, h0r, h0i, seq_len, n_seq, row0):
    blk0 = row0 // seq_len
    has_h0 = h0r is not None
    st_spec = pl.BlockSpec((1, 2, S5_SW), lambda s, c: (s, 0, c))
    in_specs = [pl.BlockSpec((seq_len, LANES), lambda s, c: (blk0 + s, col0 + c)),
                pl.BlockSpec((2, 1, 2, LANES, S5_SW), lambda s, c: (0, c, 0, 0, 0)),
                pl.BlockSpec((2, 1, 2, S5_SW, LANES), lambda s, c: (0, c, 0, 0, 0)),
                pl.BlockSpec((2, 2, S5_TC, S5_SW), lambda s, c: (0, 0, 0, c)),
                pl.BlockSpec((1, LANES), lambda s, c: (0, c))]
    args = [z, bb, cc, apow, dsk]
    if has_h0:
        in_specs += [st_spec, st_spec]
        args += [h0r, h0i]
    st_shape = jax.ShapeDtypeStruct((n_seq, 2, S5_G * S5_P), F32)
    return pl.pallas_call(
        functools.partial(_s5_kernel, seq_len=seq_len, has_h0=has_h0),
        grid=(n_seq, S5_CB),
        in_specs=in_specs,
        out_specs=[pl.BlockSpec((seq_len, LANES), lambda s, c: (s, c)), st_spec, st_spec],
        out_shape=[jax.ShapeDtypeStruct((n_seq * seq_len, MIX_W), F32), st_shape, st_shape],
        scratch_shapes=[pltpu.VMEM((seq_len, LANES), F32), pltpu.VMEM((seq_len, LANES), F32),
                        pltpu.VMEM((4 * (S5_SW // LANES) * S5_TC, LANES), F32)],
        compiler_params=_cparams(("parallel", "parallel")),
        name=f"mix_s5_{seq_len}",
    )(*args)


def _glu_kernel(y_ref, w_ref, b_ref, o_ref):
    zz = jax.nn.gelu(y_ref[...])
    o_ref[...] = (zz * jax.nn.sigmoid(_dot(zz.astype(BF16), w_ref[...]) + b_ref[...])).astype(o_ref.dtype)


def _glu_call(y, w, b):
    t = y.shape[0]
    return pl.pallas_call(
        _glu_kernel,
        grid=(t // TM,),
        in_specs=[pl.BlockSpec((TM, MIX_W), lambda i: (i, 0)),
                  pl.BlockSpec((MIX_W, MIX_W), lambda i: (0, 0)),
                  pl.BlockSpec((1, MIX_W), lambda i: (0, 0))],
        out_specs=pl.BlockSpec((TM, MIX_W), lambda i: (i, 0)),
        out_shape=jax.ShapeDtypeStruct((t, MIX_W), BF16),
        compiler_params=_cparams(("parallel",)),
        name="s5_glu",
    )(y, w, b)


PK_HP = 2 * PK_HEADS
PK_SIDE = PK_KEYS
PK_PICKS = PK_HEADS * PK_TOPK
TK_TM = 128
TK_HG = 2
WB_TM = 128
WB_GRP = 16
WB_STRIDE = PK_SIDE + SUBLANES
PM_TM = 1024
PM_SLABS = 8


def _score_kernel(h_ref, wq_ref, keys_ref, s_ref):
    q = _dot(h_ref[...], wq_ref[...])
    for hp in range(PK_HP):
        s_ref[hp] = _dot_nt(keys_ref[hp], q[:, hp * LANES:(hp + 1) * LANES].astype(BF16))


def _score_call(h, wq, keys):
    return pl.pallas_call(
        _score_kernel,
        grid=(h.shape[0] // TM,),
        in_specs=[pl.BlockSpec((TM, D_MODEL), lambda i: (i, 0)),
                  pl.BlockSpec((D_MODEL, PK_HP * LANES), lambda i: (0, 0)),
                  pl.BlockSpec((PK_HP, PK_KEYS, LANES), lambda i: (0, 0, 0))],
        out_specs=pl.BlockSpec((PK_HP, PK_KEYS, TM), lambda i: (0, 0, i)),
        out_shape=jax.ShapeDtypeStruct((PK_HP, PK_KEYS, h.shape[0]), F32),
        compiler_params=_cparams(("parallel",)),
        name="peer_scores",
    )(h, wq, keys)


def _top16_sorted(s):
    n_grp = s.shape[0] // SUBLANES
    n = s.shape[1]
    base = lax.broadcasted_iota(jnp.int32, (SUBLANES, n), 0).astype(F32)
    vals = [s[SUBLANES * g:SUBLANES * (g + 1), :] for g in range(n_grp)]
    keys = [base + float(SUBLANES * g) for g in range(n_grp)]
    for rnd in range(n_grp):
        for k in range(rnd % 2, n_grp - 1, 2):
            a, b, ka, kb = vals[k], vals[k + 1], keys[k], keys[k + 1]
            swap = b > a
            vals[k], vals[k + 1] = jnp.maximum(a, b), jnp.minimum(a, b)
            keys[k], keys[k + 1] = jnp.where(swap, kb, ka), jnp.where(swap, ka, kb)
    rank = lax.broadcasted_iota(jnp.int32, (PK_TOPK, n), 0)
    top_v = jnp.zeros((PK_TOPK, n), F32)
    top_k = jnp.zeros((PK_TOPK, n), F32)
    for r in range(PK_TOPK):
        m = jnp.max(vals[0], axis=0, keepdims=True)
        am = jnp.min(jnp.where(vals[0] == m, keys[0], float(s.shape[0])), axis=0, keepdims=True)
        top_v = jnp.where(rank == r, m, top_v)
        top_k = jnp.where(rank == r, am, top_k)
        hit = keys[0] == am
        for k in range(min(n_grp - 1, PK_TOPK - 1 - r)):
            vals[k] = jnp.where(hit, vals[k + 1], vals[k])
            keys[k] = jnp.where(hit, keys[k + 1], keys[k])
    return top_v, top_k


def _top16_rows(tiles):
    shape = tiles[0].shape
    row = lax.broadcasted_iota(jnp.int32, shape, 0).astype(F32)
    out_row = lax.broadcasted_iota(jnp.int32, (PK_TOPK, shape[1]), 0)

    def take(r, s, vals, idxs):
        m = jnp.max(s, axis=0, keepdims=True)
        am = jnp.min(jnp.where(s == m, row, float(shape[0])), axis=0, keepdims=True)
        at = out_row == r
        return jnp.where(row == am, -jnp.inf, s), jnp.where(at, m, vals), jnp.where(at, am, idxs)

    def body(r, carry):
        out = ()
        for i in range(len(tiles)):
            out += take(r, *carry[3 * i:3 * i + 3])
        return out

    zero = jnp.zeros((PK_TOPK, shape[1]), F32)
    init = ()
    for s in tiles:
        init += (s, zero, zero)
    res = lax.fori_loop(0, PK_TOPK, body, init)
    return [(res[3 * i + 1], res[3 * i + 2]) for i in range(len(tiles))]


def _top16_pairs(v1, a1, v2, a2):
    half = PK_TOPK // 2
    n = v1.shape[1]
    rank = lax.broadcasted_iota(jnp.int32, (PK_TOPK, n), 0)
    rank_f = rank.astype(F32)
    vals = [v1 + jnp.broadcast_to(v2[0:1], (PK_TOPK, n))]
    vals += [v1[:half] + jnp.broadcast_to(v2[b:b + 1], (half, n)) for b in range(1, PK_TOPK)]
    ids = [rank_f * PK_TOPK] + [rank_f[:half] * PK_TOPK + float(b) for b in range(1, PK_TOPK)]
    best = jnp.zeros((PK_TOPK, n), F32)
    p1 = jnp.zeros((PK_TOPK, n), F32)
    p2 = jnp.zeros((PK_TOPK, n), F32)
    for r in range(PK_TOPK):
        m = jnp.max(vals[0], axis=0, keepdims=True)
        pos = jnp.min(jnp.where(vals[0] == m, ids[0], float(PK_TOPK * PK_TOPK)), axis=0, keepdims=True)
        hit = ids[0] == pos
        a_sel = jnp.max(jnp.where(hit, rank_f, -1.0), axis=0, keepdims=True)
        e1 = jnp.max(jnp.where(hit, a1, -1.0), axis=0, keepdims=True)
        e2 = jnp.max(jnp.where(rank_f == pos - a_sel * PK_TOPK, a2, -1.0), axis=0, keepdims=True)
        best = jnp.where(rank == r, m, best)
        p1 = jnp.where(rank == r, e1, p1)
        p2 = jnp.where(rank == r, e2, p2)
        if r + 1 < PK_TOPK:
            hit_lo = hit[:half]
            vals[0] = jnp.concatenate([jnp.where(hit_lo, vals[1], vals[0][:half]),
                                       jnp.where(hit[half:], -jnp.inf, vals[0][half:])], axis=0)
            ids[0] = jnp.concatenate([jnp.where(hit_lo, ids[1], ids[0][:half]), ids[0][half:]], axis=0)
            for k in range(1, PK_TOPK - 1 - r):
                vals[k] = jnp.where(hit_lo, vals[k + 1], vals[k])
                ids[k] = jnp.where(hit_lo, ids[k + 1], ids[k])
    return best, p1, p2


def _topk_kernel(s_ref, i1_ref, i2_ref, g_ref, p1_scr, p2_scr, g_scr):
    def heads(hg, carry):
        tops = [_top16_sorted(s_ref[2 * TK_HG * hg + i]) for i in range(2 * TK_HG)]
        for j in range(TK_HG):
            (v1, a1), (v2, a2) = tops[2 * j], tops[2 * j + 1]
            best, p1, p2 = _top16_pairs(v1, a1, v2, a2)
            ex = jnp.exp(best - jnp.max(best, axis=0, keepdims=True))
            rows = pl.ds(pl.multiple_of((hg * TK_HG + j) * PK_TOPK, PK_TOPK), PK_TOPK)
            p1_scr[rows, :] = p1
            p2_scr[rows, :] = p2
            g_scr[rows, :] = ex / jnp.sum(ex, axis=0, keepdims=True)
        return carry

    lax.fori_loop(0, PK_HEADS // TK_HG, heads, 0)
    i1_ref[...] = p1_scr[...].T
    i2_ref[...] = p2_scr[...].T
    g_ref[...] = g_scr[...].T


def _topk_call(s):
    t = s.shape[2]
    spec = pl.BlockSpec((TK_TM, PK_PICKS), lambda i: (i, 0))
    shp = jax.ShapeDtypeStruct((t, PK_PICKS), F32)
    return pl.pallas_call(
        _topk_kernel,
        grid=(t // TK_TM,),
        in_specs=[pl.BlockSpec((PK_HP, PK_KEYS, TK_TM), lambda i: (0, 0, i))],
        out_specs=[spec, spec, spec],
        out_shape=[shp, shp, shp],
        scratch_shapes=[pltpu.VMEM((PK_PICKS, TK_TM), F32)] * 3,
        compiler_params=_cparams(("parallel",)),
        name="peer_topk",
    )(s)


def _wbuild_kernel(i1_ref, i2_ref, g_ref, w_ref, stage_a, stage_b):
    sub = lax.broadcasted_iota(jnp.int32, (PK_SIDE, PK_PICKS), 0).astype(F32).astype(BF16)
    one = jnp.ones((PK_SIDE, PK_PICKS), BF16)
    zero = jnp.zeros((PK_SIDE, PK_PICKS), BF16)

    def grids(t0, stage):
        for tt in range(WB_GRP):
            r1 = i1_ref[pl.ds(t0 + tt, 1), :].astype(BF16)
            r2 = i2_ref[pl.ds(t0 + tt, 1), :].astype(BF16)
            rg = g_ref[pl.ds(t0 + tt, 1), :].astype(BF16)
            p1t = jnp.where(sub == jnp.broadcast_to(r1, sub.shape), one, zero)
            q2t = jnp.where(sub == jnp.broadcast_to(r2, sub.shape), jnp.broadcast_to(rg, sub.shape), zero)
            stage[tt * WB_STRIDE:tt * WB_STRIDE + PK_SIDE, :] = _dot_nt(p1t, q2t)

    def regroup(t0, stage):
        for i1 in range(PK_SIDE):
            lo = stage[pl.ds(i1, SUBLANES, stride=WB_STRIDE), :]
            hi = stage[pl.ds(i1 + SUBLANES * WB_STRIDE, SUBLANES, stride=WB_STRIDE), :]
            w_ref[0, i1, pl.ds(t0, WB_GRP), :] = jnp.concatenate([lo, hi], axis=0).astype(BF16)

    def pair(gi, carry):
        t_a = pl.multiple_of(gi * 2 * WB_GRP, WB_GRP)
        t_b = pl.multiple_of(gi * 2 * WB_GRP + WB_GRP, WB_GRP)
        grids(t_a, stage_a)
        grids(t_b, stage_b)
        regroup(t_a, stage_a)
        regroup(t_b, stage_b)
        return carry

    lax.fori_loop(0, WB_TM // (2 * WB_GRP), pair, 0)


def _wbuild_call(i1, i2, g):
    spec = pl.BlockSpec((WB_TM, PK_PICKS), lambda i: (i, 0))
    return pl.pallas_call(
        _wbuild_kernel,
        grid=(i1.shape[0] // WB_TM,),
        in_specs=[spec, spec, spec],
        out_specs=pl.BlockSpec((1, PK_SIDE, WB_TM, PK_SIDE), lambda i: (i, 0, 0, 0)),
        out_shape=jax.ShapeDtypeStruct((i1.shape[0] // WB_TM, PK_SIDE, WB_TM, PK_SIDE), BF16),
        scratch_shapes=[pltpu.VMEM((WB_GRP * WB_STRIDE, PK_SIDE), F32)] * 2,
        compiler_params=_cparams(("parallel",)),
        name="peer_route",
    )(i1, i2, g)


def _peer_kernel(h_ref, u_ref, v_ref, w_ref, x_ref, g2_ref, o_ref, acc):
    j = pl.program_id(1)

    @pl.when(j == 0)
    def _():
        acc[...] = jnp.zeros_like(acc)

    a = _dot_nt(h_ref[...], u_ref[0].astype(BF16))
    nsub = PM_TM // WB_TM
    w = jnp.concatenate(
        [jnp.concatenate([w_ref[n, sl] for sl in range(PM_SLABS)], axis=1) for n in range(nsub)], axis=0)
    acc[...] += _dot((jax.nn.gelu(a) * w.astype(F32)).astype(BF16), v_ref[0].astype(BF16))

    @pl.when(j == pl.num_programs(1) - 1)
    def _():
        o_ref[...] = x_ref[...] + g2_ref[0] * acc[...]


def _peer_call(h16, u_tabs, v_tabs, layer, w, x, mod3):
    ne = PM_SLABS * PK_SIDE
    tok = pl.BlockSpec((PM_TM, D_MODEL), lambda i, j: (i, 0))
    return pl.pallas_call(
        _peer_kernel,
        grid=(x.shape[0] // PM_TM, PK_SIDE // PM_SLABS),
        in_specs=[tok,
                  pl.BlockSpec((1, ne, D_MODEL), lambda i, j: (layer, j, 0)),
                  pl.BlockSpec((1, ne, D_MODEL), lambda i, j: (layer, j, 0)),
                  pl.BlockSpec((PM_TM // WB_TM, PM_SLABS, WB_TM, PK_SIDE), lambda i, j: (i, j, 0, 0)),
                  tok,
                  pl.BlockSpec((1, 1, D_MODEL), lambda i, j: (_cond_row(i, PM_TM) * 6 + 5, 0, 0))],
        out_specs=tok,
        out_shape=jax.ShapeDtypeStruct(x.shape, F32),
        scratch_shapes=[pltpu.VMEM((PM_TM, D_MODEL), F32)],
        compiler_params=_cparams(("parallel", "arbitrary")),
        name="peer_experts",
    )(h16, u_tabs, v_tabs, w, x, mod3)


def _final_norm_kernel(x_ref, g_ref, o_ref):
    x = x_ref[...]
    o_ref[...] = x * lax.rsqrt(jnp.mean(x * x, axis=-1, keepdims=True) + NORM_EPS) * g_ref[...]


def _final_norm_call(x, g):
    spec = pl.BlockSpec((TM, D_MODEL), lambda i: (i, 0))
    return pl.pallas_call(
        _final_norm_kernel,
        grid=(x.shape[0] // TM,),
        in_specs=[spec, pl.BlockSpec((1, D_MODEL), lambda i: (0, 0))],
        out_specs=spec,
        out_shape=jax.ShapeDtypeStruct(x.shape, F32),
        compiler_params=_cparams(("parallel",)),
        name="final_norm",
    )(x, g)


def _dn_params(conv_w, a_log, dt_bias, dn_norm):
    bc = lambda p: jnp.broadcast_to(p.astype(F32).T[:, :, None, None], (DN_HEADS, 2, 1, LANES))
    return {"conv": conv_w.astype(F32), "alog": bc(a_log), "dtb": bc(dt_bias),
            "gn": dn_norm.astype(F32).reshape(1, LANES)}


def _dn_call(z, qkv, col, alog, dtb, gn, s0, seq_len, n_seq, row0, with_state_out):
    blk0 = row0 // seq_len
    hd = DN_HEADS
    nhb = hd // HPS
    wide = HPS * LANES
    gate0 = col["gate"] // HPS
    one = pl.Buffered(1)
    in_specs = [pl.BlockSpec((seq_len, wide), lambda s, h: (s, h), pipeline_mode=one),
                pl.BlockSpec((seq_len, wide), lambda s, h: (s, nhb + h), pipeline_mode=one),
                pl.BlockSpec((seq_len, wide), lambda s, h: (s, 2 * nhb + h), pipeline_mode=one),
                pl.BlockSpec((seq_len, LANES), lambda s, h: (blk0 + s, col["ab"])),
                pl.BlockSpec((seq_len, wide), lambda s, h: (blk0 + s, gate0 + h), pipeline_mode=one),
                pl.BlockSpec((HPS, 2, 1, LANES), lambda s, h: (h, 0, 0, 0)),
                pl.BlockSpec((HPS, 2, 1, LANES), lambda s, h: (h, 0, 0, 0)),
                pl.BlockSpec((1, LANES), lambda s, h: (0, 0))]
    args = [qkv, qkv, qkv, z, z, alog, dtb, gn]
    st_spec = pl.BlockSpec((1, 2, HPS, LANES, LANES), lambda s, h: (s, 0, h, 0, 0))
    has_s0 = s0 is not None
    if has_s0:
        in_specs.append(st_spec)
        args.append(s0)
    out_specs = [pl.BlockSpec((seq_len, wide), lambda s, h: (s, h))]
    out_shape = [jax.ShapeDtypeStruct((n_seq * seq_len, hd * LANES), BF16)]
    if with_state_out:
        out_specs.append(st_spec)
        out_shape.append(jax.ShapeDtypeStruct((n_seq, 2, hd, LANES, LANES), F32))
    return pl.pallas_call(
        functools.partial(_dn_kernel, seq_len=seq_len, with_state_out=with_state_out, has_s0=has_s0),
        grid=(n_seq, nhb),
        in_specs=in_specs,
        out_specs=out_specs,
        out_shape=out_shape,
        scratch_shapes=[pltpu.VMEM((seq_len, wide), F32), pltpu.VMEM((seq_len, wide), F32)],
        compiler_params=_cparams(("parallel", "parallel")),
        name=f"mix_dn_{seq_len}",
    )(*args)


def _grid_col_major(t):
    n, d = t.shape
    rows = DEC_SEQ // GRID_W
    return t.reshape(DEC_BATCH, rows, GRID_W, d).transpose(0, 2, 1, 3).reshape(n, d)


def _grid_row_major(t):
    n, d = t.shape
    rows = DEC_SEQ // GRID_W
    return t.reshape(DEC_BATCH, GRID_W, rows, d).transpose(0, 2, 1, 3).reshape(n, d)


def _both_groups(fn, s0_sample):
    res_p = fn(None, SEQ, BATCH, 0, True)
    res_s = fn(s0_sample, DEC_SEQ, DEC_BATCH, T_PROMPT, False)
    return jnp.concatenate([res_p[0], res_s[0]], axis=0), res_p[1]


def kernel(x_prompt, x_sample, state_gla, state_hgrn, state_s5_re, state_s5_im, state_dn, c, c_ctx, w_ada, b_ada, norm1, norm2, w_in_even, w_gla_gate, b_gla_gate, gla_norm, hg_lb, hg_norm, w_in_odd, s5_lam_re, s5_lam_im, s5_log_dt, s5_b_re, s5_b_im, s5_c_re, s5_c_im, s5_d, s5_w_glu, s5_b_glu, dn_conv, dn_a_log, dn_dt_bias, dn_norm, w_out, pk_w_q, pk_keys, pk_u, pk_v, final_norm):
    x = jnp.concatenate([x_prompt.reshape(T_PROMPT, D_MODEL), x_sample.reshape(T_SAMPLE, D_MODEL)], axis=0)
    cond = jnp.zeros((N_COND, D_MODEL), F32).at[0].set(c_ctx.astype(F32)).at[1:1 + DEC_BATCH].set(c.astype(F32))
    mod = _ada_call(cond, w_ada.astype(F32), b_ada.astype(F32))

    out_g, out_h, out_re, out_im, out_dn = [], [], [], [], []
    for l in range(DEPTH):
        mod3 = mod[l].reshape(N_COND * 6, 1, D_MODEL)
        g1 = norm1[l].astype(F32).reshape(1, D_MODEL)
        if l % 2 == 0:
            e = l // 2
            w_in = _relayout_cols(w_in_even[e], EVEN_IDX).astype(BF16)
            z = _in_call(x, mod3, g1, w_in, 2)
            prm = _even_params(w_gla_gate[e], b_gla_gate[e], gla_norm[e], hg_lb, hg_norm[e], e)
            s0g = jnp.pad(state_gla[:, e].astype(F32), ((0, 0),) * 3 + ((0, HG_DK - GLA_DK), (0, 0)))
            m_g, st_g = _both_groups(
                lambda s0, L, n, r0, so: _even_call(z, "gla", EVEN_COL, prm["gla"], s0, L, n, r0, so), s0g)
            m_h, st_h = _both_groups(
                lambda s0, L, n, r0, so: _even_call(z, "hgrn", EVEN_COL, prm["hgrn"], s0, L, n, r0, so),
                state_hgrn[:, e].astype(F32))
            m = jnp.concatenate([m_g, m_h], axis=1)
            out_g.append(st_g[..., :GLA_DK, :])
            out_h.append(st_h)
        else:
            j = l // 2
            x_in = jnp.concatenate([x[:T_PROMPT], _grid_col_major(x[T_PROMPT:])], axis=0)
            w_in = _relayout_cols(w_in_odd[j], ODD_IDX).astype(BF16)
            z = _in_call(x_in, mod3, g1, w_in, 3)
            bb, cc, apow = _s5_params(s5_lam_re[j], s5_lam_im[j], s5_log_dt[j], s5_b_re[j], s5_b_im[j],
                                      s5_c_re[j], s5_c_im[j])
            dsk = s5_d[j].astype(F32).reshape(1, MIX_W)
            y_p, fr, fi = _s5_call(z, ODD_COL["u"], bb, cc, apow, dsk, None, None, SEQ, BATCH, 0)
            y_s, _, _ = _s5_call(z, ODD_COL["u"], bb, cc, apow, dsk,
                                 state_s5_re[:, j].astype(F32).reshape(DEC_BATCH, 2, -1),
                                 state_s5_im[:, j].astype(F32).reshape(DEC_BATCH, 2, -1),
                                 DEC_SEQ, DEC_BATCH, T_PROMPT)
            m_s5 = _glu_call(jnp.concatenate([y_p, y_s], axis=0), s5_w_glu[j].astype(BF16),
                             s5_b_glu[j].astype(F32).reshape(1, MIX_W))
            dp = _dn_params(dn_conv[j], dn_a_log[j], dn_dt_bias[j], dn_norm[j])

            def dn(s0, L, n, r0, so):
                qkv = _conv_call(z, ODD_COL["qkv"], dp["conv"], L, n, r0)
                return _dn_call(z, qkv, ODD_COL, dp["alog"], dp["dtb"], dp["gn"], s0, L, n, r0, so)

            m_dn, st_dn = _both_groups(dn, state_dn[:, j].astype(F32))
            m = jnp.concatenate([m_s5, m_dn], axis=1)
            m = jnp.concatenate([m[:T_PROMPT], _grid_row_major(m[T_PROMPT:])], axis=0)
            out_re.append(fr.reshape(BATCH, 2, S5_G, S5_P))
            out_im.append(fi.reshape(BATCH, 2, S5_G, S5_P))
            out_dn.append(st_dn)
        x, h2 = _out_call(x, m, w_out[l].astype(BF16), mod3, norm2[l].astype(F32).reshape(1, D_MODEL))
        s = _score_call(h2, pk_w_q[l].astype(BF16), pk_keys[l].reshape(PK_HP, PK_KEYS, PK_DQ // 2).astype(BF16))
        i1, i2, gate = _topk_call(s)
        w = _wbuild_call(i1, i2, gate)
        x = _peer_call(h2, pk_u, pk_v, l, w, x, mod3)
    y = _final_norm_call(x, final_norm.astype(F32).reshape(1, D_MODEL))
    return (y[:T_PROMPT].reshape(BATCH, SEQ, D_MODEL), y[T_PROMPT:].reshape(DEC_BATCH, DEC_SEQ, D_MODEL),
            jnp.stack(out_g, axis=1), jnp.stack(out_h, axis=1), jnp.stack(out_re, axis=1),
            jnp.stack(out_im, axis=1), jnp.stack(out_dn, axis=1))
```

```python
import functools

import numpy as np
import jax
import jax.numpy as jnp
from jax import lax
from jax.experimental import pallas as pl
from jax.experimental.pallas import tpu as pltpu

F32 = jnp.float32
BF16 = jnp.bfloat16

D_MODEL = 1024
BATCH = 32
SEQ = 256
DEPTH = 4
DEC_BATCH = 8
DEC_SEQ = 2048
GRID_W = 64
N_EVEN = 2
N_ODD = 2
MIX_W = 512
GLA_HEADS = 4
GLA_DK = 64
GLA_DV = 128
GLA_RANK = 16
GLA_TAU = 16.0
HG_HEADS = 4
HG_DK = 128
HG_DV = 128
S5_GS = 16
S5_G = 32
S5_P = 64
DN_HEADS = 4
DN_DK = 128
DN_DV = 128
DN_CONV = 5
PK_HEADS = 8
PK_KEYS = 128
PK_DQ = 256
PK_TOPK = 16
NORM_EPS = 1e-6
GATE_FLOOR = 1e-30

T_PROMPT = BATCH * SEQ
T_SAMPLE = DEC_BATCH * DEC_SEQ
T_ALL = T_PROMPT + T_SAMPLE
N_COND = 16

LANES = 128
SUBLANES = 8
VMEM_LIMIT = 56 * 1024 * 1024

TM = 512


def _layout(pieces):
    idx, col = [], {}
    for name, src, width in pieces:
        assert len(idx) % LANES == 0 and width % LANES == 0
        col[name] = len(idx) // LANES
        src = list(src)
        idx += src + [-1] * (width - len(src))
    return np.asarray(idx, np.int32), col


def _even_layout():
    o = np.cumsum((0, 256, 256, 512, 16, 16, 512, 512, 512, 512, 512, 512))
    rng = lambda i: range(o[i], o[i + 1])
    pieces = []
    for nm, base in (("gq", o[0]), ("gk", o[1])):
        for h in range(GLA_HEADS):
            pieces.append((nm if h == 0 else f"{nm}{h}", range(base + 64 * h, base + 64 * h + 64), LANES))
    pieces += [("gv", rng(2), 512), ("gr", rng(5), 512),
               ("hq", rng(6), 512), ("hf_f", rng(7), 512), ("hf_b", rng(8), 512),
               ("hi", rng(9), 512), ("hgate", rng(10), 512),
               ("lr", list(rng(3)) + list(rng(4)), LANES), ("pad", [], LANES)]
    return _layout(pieces)


def _odd_layout():
    o = np.cumsum((0, 512, 1536, 4, 4, 4, 4, 512))
    rng = lambda i: range(o[i], o[i + 1])
    return _layout([("u", rng(0), 512), ("qkv", rng(1), 1536), ("gate", rng(6), 512),
                    ("ab", range(o[2], o[6]), LANES)])


EVEN_IDX, EVEN_COL = _even_layout()
ODD_IDX, ODD_COL = _odd_layout()


def _relayout_cols(w, idx):
    runs = []
    for v in (int(v) for v in idx):
        grows = runs and ((v < 0 and runs[-1][0] < 0) or (v >= 0 and runs[-1][0] >= 0 and runs[-1][0] + runs[-1][1] == v))
        if grows:
            runs[-1][1] += 1
        else:
            runs.append([max(v, -1), 1])
    pieces = [jnp.zeros(w.shape[:-1] + (n,), w.dtype) if c < 0 else w[..., c:c + n] for c, n in runs]
    return jnp.concatenate(pieces, axis=-1)


def _cparams(sem):
    return pltpu.CompilerParams(dimension_semantics=sem, vmem_limit_bytes=VMEM_LIMIT)


def _cond_row(i, tm=TM):
    n_p = T_PROMPT // tm
    per_seq = DEC_SEQ // tm
    return jnp.where(i < n_p, 0, 1 + (i - n_p) // per_seq)


def _split3(x):
    hi = x.astype(BF16)
    r1 = x - hi.astype(F32)
    mid = r1.astype(BF16)
    lo = (r1 - mid.astype(F32)).astype(BF16)
    return hi, mid, lo


def _dot(a, b):
    return jnp.dot(a, b, preferred_element_type=F32)


def _dot_nt(a, b):
    return lax.dot_general(a, b, (((1,), (1,)), ((), ())), preferred_element_type=F32)


def _dot_tn(a, b):
    return lax.dot_general(a, b, (((0,), (0,)), ((), ())), preferred_element_type=F32)


def _dot_sel(sel_bf16, x):
    hi, mid, lo = _split3(x)
    return _dot(sel_bf16, hi) + _dot(sel_bf16, mid) + _dot(sel_bf16, lo)


def _dot_x_sel(x, sel_bf16):
    hi, mid, lo = _split3(x)
    return _dot(hi, sel_bf16) + _dot(mid, sel_bf16) + _dot(lo, sel_bf16)


def _dot_hi(a, b):
    ah = a.astype(BF16)
    al = (a - ah.astype(F32)).astype(BF16)
    bh = b.astype(BF16)
    bl = (b - bh.astype(F32)).astype(BF16)
    return _dot(ah, bh) + (_dot(ah, bl) + _dot(al, bh))


def _ada_kernel(c_ref, w_ref, b_ref, o_ref):
    c = c_ref[...]
    cs = c * jax.nn.sigmoid(c)
    o_ref[0] = _dot(cs, w_ref[0]) + b_ref[0]


def _ada_call(cond, w_ada, b_ada):
    nt = 6
    return pl.pallas_call(
        _ada_kernel,
        grid=(DEPTH, nt),
        in_specs=[
            pl.BlockSpec((N_COND, D_MODEL), lambda l, j: (0, 0)),
            pl.BlockSpec((1, D_MODEL, D_MODEL), lambda l, j: (l, 0, j)),
            pl.BlockSpec((1, 1, D_MODEL), lambda l, j: (l, 0, j)),
        ],
        out_specs=pl.BlockSpec((1, N_COND, D_MODEL), lambda l, j: (l, 0, j)),
        out_shape=jax.ShapeDtypeStruct((DEPTH, N_COND, 6 * D_MODEL), F32),
        compiler_params=_cparams(("parallel", "parallel")),
        name="ada_mod",
    )(cond, w_ada, b_ada.reshape(DEPTH, 1, 6 * D_MODEL))


def _modulated_norm(x, g, sc, sh):
    ms = jnp.mean(x * x, axis=-1, keepdims=True)
    y = x * lax.rsqrt(ms + NORM_EPS) * g
    return y * (1.0 + sc) + sh


N_PROMPT_TILES = T_PROMPT // TM


def _prompt_tile(i):
    return jnp.minimum(i, N_PROMPT_TILES - 1)


def _sample_tile(i):
    return jnp.maximum(i - N_PROMPT_TILES, 0)


def _in_kernel(*refs, sample_alt):
    if sample_alt:
        x_ref, xs_ref, sh_ref, sc_ref, g_ref, w_ref, o_ref = refs
        x = jnp.where(pl.program_id(1) >= N_PROMPT_TILES, xs_ref[...], x_ref[...])
    else:
        x_ref, sh_ref, sc_ref, g_ref, w_ref, o_ref = refs
        x = x_ref[...]
    h = _modulated_norm(x, g_ref[...], sc_ref[0], sh_ref[0])
    o_ref[...] = _dot(h.astype(BF16), w_ref[...])


def _in_call(x, x_sample_alt, mod3, g, w, n_col_tiles):
    n = w.shape[1]
    tn = n // n_col_tiles
    sample_alt = x_sample_alt is not None
    if sample_alt:
        tok_specs = [pl.BlockSpec((TM, D_MODEL), lambda j, i: (_prompt_tile(i), 0)),
                     pl.BlockSpec((TM, D_MODEL), lambda j, i: (_sample_tile(i), 0))]
        toks = [x, x_sample_alt]
    else:
        tok_specs = [pl.BlockSpec((TM, D_MODEL), lambda j, i: (i, 0))]
        toks = [x]
    return pl.pallas_call(
        functools.partial(_in_kernel, sample_alt=sample_alt),
        grid=(n_col_tiles, x.shape[0] // TM),
        in_specs=tok_specs + [
            pl.BlockSpec((1, 1, D_MODEL), lambda j, i: (_cond_row(i) * 6 + 0, 0, 0)),
            pl.BlockSpec((1, 1, D_MODEL), lambda j, i: (_cond_row(i) * 6 + 1, 0, 0)),
            pl.BlockSpec((1, D_MODEL), lambda j, i: (0, 0)),
            pl.BlockSpec((D_MODEL, tn), lambda j, i: (0, j)),
        ],
        out_specs=pl.BlockSpec((TM, tn), lambda j, i: (i, j)),
        out_shape=jax.ShapeDtypeStruct((x.shape[0], n), F32),
        compiler_params=_cparams(("parallel", "parallel")),
        name="in_proj",
    )(*toks, mod3, mod3, g, w)


def _out_kernel(x_ref, ap_ref, as_ref, bp_ref, bs_ref, wa_ref, wb_ref, g1_ref, sh_ref, sc_ref, g_ref,
                xo_ref, h_ref):
    is_sample = pl.program_id(0) >= N_PROMPT_TILES
    m_a = jnp.where(is_sample, as_ref[...], ap_ref[...])
    m_b = jnp.where(is_sample, bs_ref[...], bp_ref[...])
    x = x_ref[...] + g1_ref[0] * (_dot(m_a, wa_ref[...]) + _dot(m_b, wb_ref[...]))
    xo_ref[...] = x
    h_ref[...] = _modulated_norm(x, g_ref[...], sc_ref[0], sh_ref[0]).astype(h_ref.dtype)


def _out_call(x, m_a, m_b, w_out, mod3, g2):
    spec_tok = pl.BlockSpec((TM, D_MODEL), lambda i: (i, 0))
    spec_p = pl.BlockSpec((TM, MIX_W), lambda i: (_prompt_tile(i), 0))
    spec_s = pl.BlockSpec((TM, MIX_W), lambda i: (_sample_tile(i), 0))
    spec_w = pl.BlockSpec((MIX_W, D_MODEL), lambda i: (0, 0))

    def mod_spec(k):
        return pl.BlockSpec((1, 1, D_MODEL), lambda i: (_cond_row(i) * 6 + k, 0, 0))

    return pl.pallas_call(
        _out_kernel,
        grid=(x.shape[0] // TM,),
        in_specs=[
            spec_tok, spec_p, spec_s, spec_p, spec_s, spec_w, spec_w,
            mod_spec(2), mod_spec(3), mod_spec(4),
            pl.BlockSpec((1, D_MODEL), lambda i: (0, 0)),
        ],
        out_specs=[spec_tok, spec_tok],
        out_shape=[jax.ShapeDtypeStruct(x.shape, F32), jax.ShapeDtypeStruct(x.shape, BF16)],
        compiler_params=_cparams(("parallel",)),
        name="out_proj",
    )(x, m_a[0], m_a[1], m_b[0], m_b[1], w_out[:MIX_W], w_out[MIX_W:], mod3, mod3, mod3, g2)


CHUNK = 64
SUB = 16
NSUB = CHUNK // SUB
NEG_BIG = -1e30
HPS = 4


def _tri(n, lower, strict=False):
    r = lax.broadcasted_iota(jnp.int32, (n, n), 0)
    c = lax.broadcasted_iota(jnp.int32, (n, n), 1)
    if lower:
        return (c < r) if strict else (c <= r)
    return (c > r) if strict else (c >= r)


def _lockstep(gens):
    results = [None] * len(gens)
    active = list(enumerate(gens))
    while active:
        still = []
        for i, g in active:
            try:
                next(g)
                still.append((i, g))
            except StopIteration as stop:
                results[i] = stop.value
        active = still
    return results


def _gla_chunk(q, k, v, la, st, reverse):
    kdim = q.shape[1]
    tri = jnp.where(_tri(CHUNK, not reverse), 1.0, 0.0).astype(BF16)
    b = _dot_sel(tri, la)
    yield
    last = 0 if reverse else CHUNK - 1
    b_last = b[last:last + 1, :]
    o_inter = _dot_nt((q * jnp.exp(b)).astype(BF16), st.astype(BF16))
    kd = k * jnp.exp(b_last - b)
    st_new = st * jnp.exp(b_last) + _dot_tn(v.astype(BF16), kd.astype(BF16))
    yield

    ones_k = jnp.ones((kdim, LANES), BF16)
    jrow = lax.broadcasted_iota(jnp.int32, (SUB, kdim), 0)
    rsel = lax.broadcasted_iota(jnp.int32, (SUB, SUB * SUB), 0)
    csel = lax.broadcasted_iota(jnp.int32, (SUB, SUB * SUB), 1)
    in_grp = jnp.logical_and(csel >= rsel * SUB, csel < rsel * SUB + SUB)
    sel = jnp.where(in_grp, 1.0, 0.0).astype(BF16)
    outs = []
    for blk in range(NSUB):
        r0 = blk * SUB
        q_i, k_i, b_i, v_i = q[r0:r0 + SUB], k[r0:r0 + SUB], b[r0:r0 + SUB], v[r0:r0 + SUB]
        rows = []
        for i in range(SUB):
            keep = (jrow >= i) if reverse else (jrow <= i)
            e = jnp.exp(jnp.where(keep, b_i[i:i + 1, :] - b_i, NEG_BIG))
            rows.append((q_i[i:i + 1, :] * k_i) * e)
        p = jnp.concatenate(rows, axis=0)
        att = _dot(p.astype(BF16), ones_k)
        yield
        zv = att * jnp.concatenate([v_i] * SUB, axis=0)
        o_blk = _dot(sel, zv.astype(BF16))
        yield
        if reverse and blk < NSUB - 1:
            r1 = r0 + SUB
            ref = b[r1:r1 + 1, :]
            qt = q_i * jnp.exp(b_i - ref)
            kt = k[r1:] * jnp.exp(ref - b[r1:])
            a_off = _dot_nt(qt.astype(BF16), kt.astype(BF16))
            yield
            o_blk = o_blk + _dot(a_off.astype(BF16), v[r1:].astype(BF16))
        if (not reverse) and blk > 0:
            ref = b[r0 - 1:r0, :]
            qt = q_i * jnp.exp(b_i - ref)
            kt = k[:r0] * jnp.exp(ref - b[:r0])
            a_off = _dot_nt(qt.astype(BF16), kt.astype(BF16))
            yield
            o_blk = o_blk + _dot(a_off.astype(BF16), v[:r0].astype(BF16))
        outs.append(o_blk)
    return o_inter + jnp.concatenate(outs, axis=0), st_new


def _log_sigmoid(x):
    return -(jnp.maximum(-x, 0.0) + jnp.log1p(jnp.exp(-jnp.abs(x))))


def _head_norm_gate(o, g, gate):
    o = o * lax.rsqrt(jnp.mean(o * o, axis=-1, keepdims=True) + NORM_EPS)
    return (o * g) * (gate * jax.nn.sigmoid(gate))


def _even_kernel(*refs, mode, seq_len, with_state_out, has_s0):
    it = iter(refs)
    if mode == "gla":
        q_ref, k_ref, v_ref, lr_ref, gate_ref, wg_ref, bg_ref, gn_ref = (next(it) for _ in range(8))
    else:
        q_ref, ff_ref, fb_ref, v_ref, gate_ref, lb_ref, gn_ref = (next(it) for _ in range(7))
    s0_ref = next(it) if has_s0 else None
    m_ref = next(it)
    so_ref = next(it) if with_state_out else None
    of_scr, ob_scr = next(it), next(it)
    n_chunks = seq_len // CHUNK
    streams = [(hh, d) for hh in range(HPS) for d in range(2)]

    def chain(c, hh, d, st):
        rows = pl.ds(pl.multiple_of(c * CHUNK, CHUNK), CHUNK)
        cols = slice(hh * LANES, (hh + 1) * LANES)
        if mode == "gla":
            q = q_ref[rows, cols] * (GLA_DK ** -0.5)
            k = k_ref[rows, cols]
            pre = _dot_hi(lr_ref[rows, :], wg_ref[hh, d]) + bg_ref[hh, d]
            yield
            la = _log_sigmoid(pre) / GLA_TAU
        else:
            q = q_ref[rows, cols]
            f = (ff_ref, fb_ref)[d][rows, cols]
            lb = lb_ref[d:d + 1, cols]
            gate = lb + (1.0 - lb) * jax.nn.sigmoid(f)
            la = jnp.log(jnp.maximum(gate, GATE_FLOOR))
            k = (1.0 - lb) * jax.nn.sigmoid(-f)
        o, st_new = yield from _gla_chunk(q, k, v_ref[rows, cols], la, st, d == 1)
        return rows, cols, o, st_new

    def body(c, carry):
        res = _lockstep([chain(n_chunks - 1 - c if d else c, hh, d, carry[i])
                         for i, (hh, d) in enumerate(streams)])
        for (hh, d), (rows, cols, o, _) in zip(streams, res):
            (ob_scr if d else of_scr)[rows, cols] = o
        return tuple(r[3] for r in res)

    if has_s0:
        init = tuple(s0_ref[0, d, hh].T for hh, d in streams)
    else:
        init = tuple(jnp.zeros((LANES, LANES), F32) for _ in streams)
    fin = lax.fori_loop(0, n_chunks, body, init)
    for hh in range(HPS):
        cols = slice(hh * LANES, (hh + 1) * LANES)
        m_ref[:, cols] = _head_norm_gate(of_scr[:, cols] + ob_scr[:, cols], gn_ref[...],
                                         gate_ref[:, cols]).astype(m_ref.dtype)
    if with_state_out:
        for i, (hh, d) in enumerate(streams):
            so_ref[0, d, hh] = fin[i].T


def _even_params(w_gate, b_gate, gla_norm, hg_lb, hg_norm, e):
    wg = jnp.zeros((GLA_HEADS, 2, LANES, LANES), F32)
    bg = jnp.zeros((GLA_HEADS, 2, 1, LANES), F32)
    for h in range(GLA_HEADS):
        for d in range(2):
            wg = wg.at[h, d, d * GLA_RANK:(d + 1) * GLA_RANK, :GLA_DK].set(
                w_gate[d, :, h * GLA_DK:(h + 1) * GLA_DK].astype(F32))
            bg = bg.at[h, d, 0, :GLA_DK].set(b_gate[d, h * GLA_DK:(h + 1) * GLA_DK].astype(F32))
    lbp = jax.nn.softmax(hg_lb.astype(F32), axis=1)
    lb = (jnp.cumsum(lbp, axis=1) - lbp[:, :1])[:, e]
    return {"gla": (wg, bg, gla_norm.astype(F32).reshape(1, LANES)),
            "hgrn": (lb, hg_norm.astype(F32).reshape(1, LANES))}


def _even_call(z, mode, col, params, s0, seq_len, n_seq, row0, with_state_out):
    heads = 4
    blk0 = row0 // seq_len
    wide = HPS * LANES

    def tok(name):
        c0 = col[name] // HPS
        return pl.BlockSpec((seq_len, wide), lambda s, h: (blk0 + s, c0 + h), pipeline_mode=pl.Buffered(1))

    if mode == "gla":
        wg, bg, gn = params
        in_specs = [tok("gq"), tok("gk"), tok("gv"),
                    pl.BlockSpec((seq_len, LANES), lambda s, h: (blk0 + s, col["lr"])),
                    tok("gr"),
                    pl.BlockSpec((HPS, 2, LANES, LANES), lambda s, h: (h, 0, 0, 0)),
                    pl.BlockSpec((HPS, 2, 1, LANES), lambda s, h: (h, 0, 0, 0)),
                    pl.BlockSpec((1, LANES), lambda s, h: (0, 0))]
        args = [z, z, z, z, z, wg, bg, gn]
    else:
        lb, gn = params
        in_specs = [tok("hq"), tok("hf_f"), tok("hf_b"), tok("hi"), tok("hgate"),
                    pl.BlockSpec((2, wide), lambda s, h: (0, h)),
                    pl.BlockSpec((1, LANES), lambda s, h: (0, 0))]
        args = [z, z, z, z, z, lb, gn]
    st_spec = pl.BlockSpec((1, 2, HPS, LANES, LANES), lambda s, h: (s, 0, h, 0, 0))
    has_s0 = s0 is not None
    if has_s0:
        in_specs.append(st_spec)
        args.append(s0)
    out_specs = [pl.BlockSpec((seq_len, wide), lambda s, h: (s, h))]
    out_shape = [jax.ShapeDtypeStruct((n_seq * seq_len, heads * LANES), BF16)]
    if with_state_out:
        out_specs.append(st_spec)
        out_shape.append(jax.ShapeDtypeStruct((n_seq, 2, heads, LANES, LANES), F32))
    res = pl.pallas_call(
        functools.partial(_even_kernel, mode=mode, seq_len=seq_len,
                          with_state_out=with_state_out, has_s0=has_s0),
        grid=(n_seq, heads // HPS),
        in_specs=in_specs,
        out_specs=out_specs,
        out_shape=out_shape,
        scratch_shapes=[pltpu.VMEM((seq_len, wide), F32), pltpu.VMEM((seq_len, wide), F32)],
        compiler_params=_cparams(("parallel", "parallel")),
        name=f"mix_{mode}_{seq_len}",
    )(*args)
    return res


CONV_PAD = SUBLANES


def _conv_kernel(x_ref, w_ref, o_ref, pad_scr, *, seq_len):
    cb = pl.program_id(1)
    wide = DN_HEADS * LANES
    zeros = jnp.zeros((CONV_PAD, wide), F32)
    pad_scr[0:CONV_PAD, :] = zeros
    pad_scr[CONV_PAD + seq_len:2 * CONV_PAD + seq_len, :] = zeros
    pad_scr[CONV_PAD:CONV_PAD + seq_len, :] = x_ref[...]
    is_q = cb == 0
    is_qk = cb < 2
    scale = jnp.where(is_q, DN_DK ** -0.5, 1.0)
    for hh in range(DN_HEADS):
        cols = slice(hh * LANES, (hh + 1) * LANES)
        acc = jnp.zeros((seq_len, LANES), F32)
        for w in range(DN_CONV):
            acc = acc + pad_scr[pl.ds(CONV_PAD + w - DN_CONV // 2, seq_len), cols] * w_ref[w:w + 1, cols]
        y = acc * jax.nn.sigmoid(acc)
        yn = y * lax.rsqrt(jnp.sum(y * y, axis=-1, keepdims=True) + NORM_EPS)
        o_ref[:, cols] = jnp.where(is_qk, yn * scale, y)


def _conv_call(z, col0, w, seq_len, n_seq, row0):
    blk0 = row0 // seq_len
    wide = DN_HEADS * LANES
    c0 = col0 // DN_HEADS
    return pl.pallas_call(
        functools.partial(_conv_kernel, seq_len=seq_len),
        grid=(n_seq, 3),
        in_specs=[pl.BlockSpec((seq_len, wide), lambda s, c: (blk0 + s, c0 + c)),
                  pl.BlockSpec((DN_CONV, wide), lambda s, c: (0, c))],
        out_specs=pl.BlockSpec((seq_len, wide), lambda s, c: (s, c)),
        out_shape=jax.ShapeDtypeStruct((n_seq * seq_len, 3 * wide), F32),
        scratch_shapes=[pltpu.VMEM((seq_len + 2 * CONV_PAD, wide), F32)],
        compiler_params=_cparams(("parallel", "parallel")),
        name=f"dn_conv_{seq_len}",
    )(z, w)


def _softplus(x):
    return jnp.maximum(x, 0.0) + jnp.log1p(jnp.exp(-jnp.abs(x)))


def _dn_chunk(q, k, v, g, beta, s, reverse):
    c = CHUNK
    lower = not reverse
    tri = jnp.where(_tri(c, lower), 1.0, 0.0).astype(BF16)
    gam = _dot_sel(tri, g)
    yield
    ones_c = jnp.ones((c, c), BF16)
    gam_row = _dot_sel(ones_c, jnp.where(_tri(c, reverse), g[:, :c], 0.0))
    incl = _tri(c, lower)
    strict = _tri(c, lower, strict=True)
    diff = gam[:, :c] - gam_row
    dec = jnp.where(incl, jnp.exp(jnp.where(incl, diff, 0.0)), 0.0)
    kb = k * beta
    k16 = k.astype(BF16)
    m = jnp.where(strict, _dot_nt(kb.astype(BF16), k16) * dec, 0.0)
    yield
    eg = jnp.exp(gam)
    x = jnp.concatenate([v * beta, kb * eg], axis=1)
    p = -m
    n_fac = int(np.log2(c))
    for j in range(n_fac):
        x = x + _dot_hi(p, x)
        if j + 1 < n_fac:
            p = _dot_hi(p, p)
        yield
    u, w = x[:, :LANES], x[:, LANES:]
    s16 = s.astype(BF16)
    v_new = u - _dot(w.astype(BF16), s16)
    att = _dot_nt(q.astype(BF16), k16) * dec
    yield
    o = _dot((q * eg).astype(BF16), s16) + _dot(att.astype(BF16), v_new.astype(BF16))
    last = 0 if reverse else c - 1
    gl = gam[last:last + 1, :]
    s_new = s * jnp.exp(gl) + _dot_tn((k * jnp.exp(gl - gam)).astype(BF16), v_new.astype(BF16))
    return o, s_new


def _dn_kernel(*refs, seq_len, with_state_out, has_s0):
    it = iter(refs)
    q_ref, k_ref, v_ref, ab_ref, gate_ref, alog_ref, dtb_ref, gn_ref = (next(it) for _ in range(8))
    s0_ref = next(it) if has_s0 else None
    m_ref = next(it)
    so_ref = next(it) if with_state_out else None
    of_scr, ob_scr = next(it), next(it)
    h0 = pl.program_id(1) * HPS
    n_chunks = seq_len // CHUNK
    srow = lax.broadcasted_iota(jnp.int32, (LANES, LANES), 0)
    streams = [(hh, d) for hh in range(HPS) for d in range(2)]

    def chain(c, hh, d, s):
        rows = pl.ds(pl.multiple_of(c * CHUNK, CHUNK), CHUNK)
        cols = slice(hh * LANES, (hh + 1) * LANES)
        ab = ab_ref[rows, :]
        sel_a = jnp.where(srow == DN_HEADS * d + h0 + hh, 1.0, 0.0).astype(BF16)
        sel_b = jnp.where(srow == 2 * DN_HEADS + DN_HEADS * d + h0 + hh, 1.0, 0.0).astype(BF16)
        a = _dot_x_sel(ab, sel_a)
        beta = jax.nn.sigmoid(_dot_x_sel(ab, sel_b))
        g = -jnp.exp(alog_ref[hh, d]) * _softplus(a + dtb_ref[hh, d])
        yield
        o, s_new = yield from _dn_chunk(q_ref[rows, cols], k_ref[rows, cols], v_ref[rows, cols],
                                        g, beta, s, d == 1)
        return rows, cols, o, s_new

    def body(c, carry):
        res = _lockstep([chain(n_chunks - 1 - c if d else c, hh, d, carry[i])
                         for i, (hh, d) in enumerate(streams)])
        for (hh, d), (rows, cols, o, _) in zip(streams, res):
            (ob_scr if d else of_scr)[rows, cols] = o
        return tuple(r[3] for r in res)

    if has_s0:
        init = tuple(s0_ref[0, d, hh] for hh, d in streams)
    else:
        init = tuple(jnp.zeros((LANES, LANES), F32) for _ in streams)
    fin = lax.fori_loop(0, n_chunks, body, init)
    for hh in range(HPS):
        cols = slice(hh * LANES, (hh + 1) * LANES)
        m_ref[:, cols] = _head_norm_gate(of_scr[:, cols] + ob_scr[:, cols], gn_ref[...],
                                         gate_ref[:, cols]).astype(m_ref.dtype)
    if with_state_out:
        for i, (hh, d) in enumerate(streams):
            so_ref[0, d, hh] = fin[i]


S5_TC = 128
S5_CB = 4
S5_SW = S5_G * S5_P // S5_CB


def _s5_params(lam_re, lam_im, log_dt, b_re, b_im, c_re, c_im):
    gpb = S5_G // S5_CB
    eye = jnp.eye(gpb, dtype=F32)
    n = jnp.arange(1, S5_TC + 1, dtype=F32)[:, None, None]
    bbs, ccs, aps = [], [], []
    for d in range(2):
        lr = jnp.minimum(lam_re[d].astype(F32), -1e-4)
        li = lam_im[d].astype(F32)
        dt = jnp.exp(log_dt[d].astype(F32))[:, None]
        mag = jnp.exp(lr * dt)
        ar, ai = mag * jnp.cos(li * dt), mag * jnp.sin(li * dt)
        den = lr * lr + li * li
        nr = ar - 1.0
        cr = (nr * lr + ai * li) / den
        ci = (ai * lr - nr * li) / den
        bbr = cr[..., None] * b_re - ci[..., None] * b_im
        bbi = cr[..., None] * b_im + ci[..., None] * b_re
        blk = lambda t: jnp.einsum('bgpc,gh->bgchp', t.reshape(S5_CB, gpb, S5_P, S5_GS), eye).reshape(
            S5_CB, gpb * S5_GS, gpb * S5_P)
        bbs.append(jnp.stack([blk(bbr), blk(bbi)]))
        cblk = lambda t: jnp.einsum('bgcp,gh->bgphc', t.reshape(S5_CB, gpb, S5_GS, S5_P), eye).reshape(
            S5_CB, gpb * S5_P, gpb * S5_GS)
        ccs.append(jnp.stack([cblk(c_re[d].astype(F32)), cblk(c_im[d].astype(F32))]))
        pm = jnp.exp(n * (lr * dt)[None])
        pr = (pm * jnp.cos(n * (li * dt)[None])).reshape(S5_TC, -1)
        pi = (pm * jnp.sin(n * (li * dt)[None])).reshape(S5_TC, -1)
        if d == 1:
            pr, pi = pr[::-1], pi[::-1]
        aps.append(jnp.stack([pr, pi]))
    bb = jnp.stack(bbs, axis=2).astype(BF16)
    cc = jnp.stack(ccs, axis=2).astype(BF16)
    apow = jnp.stack(aps, axis=1)
    return bb, cc, apow


def _s5_scan(xr, xi, pr, pi, cr, ci, reverse, loc, loc0):
    tc, w = xr.shape
    g8 = SUBLANES
    ng = tc // g8

    def power(n):
        idx = tc - n if reverse else n - 1
        return pr[idx:idx + 1, :], pi[idx:idx + 1, :]

    def step(xr, xi, s, unit):
        n = xr.shape[0]
        row = lax.broadcasted_iota(jnp.int32, xr.shape, 0)
        period = g8 if unit == 1 else n
        pos = jnp.bitwise_and(row, period - 1) if unit == 1 else row
        keep = (pos < period - s) if reverse else (pos >= s)
        shift = n - s if reverse else s
        a_r, a_i = power(s * unit)
        if unit == 1:
            rot = lambda x: pltpu.roll(x.reshape(n // g8, g8, w), g8 - s if reverse else s, 1).reshape(n, w)
        else:
            rot = lambda x: pltpu.roll(x, shift, 0)
        sr = jnp.where(keep, rot(xr), 0.0)
        si = jnp.where(keep, rot(xi), 0.0)
        return xr + (a_r * sr - a_i * si), xi + (a_r * si + a_i * sr)

    s = 1
    while s < g8:
        xr, xi = step(xr, xi, s, 1)
        s *= 2
    end = 0 if reverse else g8 - 1
    ers, eis = [], []
    for k in range(w // LANES):
        r0, i0 = (loc0 + k) * tc, (loc0 + w // LANES + k) * tc
        loc[r0:r0 + tc, :] = xr[:, k * LANES:(k + 1) * LANES]
        loc[i0:i0 + tc, :] = xi[:, k * LANES:(k + 1) * LANES]
        ers.append(loc[pl.ds(r0 + end, ng, stride=g8), :])
        eis.append(loc[pl.ds(i0 + end, ng, stride=g8), :])
    er = jnp.concatenate(ers, axis=1)
    ei = jnp.concatenate(eis, axis=1)
    s = 1
    while s < ng:
        er, ei = step(er, ei, s, g8)
        s *= 2
    grow = lax.broadcasted_iota(jnp.int32, (ng, w), 0)
    if reverse:
        pin_r = jnp.concatenate([pr[g8 * (g + 1):g8 * (g + 1) + 1] for g in range(ng - 1)] + [jnp.ones((1, w), F32)], 0)
        pin_i = jnp.concatenate([pi[g8 * (g + 1):g8 * (g + 1) + 1] for g in range(ng - 1)] + [jnp.zeros((1, w), F32)], 0)
        prev_r = jnp.where(grow < ng - 1, pltpu.roll(er, ng - 1, 0), 0.0)
        prev_i = jnp.where(grow < ng - 1, pltpu.roll(ei, ng - 1, 0), 0.0)
    else:
        pin_r = jnp.concatenate([jnp.ones((1, w), F32)] + [pr[g8 * g - 1:g8 * g] for g in range(1, ng)], 0)
        pin_i = jnp.concatenate([jnp.zeros((1, w), F32)] + [pi[g8 * g - 1:g8 * g] for g in range(1, ng)], 0)
        prev_r = jnp.where(grow >= 1, pltpu.roll(er, 1, 0), 0.0)
        prev_i = jnp.where(grow >= 1, pltpu.roll(ei, 1, 0), 0.0)
    in_r = prev_r + (pin_r * cr - pin_i * ci)
    in_i = prev_i + (pin_r * ci + pin_i * cr)
    in_r = jnp.concatenate([jnp.broadcast_to(in_r[g:g + 1], (g8, w)) for g in range(ng)], axis=0)
    in_i = jnp.concatenate([jnp.broadcast_to(in_i[g:g + 1], (g8, w)) for g in range(ng)], axis=0)
    p8r = pr[tc - g8:] if reverse else pr[:g8]
    p8i = pi[tc - g8:] if reverse else pi[:g8]
    a_r = jnp.concatenate([p8r] * ng, axis=0)
    a_i = jnp.concatenate([p8i] * ng, axis=0)
    return xr + (a_r * in_r - a_i * in_i), xi + (a_r * in_i + a_i * in_r)


def _s5_kernel(*refs, seq_len, has_h0):
    it = iter(refs)
    u_ref, bb_ref, cc_ref, ap_ref, dsk_ref = (next(it) for _ in range(5))
    h0r_ref, h0i_ref = (next(it), next(it)) if has_h0 else (None, None)
    y_ref, fr_ref, fi_ref = next(it), next(it), next(it)
    yf_scr, yb_scr, loc_scr = next(it), next(it), next(it)
    n_chunks = seq_len // S5_TC

    def run(c, d, carry):
        cr, ci = carry
        rows = pl.ds(pl.multiple_of(c * S5_TC, S5_TC), S5_TC)
        ub = u_ref[rows, :].astype(BF16)
        pr, pi = ap_ref[0, d], ap_ref[1, d]
        xr, xi = _dot(ub, bb_ref[0, 0, d]), _dot(ub, bb_ref[1, 0, d])
        hr, hi = _s5_scan(xr, xi, pr, pi, cr, ci, d == 1, loc_scr, d * 2 * (S5_SW // LANES))
        (yf_scr, yb_scr)[d][rows, :] = (_dot(hr.astype(BF16), cc_ref[0, 0, d])
                                         - _dot(hi.astype(BF16), cc_ref[1, 0, d]))
        last = 0 if d == 1 else S5_TC - 1
        return hr[last:last + 1, :], hi[last:last + 1, :]

    def body(c, carry):
        cf, cb = carry
        return run(c, 0, cf), run(n_chunks - 1 - c, 1, cb)

    if has_h0:
        init = tuple((h0r_ref[0, d:d + 1, :], h0i_ref[0, d:d + 1, :]) for d in range(2))
    else:
        z = jnp.zeros((1, S5_SW), F32)
        init = ((z, z), (z, z))
    fin = lax.fori_loop(0, n_chunks, body, init)
    y_ref[...] = dsk_ref[...] * u_ref[...] + (yf_scr[...] + yb_scr[...])
    for d in range(2):
        fr_ref[0, d:d + 1, :] = fin[d][0]
        fi_ref[0, d:d + 1, :] = fin[d][1]


def _s5_call(z, col0, bb, cc, apow, dsk, h0r, h0i, seq_len, n_seq, row0):
    blk0 = row0 // seq_len
    has_h0 = h0r is not None
    st_spec = pl.BlockSpec((1, 2, S5_SW), lambda s, c: (s, 0, c))
    in_specs = [pl.BlockSpec((seq_len, LANES), lambda s, c: (blk0 + s, col0 + c)),
                pl.BlockSpec((2, 1, 2, LANES, S5_SW), lambda s, c: (0, c, 0, 0, 0)),
                pl.BlockSpec((2, 1, 2, S5_SW, LANES), lambda s, c: (0, c, 0, 0, 0)),
                pl.BlockSpec((2, 2, S5_TC, S5_SW), lambda s, c: (0, 0, 0, c)),
                pl.BlockSpec((1, LANES), lambda s, c: (0, c))]
    args = [z, bb, cc, apow, dsk]
    if has_h0:
        in_specs += [st_spec, st_spec]
        args += [h0r, h0i]
    st_shape = jax.ShapeDtypeStruct((n_seq, 2, S5_G * S5_P), F32)
    return pl.pallas_call(
        functools.partial(_s5_kernel, seq_len=seq_len, has_h0=has_h0),
        grid=(n_seq, S5_CB),
        in_specs=in_specs,
        out_specs=[pl.BlockSpec((seq_len, LANES), lambda s, c: (s, c)), st_spec, st_spec],
        out_shape=[jax.ShapeDtypeStruct((n_seq * seq_len, MIX_W), F32), st_shape, st_shape],
        scratch_shapes=[pltpu.VMEM((seq_len, LANES), F32), pltpu.VMEM((seq_len, LANES), F32),
                        pltpu.VMEM((4 * (S5_SW // LANES) * S5_TC, LANES), F32)],
        compiler_params=_cparams(("parallel", "parallel")),
        name=f"mix_s5_{seq_len}",
    )(*args)


def _glu_kernel(y_ref, w_ref, b_ref, o_ref):
    zz = jax.nn.gelu(y_ref[...])
    o_ref[...] = (zz * jax.nn.sigmoid(_dot(zz.astype(BF16), w_ref[...]) + b_ref[...])).astype(o_ref.dtype)


def _glu_call(y, w, b):
    t = y.shape[0]
    return pl.pallas_call(
        _glu_kernel,
        grid=(t // TM,),
        in_specs=[pl.BlockSpec((TM, MIX_W), lambda i: (i, 0)),
                  pl.BlockSpec((MIX_W, MIX_W), lambda i: (0, 0)),
                  pl.BlockSpec((1, MIX_W), lambda i: (0, 0))],
        out_specs=pl.BlockSpec((TM, MIX_W), lambda i: (i, 0)),
        out_shape=jax.ShapeDtypeStruct((t, MIX_W), BF16),
        compiler_params=_cparams(("parallel",)),
        name="s5_glu",
    )(y, w, b)


PK_HP = 2 * PK_HEADS
PK_SIDE = PK_KEYS
PK_PICKS = PK_HEADS * PK_TOPK
TK_TM = 128
TK_HG = 2
WB_TM = 128
WB_GRP = 16
WB_STRIDE = PK_SIDE + SUBLANES
PM_TM = 1024
PM_SLABS = 8


def _score_kernel(h_ref, wq_ref, keys_ref, s_ref):
    q = _dot(h_ref[...], wq_ref[...])
    for hp in range(PK_HP):
        s_ref[hp] = _dot_nt(keys_ref[hp], q[:, hp * LANES:(hp + 1) * LANES].astype(BF16))


def _score_call(h, wq, keys):
    return pl.pallas_call(
        _score_kernel,
        grid=(h.shape[0] // TM,),
        in_specs=[pl.BlockSpec((TM, D_MODEL), lambda i: (i, 0)),
                  pl.BlockSpec((D_MODEL, PK_HP * LANES), lambda i: (0, 0)),
                  pl.BlockSpec((PK_HP, PK_KEYS, LANES), lambda i: (0, 0, 0))],
        out_specs=pl.BlockSpec((PK_HP, PK_KEYS, TM), lambda i: (0, 0, i)),
        out_shape=jax.ShapeDtypeStruct((PK_HP, PK_KEYS, h.shape[0]), F32),
        compiler_params=_cparams(("parallel",)),
        name="peer_scores",
    )(h, wq, keys)


def _top16_sorted(s):
    n_grp = s.shape[0] // SUBLANES
    n = s.shape[1]
    base = lax.broadcasted_iota(jnp.int32, (SUBLANES, n), 0).astype(F32)
    vals = [s[SUBLANES * g:SUBLANES * (g + 1), :] for g in range(n_grp)]
    keys = [base + float(SUBLANES * g) for g in range(n_grp)]
    for rnd in range(n_grp):
        for k in range(rnd % 2, n_grp - 1, 2):
            a, b, ka, kb = vals[k], vals[k + 1], keys[k], keys[k + 1]
            swap = b > a
            vals[k], vals[k + 1] = jnp.maximum(a, b), jnp.minimum(a, b)
            keys[k], keys[k + 1] = jnp.where(swap, kb, ka), jnp.where(swap, ka, kb)
    rank = lax.broadcasted_iota(jnp.int32, (PK_TOPK, n), 0)
    top_v = jnp.zeros((PK_TOPK, n), F32)
    top_k = jnp.zeros((PK_TOPK, n), F32)
    for r in range(PK_TOPK):
        m = jnp.max(vals[0], axis=0, keepdims=True)
        am = jnp.min(jnp.where(vals[0] == m, keys[0], float(s.shape[0])), axis=0, keepdims=True)
        top_v = jnp.where(rank == r, m, top_v)
        top_k = jnp.where(rank == r, am, top_k)
        hit = keys[0] == am
        for k in range(min(n_grp - 1, PK_TOPK - 1 - r)):
            vals[k] = jnp.where(hit, vals[k + 1], vals[k])
            keys[k] = jnp.where(hit, keys[k + 1], keys[k])
    return top_v, top_k


def _top16_rows(tiles):
    shape = tiles[0].shape
    row = lax.broadcasted_iota(jnp.int32, shape, 0).astype(F32)
    out_row = lax.broadcasted_iota(jnp.int32, (PK_TOPK, shape[1]), 0)

    def take(r, s, vals, idxs):
        m = jnp.max(s, axis=0, keepdims=True)
        am = jnp.min(jnp.where(s == m, row, float(shape[0])), axis=0, keepdims=True)
        at = out_row == r
        return jnp.where(row == am, -jnp.inf, s), jnp.where(at, m, vals), jnp.where(at, am, idxs)

    def body(r, carry):
        out = ()
        for i in range(len(tiles)):
            out += take(r, *carry[3 * i:3 * i + 3])
        return out

    zero = jnp.zeros((PK_TOPK, shape[1]), F32)
    init = ()
    for s in tiles:
        init += (s, zero, zero)
    res = lax.fori_loop(0, PK_TOPK, body, init)
    return [(res[3 * i + 1], res[3 * i + 2]) for i in range(len(tiles))]


def _top16_pairs(v1, a1, v2, a2):
    half = PK_TOPK // 2
    n = v1.shape[1]
    rank = lax.broadcasted_iota(jnp.int32, (PK_TOPK, n), 0)
    rank_f = rank.astype(F32)
    vals = [v1 + jnp.broadcast_to(v2[0:1], (PK_TOPK, n))]
    vals += [v1[:half] + jnp.broadcast_to(v2[b:b + 1], (half, n)) for b in range(1, PK_TOPK)]
    ids = [rank_f * PK_TOPK] + [rank_f[:half] * PK_TOPK + float(b) for b in range(1, PK_TOPK)]
    best = jnp.zeros((PK_TOPK, n), F32)
    p1 = jnp.zeros((PK_TOPK, n), F32)
    p2 = jnp.zeros((PK_TOPK, n), F32)
    for r in range(PK_TOPK):
        m = jnp.max(vals[0], axis=0, keepdims=True)
        pos = jnp.min(jnp.where(vals[0] == m, ids[0], float(PK_TOPK * PK_TOPK)), axis=0, keepdims=True)
        hit = ids[0] == pos
        a_sel = jnp.max(jnp.where(hit, rank_f, -1.0), axis=0, keepdims=True)
        e1 = jnp.max(jnp.where(hit, a1, -1.0), axis=0, keepdims=True)
        e2 = jnp.max(jnp.where(rank_f == pos - a_sel * PK_TOPK, a2, -1.0), axis=0, keepdims=True)
        best = jnp.where(rank == r, m, best)
        p1 = jnp.where(rank == r, e1, p1)
        p2 = jnp.where(rank == r, e2, p2)
        if r + 1 < PK_TOPK:
            hit_lo = hit[:half]
            vals[0] = jnp.concatenate([jnp.where(hit_lo, vals[1], vals[0][:half]),
                                       jnp.where(hit[half:], -jnp.inf, vals[0][half:])], axis=0)
            ids[0] = jnp.concatenate([jnp.where(hit_lo, ids[1], ids[0][:half]), ids[0][half:]], axis=0)
            for k in range(1, PK_TOPK - 1 - r):
                vals[k] = jnp.where(hit_lo, vals[k + 1], vals[k])
                ids[k] = jnp.where(hit_lo, ids[k + 1], ids[k])
    return best, p1, p2


def _topk_kernel(s_ref, i1_ref, i2_ref, g_ref, p1_scr, p2_scr, g_scr):
    def heads(hg, carry):
        tops = [_top16_sorted(s_ref[2 * TK_HG * hg + i]) for i in range(2 * TK_HG)]
        for j in range(TK_HG):
            (v1, a1), (v2, a2) = tops[2 * j], tops[2 * j + 1]
            best, p1, p2 = _top16_pairs(v1, a1, v2, a2)
            ex = jnp.exp(best - jnp.max(best, axis=0, keepdims=True))
            rows = pl.ds(pl.multiple_of((hg * TK_HG + j) * PK_TOPK, PK_TOPK), PK_TOPK)
            p1_scr[rows, :] = p1
            p2_scr[rows, :] = p2
            g_scr[rows, :] = ex / jnp.sum(ex, axis=0, keepdims=True)
        return carry

    lax.fori_loop(0, PK_HEADS // TK_HG, heads, 0)
    i1_ref[...] = p1_scr[...].T
    i2_ref[...] = p2_scr[...].T
    g_ref[...] = g_scr[...].T


def _topk_call(s):
    t = s.shape[2]
    spec = pl.BlockSpec((TK_TM, PK_PICKS), lambda i: (i, 0))
    shp = jax.ShapeDtypeStruct((t, PK_PICKS), F32)
    return pl.pallas_call(
        _topk_kernel,
        grid=(t // TK_TM,),
        in_specs=[pl.BlockSpec((PK_HP, PK_KEYS, TK_TM), lambda i: (0, 0, i))],
        out_specs=[spec, spec, spec],
        out_shape=[shp, shp, shp],
        scratch_shapes=[pltpu.VMEM((PK_PICKS, TK_TM), F32)] * 3,
        compiler_params=_cparams(("parallel",)),
        name="peer_topk",
    )(s)


def _wbuild_kernel(i1_ref, i2_ref, g_ref, w_ref, stage_a, stage_b):
    sub = lax.broadcasted_iota(jnp.int32, (PK_SIDE, PK_PICKS), 0).astype(F32).astype(BF16)
    one = jnp.ones((PK_SIDE, PK_PICKS), BF16)
    zero = jnp.zeros((PK_SIDE, PK_PICKS), BF16)

    def grids(t0, stage):
        for tt in range(WB_GRP):
            r1 = i1_ref[pl.ds(t0 + tt, 1), :].astype(BF16)
            r2 = i2_ref[pl.ds(t0 + tt, 1), :].astype(BF16)
            rg = g_ref[pl.ds(t0 + tt, 1), :].astype(BF16)
            p1t = jnp.where(sub == jnp.broadcast_to(r1, sub.shape), one, zero)
            q2t = jnp.where(sub == jnp.broadcast_to(r2, sub.shape), jnp.broadcast_to(rg, sub.shape), zero)
            stage[tt * WB_STRIDE:tt * WB_STRIDE + PK_SIDE, :] = _dot_nt(p1t, q2t)

    def regroup(t0, stage):
        for i1 in range(PK_SIDE):
            lo = stage[pl.ds(i1, SUBLANES, stride=WB_STRIDE), :]
            hi = stage[pl.ds(i1 + SUBLANES * WB_STRIDE, SUBLANES, stride=WB_STRIDE), :]
            w_ref[0, i1, pl.ds(t0, WB_GRP), :] = jnp.concatenate([lo, hi], axis=0).astype(BF16)

    def pair(gi, carry):
        t_a = pl.multiple_of(gi * 2 * WB_GRP, WB_GRP)
        t_b = pl.multiple_of(gi * 2 * WB_GRP + WB_GRP, WB_GRP)
        grids(t_a, stage_a)
        grids(t_b, stage_b)
        regroup(t_a, stage_a)
        regroup(t_b, stage_b)
        return carry

    lax.fori_loop(0, WB_TM // (2 * WB_GRP), pair, 0)


def _wbuild_call(i1, i2, g):
    spec = pl.BlockSpec((WB_TM, PK_PICKS), lambda i: (i, 0))
    return pl.pallas_call(
        _wbuild_kernel,
        grid=(i1.shape[0] // WB_TM,),
        in_specs=[spec, spec, spec],
        out_specs=pl.BlockSpec((1, PK_SIDE, WB_TM, PK_SIDE), lambda i: (i, 0, 0, 0)),
        out_shape=jax.ShapeDtypeStruct((i1.shape[0] // WB_TM, PK_SIDE, WB_TM, PK_SIDE), BF16),
        scratch_shapes=[pltpu.VMEM((WB_GRP * WB_STRIDE, PK_SIDE), F32)] * 2,
        compiler_params=_cparams(("parallel",)),
        name="peer_route",
    )(i1, i2, g)


def _peer_kernel(h_ref, u_ref, v_ref, w_ref, x_ref, g2_ref, o_ref, acc):
    j = pl.program_id(1)

    @pl.when(j == 0)
    def _():
        acc[...] = jnp.zeros_like(acc)

    a = _dot_nt(h_ref[...], u_ref[0].astype(BF16))
    nsub = PM_TM // WB_TM
    w = jnp.concatenate(
        [jnp.concatenate([w_ref[n, sl] for sl in range(PM_SLABS)], axis=1) for n in range(nsub)], axis=0)
    acc[...] += _dot((jax.nn.gelu(a) * w.astype(F32)).astype(BF16), v_ref[0].astype(BF16))

    @pl.when(j == pl.num_programs(1) - 1)
    def _():
        o_ref[...] = x_ref[...] + g2_ref[0] * acc[...]


def _peer_call(h16, u_tabs, v_tabs, layer, w, x, mod3):
    ne = PM_SLABS * PK_SIDE
    tok = pl.BlockSpec((PM_TM, D_MODEL), lambda i, j: (i, 0))
    return pl.pallas_call(
        _peer_kernel,
        grid=(x.shape[0] // PM_TM, PK_SIDE // PM_SLABS),
        in_specs=[tok,
                  pl.BlockSpec((1, ne, D_MODEL), lambda i, j: (layer, j, 0)),
                  pl.BlockSpec((1, ne, D_MODEL), lambda i, j: (layer, j, 0)),
                  pl.BlockSpec((PM_TM // WB_TM, PM_SLABS, WB_TM, PK_SIDE), lambda i, j: (i, j, 0, 0)),
                  tok,
                  pl.BlockSpec((1, 1, D_MODEL), lambda i, j: (_cond_row(i, PM_TM) * 6 + 5, 0, 0))],
        out_specs=tok,
        out_shape=jax.ShapeDtypeStruct(x.shape, F32),
        scratch_shapes=[pltpu.VMEM((PM_TM, D_MODEL), F32)],
        compiler_params=_cparams(("parallel", "arbitrary")),
        name="peer_experts",
    )(h16, u_tabs, v_tabs, w, x, mod3)


def _final_norm_kernel(x_ref, g_ref, o_ref):
    x = x_ref[...]
    o_ref[...] = x * lax.rsqrt(jnp.mean(x * x, axis=-1, keepdims=True) + NORM_EPS) * g_ref[...]


def _final_norm_call(x, g):
    spec = pl.BlockSpec((TM, D_MODEL), lambda i: (i, 0))
    return pl.pallas_call(
        _final_norm_kernel,
        grid=(x.shape[0] // TM,),
        in_specs=[spec, pl.BlockSpec((1, D_MODEL), lambda i: (0, 0))],
        out_specs=spec,
        out_shape=jax.ShapeDtypeStruct(x.shape, F32),
        compiler_params=_cparams(("parallel",)),
        name="final_norm",
    )(x, g)


def _dn_params(conv_w, a_log, dt_bias, dn_norm):
    bc = lambda p: jnp.broadcast_to(p.astype(F32).T[:, :, None, None], (DN_HEADS, 2, 1, LANES))
    return {"conv": conv_w.astype(F32), "alog": bc(a_log), "dtb": bc(dt_bias),
            "gn": dn_norm.astype(F32).reshape(1, LANES)}


def _dn_call(z, qkv, col, alog, dtb, gn, s0, seq_len, n_seq, row0, with_state_out):
    blk0 = row0 // seq_len
    hd = DN_HEADS
    nhb = hd // HPS
    wide = HPS * LANES
    gate0 = col["gate"] // HPS
    one = pl.Buffered(1)
    in_specs = [pl.BlockSpec((seq_len, wide), lambda s, h: (s, h), pipeline_mode=one),
                pl.BlockSpec((seq_len, wide), lambda s, h: (s, nhb + h), pipeline_mode=one),
                pl.BlockSpec((seq_len, wide), lambda s, h: (s, 2 * nhb + h), pipeline_mode=one),
                pl.BlockSpec((seq_len, LANES), lambda s, h: (blk0 + s, col["ab"])),
                pl.BlockSpec((seq_len, wide), lambda s, h: (blk0 + s, gate0 + h), pipeline_mode=one),
                pl.BlockSpec((HPS, 2, 1, LANES), lambda s, h: (h, 0, 0, 0)),
                pl.BlockSpec((HPS, 2, 1, LANES), lambda s, h: (h, 0, 0, 0)),
                pl.BlockSpec((1, LANES), lambda s, h: (0, 0))]
    args = [qkv, qkv, qkv, z, z, alog, dtb, gn]
    st_spec = pl.BlockSpec((1, 2, HPS, LANES, LANES), lambda s, h: (s, 0, h, 0, 0))
    has_s0 = s0 is not None
    if has_s0:
        in_specs.append(st_spec)
        args.append(s0)
    out_specs = [pl.BlockSpec((seq_len, wide), lambda s, h: (s, h))]
    out_shape = [jax.ShapeDtypeStruct((n_seq * seq_len, hd * LANES), BF16)]
    if with_state_out:
        out_specs.append(st_spec)
        out_shape.append(jax.ShapeDtypeStruct((n_seq, 2, hd, LANES, LANES), F32))
    return pl.pallas_call(
        functools.partial(_dn_kernel, seq_len=seq_len, with_state_out=with_state_out, has_s0=has_s0),
        grid=(n_seq, nhb),
        in_specs=in_specs,
        out_specs=out_specs,
        out_shape=out_shape,
        scratch_shapes=[pltpu.VMEM((seq_len, wide), F32), pltpu.VMEM((seq_len, wide), F32)],
        compiler_params=_cparams(("parallel", "parallel")),
        name=f"mix_dn_{seq_len}",
    )(*args)


def _grid_col_major(t):
    n, d = t.shape
    rows = DEC_SEQ // GRID_W
    return t.reshape(DEC_BATCH, rows, GRID_W, d).transpose(0, 2, 1, 3).reshape(n, d)


def _grid_row_major(t):
    n, d = t.shape
    rows = DEC_SEQ // GRID_W
    return t.reshape(DEC_BATCH, GRID_W, rows, d).transpose(0, 2, 1, 3).reshape(n, d)


def _both_groups(fn, s0_sample):
    res_p = fn(None, SEQ, BATCH, 0, True)
    res_s = fn(s0_sample, DEC_SEQ, DEC_BATCH, T_PROMPT, False)
    return (res_p[0], res_s[0]), res_p[1]


def kernel(x_prompt, x_sample, state_gla, state_hgrn, state_s5_re, state_s5_im, state_dn, c, c_ctx, w_ada, b_ada, norm1, norm2, w_in_even, w_gla_gate, b_gla_gate, gla_norm, hg_lb, hg_norm, w_in_odd, s5_lam_re, s5_lam_im, s5_log_dt, s5_b_re, s5_b_im, s5_c_re, s5_c_im, s5_d, s5_w_glu, s5_b_glu, dn_conv, dn_a_log, dn_dt_bias, dn_norm, w_out, pk_w_q, pk_keys, pk_u, pk_v, final_norm):
    x = jnp.concatenate([x_prompt.reshape(T_PROMPT, D_MODEL), x_sample.reshape(T_SAMPLE, D_MODEL)], axis=0)
    cond = jnp.zeros((N_COND, D_MODEL), F32).at[0].set(c_ctx.astype(F32)).at[1:1 + DEC_BATCH].set(c.astype(F32))
    mod = _ada_call(cond, w_ada.astype(F32), b_ada.astype(F32))

    out_g, out_h, out_re, out_im, out_dn = [], [], [], [], []
    for l in range(DEPTH):
        mod3 = mod[l].reshape(N_COND * 6, 1, D_MODEL)
        g1 = norm1[l].astype(F32).reshape(1, D_MODEL)
        if l % 2 == 0:
            e = l // 2
            w_in = _relayout_cols(w_in_even[e], EVEN_IDX).astype(BF16)
            z = _in_call(x, None, mod3, g1, w_in, 2)
            prm = _even_params(w_gla_gate[e], b_gla_gate[e], gla_norm[e], hg_lb, hg_norm[e], e)
            s0g = jnp.pad(state_gla[:, e].astype(F32), ((0, 0),) * 3 + ((0, HG_DK - GLA_DK), (0, 0)))
            m_g, st_g = _both_groups(
                lambda s0, L, n, r0, so: _even_call(z, "gla", EVEN_COL, prm["gla"], s0, L, n, r0, so), s0g)
            m_h, st_h = _both_groups(
                lambda s0, L, n, r0, so: _even_call(z, "hgrn", EVEN_COL, prm["hgrn"], s0, L, n, r0, so),
                state_hgrn[:, e].astype(F32))
            m_a, m_b = m_g, m_h
            out_g.append(st_g[..., :GLA_DK, :])
            out_h.append(st_h)
        else:
            j = l // 2
            w_in = _relayout_cols(w_in_odd[j], ODD_IDX).astype(BF16)
            z = _in_call(x, _grid_col_major(x[T_PROMPT:]), mod3, g1, w_in, 3)
            bb, cc, apow = _s5_params(s5_lam_re[j], s5_lam_im[j], s5_log_dt[j], s5_b_re[j], s5_b_im[j],
                                      s5_c_re[j], s5_c_im[j])
            dsk = s5_d[j].astype(F32).reshape(1, MIX_W)
            y_p, fr, fi = _s5_call(z, ODD_COL["u"], bb, cc, apow, dsk, None, None, SEQ, BATCH, 0)
            y_s, _, _ = _s5_call(z, ODD_COL["u"], bb, cc, apow, dsk,
                                 state_s5_re[:, j].astype(F32).reshape(DEC_BATCH, 2, -1),
                                 state_s5_im[:, j].astype(F32).reshape(DEC_BATCH, 2, -1),
                                 DEC_SEQ, DEC_BATCH, T_PROMPT)
            w_glu, b_glu = s5_w_glu[j].astype(BF16), s5_b_glu[j].astype(F32).reshape(1, MIX_W)
            m_s5 = (_glu_call(y_p, w_glu, b_glu), _glu_call(y_s, w_glu, b_glu))
            dp = _dn_params(dn_conv[j], dn_a_log[j], dn_dt_bias[j], dn_norm[j])

            def dn(s0, L, n, r0, so):
                qkv = _conv_call(z, ODD_COL["qkv"], dp["conv"], L, n, r0)
                return _dn_call(z, qkv, ODD_COL, dp["alog"], dp["dtb"], dp["gn"], s0, L, n, r0, so)

            m_dn, st_dn = _both_groups(dn, state_dn[:, j].astype(F32))
            m_a = (m_s5[0], _grid_row_major(m_s5[1]))
            m_b = (m_dn[0], _grid_row_major(m_dn[1]))
            out_re.append(fr.reshape(BATCH, 2, S5_G, S5_P))
            out_im.append(fi.reshape(BATCH, 2, S5_G, S5_P))
            out_dn.append(st_dn)
        x, h2 = _out_call(x, m_a, m_b, w_out[l].astype(BF16), mod3, norm2[l].astype(F32).reshape(1, D_MODEL))
        s = _score_call(h2, pk_w_q[l].astype(BF16), pk_keys[l].reshape(PK_HP, PK_KEYS, PK_DQ // 2).astype(BF16))
        i1, i2, gate = _topk_call(s)
        w = _wbuild_call(i1, i2, gate)
        x = _peer_call(h2, pk_u, pk_v, l, w, x, mod3)
    y = _final_norm_call(x, final_norm.astype(F32).reshape(1, D_MODEL))
    return (y[:T_PROMPT].reshape(BATCH, SEQ, D_MODEL), y[T_PROMPT:].reshape(DEC_BATCH, DEC_SEQ, D_MODEL),
            jnp.stack(out_g, axis=1), jnp.stack(out_h, axis=1), jnp.stack(out_re, axis=1),
            jnp.stack(out_im, axis=1), jnp.stack(out_dn, axis=1))
```

```python
import functools

import numpy as np
import jax
import jax.numpy as jnp
from jax import lax
from jax.experimental import pallas as pl
from jax.experimental.pallas import tpu as pltpu

F32 = jnp.float32
BF16 = jnp.bfloat16

D_MODEL = 1024
BATCH = 32
SEQ = 256
DEPTH = 4
DEC_BATCH = 8
DEC_SEQ = 2048
GRID_W = 64
N_EVEN = 2
N_ODD = 2
MIX_W = 512
GLA_HEADS = 4
GLA_DK = 64
GLA_DV = 128
GLA_RANK = 16
GLA_TAU = 16.0
HG_HEADS = 4
HG_DK = 128
HG_DV = 128
S5_GS = 16
S5_G = 32
S5_P = 64
DN_HEADS = 4
DN_DK = 128
DN_DV = 128
DN_CONV = 5
PK_HEADS = 8
PK_KEYS = 128
PK_DQ = 256
PK_TOPK = 16
NORM_EPS = 1e-6
GATE_FLOOR = 1e-30

T_PROMPT = BATCH * SEQ
T_SAMPLE = DEC_BATCH * DEC_SEQ
T_ALL = T_PROMPT + T_SAMPLE
N_COND = 16

LANES = 128
SUBLANES = 8
VMEM_LIMIT = 56 * 1024 * 1024

TM = 512


def _layout(pieces):
    idx, col = [], {}
    for name, src, width in pieces:
        assert len(idx) % LANES == 0 and width % LANES == 0
        col[name] = len(idx) // LANES
        src = list(src)
        idx += src + [-1] * (width - len(src))
    return np.asarray(idx, np.int32), col


def _even_layout():
    o = np.cumsum((0, 256, 256, 512, 16, 16, 512, 512, 512, 512, 512, 512))
    rng = lambda i: range(o[i], o[i + 1])
    pieces = []
    for nm, base in (("gq", o[0]), ("gk", o[1])):
        for h in range(GLA_HEADS):
            pieces.append((nm if h == 0 else f"{nm}{h}", range(base + 64 * h, base + 64 * h + 64), LANES))
    pieces += [("gv", rng(2), 512), ("gr", rng(5), 512),
               ("hq", rng(6), 512), ("hf_f", rng(7), 512), ("hf_b", rng(8), 512),
               ("hi", rng(9), 512), ("hgate", rng(10), 512),
               ("lr", list(rng(3)) + list(rng(4)), LANES), ("pad", [], LANES)]
    return _layout(pieces)


def _odd_layout():
    o = np.cumsum((0, 512, 1536, 4, 4, 4, 4, 512))
    rng = lambda i: range(o[i], o[i + 1])
    return _layout([("u", rng(0), 512), ("qkv", rng(1), 1536), ("gate", rng(6), 512),
                    ("ab", range(o[2], o[6]), LANES)])


EVEN_IDX, EVEN_COL = _even_layout()
ODD_IDX, ODD_COL = _odd_layout()


def _relayout_cols(w, idx):
    runs = []
    for v in (int(v) for v in idx):
        grows = runs and ((v < 0 and runs[-1][0] < 0) or (v >= 0 and runs[-1][0] >= 0 and runs[-1][0] + runs[-1][1] == v))
        if grows:
            runs[-1][1] += 1
        else:
            runs.append([max(v, -1), 1])
    pieces = [jnp.zeros(w.shape[:-1] + (n,), w.dtype) if c < 0 else w[..., c:c + n] for c, n in runs]
    return jnp.concatenate(pieces, axis=-1)


def _cparams(sem):
    return pltpu.CompilerParams(dimension_semantics=sem, vmem_limit_bytes=VMEM_LIMIT)


def _cond_row(i, tm=TM):
    n_p = T_PROMPT // tm
    per_seq = DEC_SEQ // tm
    return jnp.where(i < n_p, 0, 1 + (i - n_p) // per_seq)


def _split3(x):
    hi = x.astype(BF16)
    r1 = x - hi.astype(F32)
    mid = r1.astype(BF16)
    lo = (r1 - mid.astype(F32)).astype(BF16)
    return hi, mid, lo


def _dot(a, b):
    return jnp.dot(a, b, preferred_element_type=F32)


def _dot_nt(a, b):
    return lax.dot_general(a, b, (((1,), (1,)), ((), ())), preferred_element_type=F32)


def _dot_tn(a, b):
    return lax.dot_general(a, b, (((0,), (0,)), ((), ())), preferred_element_type=F32)


def _dot_sel(sel_bf16, x):
    hi, mid, lo = _split3(x)
    return _dot(sel_bf16, hi) + _dot(sel_bf16, mid) + _dot(sel_bf16, lo)


def _dot_x_sel(x, sel_bf16):
    hi, mid, lo = _split3(x)
    return _dot(hi, sel_bf16) + _dot(mid, sel_bf16) + _dot(lo, sel_bf16)


def _dot_hi(a, b):
    ah = a.astype(BF16)
    al = (a - ah.astype(F32)).astype(BF16)
    bh = b.astype(BF16)
    bl = (b - bh.astype(F32)).astype(BF16)
    return _dot(ah, bh) + (_dot(ah, bl) + _dot(al, bh))


def _ada_kernel(c_ref, w_ref, b_ref, o_ref):
    c = c_ref[...]
    cs = c * jax.nn.sigmoid(c)
    o_ref[0] = _dot(cs, w_ref[0]) + b_ref[0]


def _ada_call(cond, w_ada, b_ada):
    nt = 6
    return pl.pallas_call(
        _ada_kernel,
        grid=(DEPTH, nt),
        in_specs=[
            pl.BlockSpec((N_COND, D_MODEL), lambda l, j: (0, 0)),
            pl.BlockSpec((1, D_MODEL, D_MODEL), lambda l, j: (l, 0, j)),
            pl.BlockSpec((1, 1, D_MODEL), lambda l, j: (l, 0, j)),
        ],
        out_specs=pl.BlockSpec((1, N_COND, D_MODEL), lambda l, j: (l, 0, j)),
        out_shape=jax.ShapeDtypeStruct((DEPTH, N_COND, 6 * D_MODEL), F32),
        compiler_params=_cparams(("parallel", "parallel")),
        name="ada_mod",
    )(cond, w_ada, b_ada.reshape(DEPTH, 1, 6 * D_MODEL))


def _modulated_norm(x, g, sc, sh):
    ms = jnp.mean(x * x, axis=-1, keepdims=True)
    y = x * lax.rsqrt(ms + NORM_EPS) * g
    return y * (1.0 + sc) + sh


N_PROMPT_TILES = T_PROMPT // TM


def _prompt_tile(i):
    return jnp.minimum(i, N_PROMPT_TILES - 1)


def _sample_tile(i):
    return jnp.maximum(i - N_PROMPT_TILES, 0)


def _in_kernel(*refs, sample_alt):
    if sample_alt:
        x_ref, xs_ref, sh_ref, sc_ref, g_ref, w_ref, o_ref = refs
        x = jnp.where(pl.program_id(1) >= N_PROMPT_TILES, xs_ref[...], x_ref[...])
    else:
        x_ref, sh_ref, sc_ref, g_ref, w_ref, o_ref = refs
        x = x_ref[...]
    h = _modulated_norm(x, g_ref[...], sc_ref[0], sh_ref[0])
    o_ref[...] = _dot(h.astype(BF16), w_ref[...])


def _in_call(x, x_sample_alt, mod3, g, w, n_col_tiles):
    n = w.shape[1]
    tn = n // n_col_tiles
    sample_alt = x_sample_alt is not None
    if sample_alt:
        tok_specs = [pl.BlockSpec((TM, D_MODEL), lambda j, i: (_prompt_tile(i), 0)),
                     pl.BlockSpec((TM, D_MODEL), lambda j, i: (_sample_tile(i), 0))]
        toks = [x, x_sample_alt]
    else:
        tok_specs = [pl.BlockSpec((TM, D_MODEL), lambda j, i: (i, 0))]
        toks = [x]
    return pl.pallas_call(
        functools.partial(_in_kernel, sample_alt=sample_alt),
        grid=(n_col_tiles, x.shape[0] // TM),
        in_specs=tok_specs + [
            pl.BlockSpec((1, 1, D_MODEL), lambda j, i: (_cond_row(i) * 6 + 0, 0, 0)),
            pl.BlockSpec((1, 1, D_MODEL), lambda j, i: (_cond_row(i) * 6 + 1, 0, 0)),
            pl.BlockSpec((1, D_MODEL), lambda j, i: (0, 0)),
            pl.BlockSpec((D_MODEL, tn), lambda j, i: (0, j)),
        ],
        out_specs=pl.BlockSpec((TM, tn), lambda j, i: (i, j)),
        out_shape=jax.ShapeDtypeStruct((x.shape[0], n), F32),
        compiler_params=_cparams(("parallel", "parallel")),
        name="in_proj",
    )(*toks, mod3, mod3, g, w)


def _out_kernel(x_ref, ap_ref, as_ref, bp_ref, bs_ref, wa_ref, wb_ref, g1_ref, sh_ref, sc_ref, g_ref,
                xo_ref, h_ref):
    is_sample = pl.program_id(0) >= N_PROMPT_TILES
    m_a = jnp.where(is_sample, as_ref[...], ap_ref[...])
    m_b = jnp.where(is_sample, bs_ref[...], bp_ref[...])
    x = x_ref[...] + g1_ref[0] * (_dot(m_a, wa_ref[...]) + _dot(m_b, wb_ref[...]))
    xo_ref[...] = x
    h_ref[...] = _modulated_norm(x, g_ref[...], sc_ref[0], sh_ref[0]).astype(h_ref.dtype)


def _out_call(x, m_a, m_b, w_out, mod3, g2):
    spec_tok = pl.BlockSpec((TM, D_MODEL), lambda i: (i, 0))
    spec_p = pl.BlockSpec((TM, MIX_W), lambda i: (_prompt_tile(i), 0))
    spec_s = pl.BlockSpec((TM, MIX_W), lambda i: (_sample_tile(i), 0))
    spec_w = pl.BlockSpec((MIX_W, D_MODEL), lambda i: (0, 0))

    def mod_spec(k):
        return pl.BlockSpec((1, 1, D_MODEL), lambda i: (_cond_row(i) * 6 + k, 0, 0))

    return pl.pallas_call(
        _out_kernel,
        grid=(x.shape[0] // TM,),
        in_specs=[
            spec_tok, spec_p, spec_s, spec_p, spec_s, spec_w, spec_w,
            mod_spec(2), mod_spec(3), mod_spec(4),
            pl.BlockSpec((1, D_MODEL), lambda i: (0, 0)),
        ],
        out_specs=[spec_tok, spec_tok],
        out_shape=[jax.ShapeDtypeStruct(x.shape, F32), jax.ShapeDtypeStruct(x.shape, BF16)],
        compiler_params=_cparams(("parallel",)),
        name="out_proj",
    )(x, m_a[0], m_a[1], m_b[0], m_b[1], w_out[:MIX_W], w_out[MIX_W:], mod3, mod3, mod3, g2)


CHUNK = 64
SUB = 16
NSUB = CHUNK // SUB
NEG_BIG = -1e30
HPS = 4


def _tri(n, lower, strict=False):
    r = lax.broadcasted_iota(jnp.int32, (n, n), 0)
    c = lax.broadcasted_iota(jnp.int32, (n, n), 1)
    if lower:
        return (c < r) if strict else (c <= r)
    return (c > r) if strict else (c >= r)


def _lockstep(gens):
    results = [None] * len(gens)
    active = list(enumerate(gens))
    while active:
        still = []
        for i, g in active:
            try:
                next(g)
                still.append((i, g))
            except StopIteration as stop:
                results[i] = stop.value
        active = still
    return results


def _gla_chunk(q, k, v, la, st, reverse):
    kdim = q.shape[1]
    tri = jnp.where(_tri(CHUNK, not reverse), 1.0, 0.0).astype(BF16)
    b = _dot_sel(tri, la)
    yield
    last = 0 if reverse else CHUNK - 1
    b_last = b[last:last + 1, :]
    o_inter = _dot_nt((q * jnp.exp(b)).astype(BF16), st.astype(BF16))
    kd = k * jnp.exp(b_last - b)
    st_new = st * jnp.exp(b_last) + _dot_tn(v.astype(BF16), kd.astype(BF16))
    yield

    ones_k = jnp.ones((kdim, LANES), BF16)
    jrow = lax.broadcasted_iota(jnp.int32, (SUB, kdim), 0)
    rsel = lax.broadcasted_iota(jnp.int32, (SUB, SUB * SUB), 0)
    csel = lax.broadcasted_iota(jnp.int32, (SUB, SUB * SUB), 1)
    in_grp = jnp.logical_and(csel >= rsel * SUB, csel < rsel * SUB + SUB)
    sel = jnp.where(in_grp, 1.0, 0.0).astype(BF16)
    atts, offs = [], []
    for blk in range(NSUB):
        r0 = blk * SUB
        q_i, k_i, b_i = q[r0:r0 + SUB], k[r0:r0 + SUB], b[r0:r0 + SUB]
        rows = []
        for i in range(SUB):
            keep = (jrow >= i) if reverse else (jrow <= i)
            e = jnp.exp(jnp.where(keep, b_i[i:i + 1, :] - b_i, NEG_BIG))
            rows.append((q_i[i:i + 1, :] * k_i) * e)
        p = jnp.concatenate(rows, axis=0)
        atts.append(_dot(p.astype(BF16), ones_k))
        past = slice(r0 + SUB, CHUNK) if reverse else slice(0, r0)
        if past.start < past.stop:
            ref = b[r0 + SUB:r0 + SUB + 1, :] if reverse else b[r0 - 1:r0, :]
            qt = q_i * jnp.exp(b_i - ref)
            kt = k[past] * jnp.exp(ref - b[past])
            offs.append((past, _dot_nt(qt.astype(BF16), kt.astype(BF16))))
        else:
            offs.append(None)
    yield
    outs = []
    for blk in range(NSUB):
        v_i = v[blk * SUB:(blk + 1) * SUB]
        zv = atts[blk] * jnp.concatenate([v_i] * SUB, axis=0)
        o_blk = _dot(sel, zv.astype(BF16))
        if offs[blk] is not None:
            past, a_off = offs[blk]
            o_blk = o_blk + _dot(a_off.astype(BF16), v[past].astype(BF16))
        outs.append(o_blk)
    return o_inter + jnp.concatenate(outs, axis=0), st_new


def _log_sigmoid(x):
    return -(jnp.maximum(-x, 0.0) + jnp.log1p(jnp.exp(-jnp.abs(x))))


def _head_norm_gate(o, g, gate):
    o = o * lax.rsqrt(jnp.mean(o * o, axis=-1, keepdims=True) + NORM_EPS)
    return (o * g) * (gate * jax.nn.sigmoid(gate))


def _even_kernel(*refs, mode, seq_len, with_state_out, has_s0):
    it = iter(refs)
    if mode == "gla":
        q_ref, k_ref, v_ref, lr_ref, gate_ref, wg_ref, bg_ref, gn_ref = (next(it) for _ in range(8))
    else:
        q_ref, ff_ref, fb_ref, v_ref, gate_ref, lb_ref, gn_ref = (next(it) for _ in range(7))
    s0_ref = next(it) if has_s0 else None
    m_ref = next(it)
    so_ref = next(it) if with_state_out else None
    of_scr, ob_scr = next(it), next(it)
    n_chunks = seq_len // CHUNK
    streams = [(hh, d) for hh in range(HPS) for d in range(2)]

    def chain(c, hh, d, st):
        rows = pl.ds(pl.multiple_of(c * CHUNK, CHUNK), CHUNK)
        cols = slice(hh * LANES, (hh + 1) * LANES)
        if mode == "gla":
            q = q_ref[rows, cols] * (GLA_DK ** -0.5)
            k = k_ref[rows, cols]
            pre = _dot_hi(lr_ref[rows, :], wg_ref[hh, d]) + bg_ref[hh, d]
            yield
            la = _log_sigmoid(pre) / GLA_TAU
        else:
            q = q_ref[rows, cols]
            f = (ff_ref, fb_ref)[d][rows, cols]
            lb = lb_ref[d:d + 1, cols]
            gate = lb + (1.0 - lb) * jax.nn.sigmoid(f)
            la = jnp.log(jnp.maximum(gate, GATE_FLOOR))
            k = (1.0 - lb) * jax.nn.sigmoid(-f)
        o, st_new = yield from _gla_chunk(q, k, v_ref[rows, cols], la, st, d == 1)
        return rows, cols, o, st_new

    def body(c, carry):
        res = _lockstep([chain(n_chunks - 1 - c if d else c, hh, d, carry[i])
                         for i, (hh, d) in enumerate(streams)])
        for (hh, d), (rows, cols, o, _) in zip(streams, res):
            (ob_scr if d else of_scr)[rows, cols] = o
        return tuple(r[3] for r in res)

    if has_s0:
        init = tuple(s0_ref[0, d, hh].T for hh, d in streams)
    else:
        init = tuple(jnp.zeros((LANES, LANES), F32) for _ in streams)
    fin = lax.fori_loop(0, n_chunks, body, init)
    for hh in range(HPS):
        cols = slice(hh * LANES, (hh + 1) * LANES)
        m_ref[:, cols] = _head_norm_gate(of_scr[:, cols] + ob_scr[:, cols], gn_ref[...],
                                         gate_ref[:, cols]).astype(m_ref.dtype)
    if with_state_out:
        for i, (hh, d) in enumerate(streams):
            so_ref[0, d, hh] = fin[i].T


def _even_params(w_gate, b_gate, gla_norm, hg_lb, hg_norm, e):
    wg = jnp.zeros((GLA_HEADS, 2, LANES, LANES), F32)
    bg = jnp.zeros((GLA_HEADS, 2, 1, LANES), F32)
    for h in range(GLA_HEADS):
        for d in range(2):
            wg = wg.at[h, d, d * GLA_RANK:(d + 1) * GLA_RANK, :GLA_DK].set(
                w_gate[d, :, h * GLA_DK:(h + 1) * GLA_DK].astype(F32))
            bg = bg.at[h, d, 0, :GLA_DK].set(b_gate[d, h * GLA_DK:(h + 1) * GLA_DK].astype(F32))
    lbp = jax.nn.softmax(hg_lb.astype(F32), axis=1)
    lb = (jnp.cumsum(lbp, axis=1) - lbp[:, :1])[:, e]
    return {"gla": (wg, bg, gla_norm.astype(F32).reshape(1, LANES)),
            "hgrn": (lb, hg_norm.astype(F32).reshape(1, LANES))}


def _even_call(z, mode, col, params, s0, seq_len, n_seq, row0, with_state_out):
    heads = 4
    blk0 = row0 // seq_len
    wide = HPS * LANES

    def tok(name):
        c0 = col[name] // HPS
        return pl.BlockSpec((seq_len, wide), lambda s, h: (blk0 + s, c0 + h), pipeline_mode=pl.Buffered(1))

    if mode == "gla":
        wg, bg, gn = params
        in_specs = [tok("gq"), tok("gk"), tok("gv"),
                    pl.BlockSpec((seq_len, LANES), lambda s, h: (blk0 + s, col["lr"])),
                    tok("gr"),
                    pl.BlockSpec((HPS, 2, LANES, LANES), lambda s, h: (h, 0, 0, 0)),
                    pl.BlockSpec((HPS, 2, 1, LANES), lambda s, h: (h, 0, 0, 0)),
                    pl.BlockSpec((1, LANES), lambda s, h: (0, 0))]
        args = [z, z, z, z, z, wg, bg, gn]
    else:
        lb, gn = params
        in_specs = [tok("hq"), tok("hf_f"), tok("hf_b"), tok("hi"), tok("hgate"),
                    pl.BlockSpec((2, wide), lambda s, h: (0, h)),
                    pl.BlockSpec((1, LANES), lambda s, h: (0, 0))]
        args = [z, z, z, z, z, lb, gn]
    st_spec = pl.BlockSpec((1, 2, HPS, LANES, LANES), lambda s, h: (s, 0, h, 0, 0))
    has_s0 = s0 is not None
    if has_s0:
        in_specs.append(st_spec)
        args.append(s0)
    out_specs = [pl.BlockSpec((seq_len, wide), lambda s, h: (s, h))]
    out_shape = [jax.ShapeDtypeStruct((n_seq * seq_len, heads * LANES), BF16)]
    if with_state_out:
        out_specs.append(st_spec)
        out_shape.append(jax.ShapeDtypeStruct((n_seq, 2, heads, LANES, LANES), F32))
    res = pl.pallas_call(
        functools.partial(_even_kernel, mode=mode, seq_len=seq_len,
                          with_state_out=with_state_out, has_s0=has_s0),
        grid=(n_seq, heads // HPS),
        in_specs=in_specs,
        out_specs=out_specs,
        out_shape=out_shape,
        scratch_shapes=[pltpu.VMEM((seq_len, wide), F32), pltpu.VMEM((seq_len, wide), F32)],
        compiler_params=_cparams(("parallel", "parallel")),
        name=f"mix_{mode}_{seq_len}",
    )(*args)
    return res


CONV_PAD = SUBLANES


def _conv_kernel(x_ref, w_ref, o_ref, pad_scr, *, seq_len):
    cb = pl.program_id(1)
    wide = DN_HEADS * LANES
    zeros = jnp.zeros((CONV_PAD, wide), F32)
    pad_scr[0:CONV_PAD, :] = zeros
    pad_scr[CONV_PAD + seq_len:2 * CONV_PAD + seq_len, :] = zeros
    pad_scr[CONV_PAD:CONV_PAD + seq_len, :] = x_ref[...]
    is_q = cb == 0
    is_qk = cb < 2
    scale = jnp.where(is_q, DN_DK ** -0.5, 1.0)
    for hh in range(DN_HEADS):
        cols = slice(hh * LANES, (hh + 1) * LANES)
        acc = jnp.zeros((seq_len, LANES), F32)
        for w in range(DN_CONV):
            acc = acc + pad_scr[pl.ds(CONV_PAD + w - DN_CONV // 2, seq_len), cols] * w_ref[w:w + 1, cols]
        y = acc * jax.nn.sigmoid(acc)
        yn = y * lax.rsqrt(jnp.sum(y * y, axis=-1, keepdims=True) + NORM_EPS)
        o_ref[:, cols] = jnp.where(is_qk, yn * scale, y)


def _conv_call(z, col0, w, seq_len, n_seq, row0):
    blk0 = row0 // seq_len
    wide = DN_HEADS * LANES
    c0 = col0 // DN_HEADS
    return pl.pallas_call(
        functools.partial(_conv_kernel, seq_len=seq_len),
        grid=(n_seq, 3),
        in_specs=[pl.BlockSpec((seq_len, wide), lambda s, c: (blk0 + s, c0 + c)),
                  pl.BlockSpec((DN_CONV, wide), lambda s, c: (0, c))],
        out_specs=pl.BlockSpec((seq_len, wide), lambda s, c: (s, c)),
        out_shape=jax.ShapeDtypeStruct((n_seq * seq_len, 3 * wide), F32),
        scratch_shapes=[pltpu.VMEM((seq_len + 2 * CONV_PAD, wide), F32)],
        compiler_params=_cparams(("parallel", "parallel")),
        name=f"dn_conv_{seq_len}",
    )(z, w)


def _softplus(x):
    return jnp.maximum(x, 0.0) + jnp.log1p(jnp.exp(-jnp.abs(x)))


def _dn_chunk(q, k, v, g, beta, s, reverse):
    c = CHUNK
    lower = not reverse
    tri = jnp.where(_tri(c, lower), 1.0, 0.0).astype(BF16)
    gam = _dot_sel(tri, g)
    yield
    ones_c = jnp.ones((c, c), BF16)
    gam_row = _dot_sel(ones_c, jnp.where(_tri(c, reverse), g[:, :c], 0.0))
    incl = _tri(c, lower)
    strict = _tri(c, lower, strict=True)
    diff = gam[:, :c] - gam_row
    dec = jnp.where(incl, jnp.exp(jnp.where(incl, diff, 0.0)), 0.0)
    kb = k * beta
    k16 = k.astype(BF16)
    m = jnp.where(strict, _dot_nt(kb.astype(BF16), k16) * dec, 0.0)
    yield
    eg = jnp.exp(gam)
    x = jnp.concatenate([v * beta, kb * eg], axis=1)
    p = -m
    n_fac = int(np.log2(c))
    for j in range(n_fac):
        x = x + _dot_hi(p, x)
        if j + 1 < n_fac:
            p = _dot_hi(p, p)
        yield
    u, w = x[:, :LANES], x[:, LANES:]
    s16 = s.astype(BF16)
    v_new = u - _dot(w.astype(BF16), s16)
    att = _dot_nt(q.astype(BF16), k16) * dec
    yield
    o = _dot((q * eg).astype(BF16), s16) + _dot(att.astype(BF16), v_new.astype(BF16))
    last = 0 if reverse else c - 1
    gl = gam[last:last + 1, :]
    s_new = s * jnp.exp(gl) + _dot_tn((k * jnp.exp(gl - gam)).astype(BF16), v_new.astype(BF16))
    return o, s_new


def _dn_kernel(*refs, seq_len, with_state_out, has_s0):
    it = iter(refs)
    q_ref, k_ref, v_ref, ab_ref, gate_ref, alog_ref, dtb_ref, gn_ref = (next(it) for _ in range(8))
    s0_ref = next(it) if has_s0 else None
    m_ref = next(it)
    so_ref = next(it) if with_state_out else None
    of_scr, ob_scr = next(it), next(it)
    h0 = pl.program_id(1) * HPS
    n_chunks = seq_len // CHUNK
    srow = lax.broadcasted_iota(jnp.int32, (LANES, LANES), 0)
    streams = [(hh, d) for hh in range(HPS) for d in range(2)]

    def chain(c, hh, d, s):
        rows = pl.ds(pl.multiple_of(c * CHUNK, CHUNK), CHUNK)
        cols = slice(hh * LANES, (hh + 1) * LANES)
        ab = ab_ref[rows, :]
        sel_a = jnp.where(srow == DN_HEADS * d + h0 + hh, 1.0, 0.0).astype(BF16)
        sel_b = jnp.where(srow == 2 * DN_HEADS + DN_HEADS * d + h0 + hh, 1.0, 0.0).astype(BF16)
        a = _dot_x_sel(ab, sel_a)
        beta = jax.nn.sigmoid(_dot_x_sel(ab, sel_b))
        g = -jnp.exp(alog_ref[hh, d]) * _softplus(a + dtb_ref[hh, d])
        yield
        o, s_new = yield from _dn_chunk(q_ref[rows, cols], k_ref[rows, cols], v_ref[rows, cols],
                                        g, beta, s, d == 1)
        return rows, cols, o, s_new

    def body(c, carry):
        res = _lockstep([chain(n_chunks - 1 - c if d else c, hh, d, carry[i])
                         for i, (hh, d) in enumerate(streams)])
        for (hh, d), (rows, cols, o, _) in zip(streams, res):
            (ob_scr if d else of_scr)[rows, cols] = o
        return tuple(r[3] for r in res)

    if has_s0:
        init = tuple(s0_ref[0, d, hh] for hh, d in streams)
    else:
        init = tuple(jnp.zeros((LANES, LANES), F32) for _ in streams)
    fin = lax.fori_loop(0, n_chunks, body, init)
    for hh in range(HPS):
        cols = slice(hh * LANES, (hh + 1) * LANES)
        m_ref[:, cols] = _head_norm_gate(of_scr[:, cols] + ob_scr[:, cols], gn_ref[...],
                                         gate_ref[:, cols]).astype(m_ref.dtype)
    if with_state_out:
        for i, (hh, d) in enumerate(streams):
            so_ref[0, d, hh] = fin[i]


S5_TC = 128
S5_CB = 4
S5_SW = S5_G * S5_P // S5_CB


def _s5_params(lam_re, lam_im, log_dt, b_re, b_im, c_re, c_im):
    gpb = S5_G // S5_CB
    eye = jnp.eye(gpb, dtype=F32)
    n = jnp.arange(1, S5_TC + 1, dtype=F32)[:, None, None]
    bbs, ccs, aps = [], [], []
    for d in range(2):
        lr = jnp.minimum(lam_re[d].astype(F32), -1e-4)
        li = lam_im[d].astype(F32)
        dt = jnp.exp(log_dt[d].astype(F32))[:, None]
        mag = jnp.exp(lr * dt)
        ar, ai = mag * jnp.cos(li * dt), mag * jnp.sin(li * dt)
        den = lr * lr + li * li
        nr = ar - 1.0
        cr = (nr * lr + ai * li) / den
        ci = (ai * lr - nr * li) / den
        bbr = cr[..., None] * b_re - ci[..., None] * b_im
        bbi = cr[..., None] * b_im + ci[..., None] * b_re
        blk = lambda t: jnp.einsum('bgpc,gh->bgchp', t.reshape(S5_CB, gpb, S5_P, S5_GS), eye).reshape(
            S5_CB, gpb * S5_GS, gpb * S5_P)
        bbs.append(jnp.stack([blk(bbr), blk(bbi)]))
        cblk = lambda t: jnp.einsum('bgcp,gh->bgphc', t.reshape(S5_CB, gpb, S5_GS, S5_P), eye).reshape(
            S5_CB, gpb * S5_P, gpb * S5_GS)
        ccs.append(jnp.stack([cblk(c_re[d].astype(F32)), cblk(c_im[d].astype(F32))]))
        pm = jnp.exp(n * (lr * dt)[None])
        pr = (pm * jnp.cos(n * (li * dt)[None])).reshape(S5_TC, -1)
        pi = (pm * jnp.sin(n * (li * dt)[None])).reshape(S5_TC, -1)
        if d == 1:
            pr, pi = pr[::-1], pi[::-1]
        aps.append(jnp.stack([pr, pi]))
    bb = jnp.stack(bbs, axis=2).astype(BF16)
    cc = jnp.stack(ccs, axis=2).astype(BF16)
    apow = jnp.stack(aps, axis=1)
    return bb, cc, apow


def _s5_scan(xr, xi, pr, pi, cr, ci, reverse, loc, loc0):
    tc, w = xr.shape
    g8 = SUBLANES
    ng = tc // g8

    def power(n):
        idx = tc - n if reverse else n - 1
        return pr[idx:idx + 1, :], pi[idx:idx + 1, :]

    def step(xr, xi, s, unit):
        n = xr.shape[0]
        row = lax.broadcasted_iota(jnp.int32, xr.shape, 0)
        period = g8 if unit == 1 else n
        pos = jnp.bitwise_and(row, period - 1) if unit == 1 else row
        keep = (pos < period - s) if reverse else (pos >= s)
        shift = n - s if reverse else s
        a_r, a_i = power(s * unit)
        if unit == 1:
            rot = lambda x: pltpu.roll(x.reshape(n // g8, g8, w), g8 - s if reverse else s, 1).reshape(n, w)
            row8 = lax.broadcasted_iota(jnp.int32, (g8, w), 0)
            keep8 = (row8 < g8 - s) if reverse else (row8 >= s)
            a_r = jnp.concatenate([jnp.where(keep8, a_r, 0.0)] * (n // g8), axis=0)
            a_i = jnp.concatenate([jnp.where(keep8, a_i, 0.0)] * (n // g8), axis=0)
            sr, si = rot(xr), rot(xi)
        else:
            sr = jnp.where(keep, pltpu.roll(xr, shift, 0), 0.0)
            si = jnp.where(keep, pltpu.roll(xi, shift, 0), 0.0)
        return xr + (a_r * sr - a_i * si), xi + (a_r * si + a_i * sr)

    s = 1
    while s < g8:
        xr, xi = step(xr, xi, s, 1)
        s *= 2
    end = 0 if reverse else g8 - 1
    ers, eis = [], []
    for k in range(w // LANES):
        r0, i0 = (loc0 + k) * tc, (loc0 + w // LANES + k) * tc
        loc[r0:r0 + tc, :] = xr[:, k * LANES:(k + 1) * LANES]
        loc[i0:i0 + tc, :] = xi[:, k * LANES:(k + 1) * LANES]
        ers.append(loc[pl.ds(r0 + end, ng, stride=g8), :])
        eis.append(loc[pl.ds(i0 + end, ng, stride=g8), :])
    er = jnp.concatenate(ers, axis=1)
    ei = jnp.concatenate(eis, axis=1)
    s = 1
    while s < ng:
        er, ei = step(er, ei, s, g8)
        s *= 2
    grow = lax.broadcasted_iota(jnp.int32, (ng, w), 0)
    if reverse:
        pin_r = jnp.concatenate([pr[g8 * (g + 1):g8 * (g + 1) + 1] for g in range(ng - 1)] + [jnp.ones((1, w), F32)], 0)
        pin_i = jnp.concatenate([pi[g8 * (g + 1):g8 * (g + 1) + 1] for g in range(ng - 1)] + [jnp.zeros((1, w), F32)], 0)
        prev_r = jnp.where(grow < ng - 1, pltpu.roll(er, ng - 1, 0), 0.0)
        prev_i = jnp.where(grow < ng - 1, pltpu.roll(ei, ng - 1, 0), 0.0)
    else:
        pin_r = jnp.concatenate([jnp.ones((1, w), F32)] + [pr[g8 * g - 1:g8 * g] for g in range(1, ng)], 0)
        pin_i = jnp.concatenate([jnp.zeros((1, w), F32)] + [pi[g8 * g - 1:g8 * g] for g in range(1, ng)], 0)
        prev_r = jnp.where(grow >= 1, pltpu.roll(er, 1, 0), 0.0)
        prev_i = jnp.where(grow >= 1, pltpu.roll(ei, 1, 0), 0.0)
    in_r = prev_r + (pin_r * cr - pin_i * ci)
    in_i = prev_i + (pin_r * ci + pin_i * cr)
    in_r = jnp.concatenate([jnp.broadcast_to(in_r[g:g + 1], (g8, w)) for g in range(ng)], axis=0)
    in_i = jnp.concatenate([jnp.broadcast_to(in_i[g:g + 1], (g8, w)) for g in range(ng)], axis=0)
    p8r = pr[tc - g8:] if reverse else pr[:g8]
    p8i = pi[tc - g8:] if reverse else pi[:g8]
    a_r = jnp.concatenate([p8r] * ng, axis=0)
    a_i = jnp.concatenate([p8i] * ng, axis=0)
    return xr + (a_r * in_r - a_i * in_i), xi + (a_r * in_i + a_i * in_r)


def _s5_kernel(*refs, seq_len, has_h0):
    it = iter(refs)
    u_ref, bb_ref, cc_ref, ap_ref, dsk_ref = (next(it) for _ in range(5))
    h0r_ref, h0i_ref = (next(it), next(it)) if has_h0 else (None, None)
    y_ref, fr_ref, fi_ref = next(it), next(it), next(it)
    yf_scr, yb_scr, loc_scr = next(it), next(it), next(it)
    n_chunks = seq_len // S5_TC

    def run(c, d, carry):
        cr, ci = carry
        rows = pl.ds(pl.multiple_of(c * S5_TC, S5_TC), S5_TC)
        ub = u_ref[rows, :].astype(BF16)
        pr, pi = ap_ref[0, d], ap_ref[1, d]
        xr, xi = _dot(ub, bb_ref[0, 0, d]), _dot(ub, bb_ref[1, 0, d])
        hr, hi = _s5_scan(xr, xi, pr, pi, cr, ci, d == 1, loc_scr, d * 2 * (S5_SW // LANES))
        (yf_scr, yb_scr)[d][rows, :] = (_dot(hr.astype(BF16), cc_ref[0, 0, d])
                                         - _dot(hi.astype(BF16), cc_ref[1, 0, d]))
        last = 0 if d == 1 else S5_TC - 1
        return hr[last:last + 1, :], hi[last:last + 1, :]

    def body(c, carry):
        cf, cb = carry
        return run(c, 0, cf), run(n_chunks - 1 - c, 1, cb)

    if has_h0:
        init = tuple((h0r_ref[0, d:d + 1, :], h0i_ref[0, d:d + 1, :]) for d in range(2))
    else:
        z = jnp.zeros((1, S5_SW), F32)
        init = ((z, z), (z, z))
    fin = lax.fori_loop(0, n_chunks, body, init)
    y_ref[...] = dsk_ref[...] * u_ref[...] + (yf_scr[...] + yb_scr[...])
    for d in range(2):
        fr_ref[0, d:d + 1, :] = fin[d][0]
        fi_ref[0, d:d + 1, :] = fin[d][1]


def _s5_call(z, col0, bb, cc, apow, dsk, h0r, h0i, seq_len, n_seq, row0):
    blk0 = row0 // seq_len
    has_h0 = h0r is not None
    st_spec = pl.BlockSpec((1, 2, S5_SW), lambda s, c: (s, 0, c))
    in_specs = [pl.BlockSpec((seq_len, LANES), lambda s, c: (blk0 + s, col0 + c)),
                pl.BlockSpec((2, 1, 2, LANES, S5_SW), lambda s, c: (0, c, 0, 0, 0)),
                pl.BlockSpec((2, 1, 2, S5_SW, LANES), lambda s, c: (0, c, 0, 0, 0)),
                pl.BlockSpec((2, 2, S5_TC, S5_SW), lambda s, c: (0, 0, 0, c)),
                pl.BlockSpec((1, LANES), lambda s, c: (0, c))]
    args = [z, bb, cc, apow, dsk]
    if has_h0:
        in_specs += [st_spec, st_spec]
        args += [h0r, h0i]
    st_shape = jax.ShapeDtypeStruct((n_seq, 2, S5_G * S5_P), F32)
    return pl.pallas_call(
        functools.partial(_s5_kernel, seq_len=seq_len, has_h0=has_h0),
        grid=(n_seq, S5_CB),
        in_specs=in_specs,
        out_specs=[pl.BlockSpec((seq_len, LANES), lambda s, c: (s, c)), st_spec, st_spec],
        out_shape=[jax.ShapeDtypeStruct((n_seq * seq_len, MIX_W), F32), st_shape, st_shape],
        scratch_shapes=[pltpu.VMEM((seq_len, LANES), F32), pltpu.VMEM((seq_len, LANES), F32),
                        pltpu.VMEM((4 * (S5_SW // LANES) * S5_TC, LANES), F32)],
        compiler_params=_cparams(("parallel", "parallel")),
        name=f"mix_s5_{seq_len}",
    )(*args)


def _glu_kernel(y_ref, w_ref, b_ref, o_ref):
    zz = jax.nn.gelu(y_ref[...])
    o_ref[...] = (zz * jax.nn.sigmoid(_dot(zz.astype(BF16), w_ref[...]) + b_ref[...])).astype(o_ref.dtype)


def _glu_call(y, w, b):
    t = y.shape[0]
    return pl.pallas_call(
        _glu_kernel,
        grid=(t // TM,),
        in_specs=[pl.BlockSpec((TM, MIX_W), lambda i: (i, 0)),
                  pl.BlockSpec((MIX_W, MIX_W), lambda i: (0, 0)),
                  pl.BlockSpec((1, MIX_W), lambda i: (0, 0))],
        out_specs=pl.BlockSpec((TM, MIX_W), lambda i: (i, 0)),
        out_shape=jax.ShapeDtypeStruct((t, MIX_W), BF16),
        compiler_params=_cparams(("parallel",)),
        name="s5_glu",
    )(y, w, b)


PK_HP = 2 * PK_HEADS
PK_SIDE = PK_KEYS
PK_PICKS = PK_HEADS * PK_TOPK
TK_TM = 128
TK_HG = 2
WB_TM = 128
WB_GRP = 16
WB_STRIDE = PK_SIDE + SUBLANES
PM_TM = 1024
PM_SLABS = 8


def _score_kernel(h_ref, wq_ref, keys_ref, s_ref):
    q = _dot(h_ref[...], wq_ref[...])
    for hp in range(PK_HP):
        s_ref[hp] = _dot_nt(keys_ref[hp], q[:, hp * LANES:(hp + 1) * LANES].astype(BF16))


def _score_call(h, wq, keys):
    return pl.pallas_call(
        _score_kernel,
        grid=(h.shape[0] // TM,),
        in_specs=[pl.BlockSpec((TM, D_MODEL), lambda i: (i, 0)),
                  pl.BlockSpec((D_MODEL, PK_HP * LANES), lambda i: (0, 0)),
                  pl.BlockSpec((PK_HP, PK_KEYS, LANES), lambda i: (0, 0, 0))],
        out_specs=pl.BlockSpec((PK_HP, PK_KEYS, TM), lambda i: (0, 0, i)),
        out_shape=jax.ShapeDtypeStruct((PK_HP, PK_KEYS, h.shape[0]), F32),
        compiler_params=_cparams(("parallel",)),
        name="peer_scores",
    )(h, wq, keys)


def _top16_sorted(s):
    n_grp = s.shape[0] // SUBLANES
    n = s.shape[1]
    base = lax.broadcasted_iota(jnp.int32, (SUBLANES, n), 0).astype(F32)
    vals = [s[SUBLANES * g:SUBLANES * (g + 1), :] for g in range(n_grp)]
    keys = [base + float(SUBLANES * g) for g in range(n_grp)]
    for rnd in range(n_grp):
        for k in range(rnd % 2, n_grp - 1, 2):
            a, b, ka, kb = vals[k], vals[k + 1], keys[k], keys[k + 1]
            swap = b > a
            vals[k], vals[k + 1] = jnp.maximum(a, b), jnp.minimum(a, b)
            keys[k], keys[k + 1] = jnp.where(swap, kb, ka), jnp.where(swap, ka, kb)
    rank = lax.broadcasted_iota(jnp.int32, (PK_TOPK, n), 0)
    top_v = jnp.zeros((PK_TOPK, n), F32)
    top_k = jnp.zeros((PK_TOPK, n), F32)
    for r in range(PK_TOPK):
        m = jnp.max(vals[0], axis=0, keepdims=True)
        am = jnp.min(jnp.where(vals[0] == m, keys[0], float(s.shape[0])), axis=0, keepdims=True)
        top_v = jnp.where(rank == r, m, top_v)
        top_k = jnp.where(rank == r, am, top_k)
        hit = keys[0] == am
        for k in range(min(n_grp - 1, PK_TOPK - 1 - r)):
            vals[k] = jnp.where(hit, vals[k + 1], vals[k])
            keys[k] = jnp.where(hit, keys[k + 1], keys[k])
    return top_v, top_k


def _top16_rows(tiles):
    shape = tiles[0].shape
    row = lax.broadcasted_iota(jnp.int32, shape, 0).astype(F32)
    out_row = lax.broadcasted_iota(jnp.int32, (PK_TOPK, shape[1]), 0)

    def take(r, s, vals, idxs):
        m = jnp.max(s, axis=0, keepdims=True)
        am = jnp.min(jnp.where(s == m, row, float(shape[0])), axis=0, keepdims=True)
        at = out_row == r
        return jnp.where(row == am, -jnp.inf, s), jnp.where(at, m, vals), jnp.where(at, am, idxs)

    def body(r, carry):
        out = ()
        for i in range(len(tiles)):
            out += take(r, *carry[3 * i:3 * i + 3])
        return out

    zero = jnp.zeros((PK_TOPK, shape[1]), F32)
    init = ()
    for s in tiles:
        init += (s, zero, zero)
    res = lax.fori_loop(0, PK_TOPK, body, init)
    return [(res[3 * i + 1], res[3 * i + 2]) for i in range(len(tiles))]


def _top16_pairs(v1, a1, v2, a2):
    half = PK_TOPK // 2
    n = v1.shape[1]
    rank = lax.broadcasted_iota(jnp.int32, (PK_TOPK, n), 0)
    rank_f = rank.astype(F32)
    vals = [v1 + jnp.broadcast_to(v2[0:1], (PK_TOPK, n))]
    vals += [v1[:half] + jnp.broadcast_to(v2[b:b + 1], (half, n)) for b in range(1, PK_TOPK)]
    ids = [rank_f * PK_TOPK] + [rank_f[:half] * PK_TOPK + float(b) for b in range(1, PK_TOPK)]
    best = jnp.zeros((PK_TOPK, n), F32)
    p1 = jnp.zeros((PK_TOPK, n), F32)
    p2 = jnp.zeros((PK_TOPK, n), F32)
    for r in range(PK_TOPK):
        m = jnp.max(vals[0], axis=0, keepdims=True)
        pos = jnp.min(jnp.where(vals[0] == m, ids[0], float(PK_TOPK * PK_TOPK)), axis=0, keepdims=True)
        hit = ids[0] == pos
        a_sel = jnp.max(jnp.where(hit, rank_f, -1.0), axis=0, keepdims=True)
        e1 = jnp.max(jnp.where(hit, a1, -1.0), axis=0, keepdims=True)
        e2 = jnp.max(jnp.where(rank_f == pos - a_sel * PK_TOPK, a2, -1.0), axis=0, keepdims=True)
        best = jnp.where(rank == r, m, best)
        p1 = jnp.where(rank == r, e1, p1)
        p2 = jnp.where(rank == r, e2, p2)
        if r + 1 < PK_TOPK:
            hit_lo = hit[:half]
            vals[0] = jnp.concatenate([jnp.where(hit_lo, vals[1], vals[0][:half]),
                                       jnp.where(hit[half:], -jnp.inf, vals[0][half:])], axis=0)
            ids[0] = jnp.concatenate([jnp.where(hit_lo, ids[1], ids[0][:half]), ids[0][half:]], axis=0)
            for k in range(1, PK_TOPK - 1 - r):
                vals[k] = jnp.where(hit_lo, vals[k + 1], vals[k])
                ids[k] = jnp.where(hit_lo, ids[k + 1], ids[k])
    return best, p1, p2


def _topk_kernel(s_ref, i1_ref, i2_ref, g_ref, p1_scr, p2_scr, g_scr):
    def heads(hg, carry):
        tops = [_top16_sorted(s_ref[2 * TK_HG * hg + i]) for i in range(2 * TK_HG)]
        for j in range(TK_HG):
            (v1, a1), (v2, a2) = tops[2 * j], tops[2 * j + 1]
            best, p1, p2 = _top16_pairs(v1, a1, v2, a2)
            ex = jnp.exp(best - jnp.max(best, axis=0, keepdims=True))
            rows = pl.ds(pl.multiple_of((hg * TK_HG + j) * PK_TOPK, PK_TOPK), PK_TOPK)
            p1_scr[rows, :] = p1
            p2_scr[rows, :] = p2
            g_scr[rows, :] = ex / jnp.sum(ex, axis=0, keepdims=True)
        return carry

    lax.fori_loop(0, PK_HEADS // TK_HG, heads, 0)
    i1_ref[...] = p1_scr[...].T
    i2_ref[...] = p2_scr[...].T
    g_ref[...] = g_scr[...].T


def _topk_call(s):
    t = s.shape[2]
    spec = pl.BlockSpec((TK_TM, PK_PICKS), lambda i: (i, 0))
    shp = jax.ShapeDtypeStruct((t, PK_PICKS), F32)
    return pl.pallas_call(
        _topk_kernel,
        grid=(t // TK_TM,),
        in_specs=[pl.BlockSpec((PK_HP, PK_KEYS, TK_TM), lambda i: (0, 0, i))],
        out_specs=[spec, spec, spec],
        out_shape=[shp, shp, shp],
        scratch_shapes=[pltpu.VMEM((PK_PICKS, TK_TM), F32)] * 3,
        compiler_params=_cparams(("parallel",)),
        name="peer_topk",
    )(s)


def _wbuild_kernel(i1_ref, i2_ref, g_ref, w_ref, stage_a, stage_b):
    sub = lax.broadcasted_iota(jnp.int32, (PK_SIDE, PK_PICKS), 0).astype(F32).astype(BF16)
    one = jnp.ones((PK_SIDE, PK_PICKS), BF16)
    zero = jnp.zeros((PK_SIDE, PK_PICKS), BF16)

    def grids(t0, stage):
        for tt in range(WB_GRP):
            r1 = i1_ref[pl.ds(t0 + tt, 1), :].astype(BF16)
            r2 = i2_ref[pl.ds(t0 + tt, 1), :].astype(BF16)
            rg = g_ref[pl.ds(t0 + tt, 1), :].astype(BF16)
            p1t = jnp.where(sub == jnp.broadcast_to(r1, sub.shape), one, zero)
            q2t = jnp.where(sub == jnp.broadcast_to(r2, sub.shape), jnp.broadcast_to(rg, sub.shape), zero)
            stage[tt * WB_STRIDE:tt * WB_STRIDE + PK_SIDE, :] = _dot_nt(p1t, q2t)

    def regroup(t0, stage):
        for i1 in range(PK_SIDE):
            lo = stage[pl.ds(i1, SUBLANES, stride=WB_STRIDE), :]
            hi = stage[pl.ds(i1 + SUBLANES * WB_STRIDE, SUBLANES, stride=WB_STRIDE), :]
            w_ref[0, i1, pl.ds(t0, WB_GRP), :] = jnp.concatenate([lo, hi], axis=0).astype(BF16)

    def pair(gi, carry):
        t_a = pl.multiple_of(gi * 2 * WB_GRP, WB_GRP)
        t_b = pl.multiple_of(gi * 2 * WB_GRP + WB_GRP, WB_GRP)
        grids(t_a, stage_a)
        grids(t_b, stage_b)
        regroup(t_a, stage_a)
        regroup(t_b, stage_b)
        return carry

    lax.fori_loop(0, WB_TM // (2 * WB_GRP), pair, 0)


def _wbuild_call(i1, i2, g):
    spec = pl.BlockSpec((WB_TM, PK_PICKS), lambda i: (i, 0))
    return pl.pallas_call(
        _wbuild_kernel,
        grid=(i1.shape[0] // WB_TM,),
        in_specs=[spec, spec, spec],
        out_specs=pl.BlockSpec((1, PK_SIDE, WB_TM, PK_SIDE), lambda i: (i, 0, 0, 0)),
        out_shape=jax.ShapeDtypeStruct((i1.shape[0] // WB_TM, PK_SIDE, WB_TM, PK_SIDE), BF16),
        scratch_shapes=[pltpu.VMEM((WB_GRP * WB_STRIDE, PK_SIDE), F32)] * 2,
        compiler_params=_cparams(("parallel",)),
        name="peer_route",
    )(i1, i2, g)


def _peer_kernel(h_ref, u_ref, v_ref, w_ref, x_ref, g2_ref, o_ref, acc):
    j = pl.program_id(1)

    @pl.when(j == 0)
    def _():
        acc[...] = jnp.zeros_like(acc)

    a = _dot_nt(h_ref[...], u_ref[0].astype(BF16))
    nsub = PM_TM // WB_TM
    w = jnp.concatenate(
        [jnp.concatenate([w_ref[n, sl] for sl in range(PM_SLABS)], axis=1) for n in range(nsub)], axis=0)
    acc[...] += _dot((jax.nn.gelu(a) * w.astype(F32)).astype(BF16), v_ref[0].astype(BF16))

    @pl.when(j == pl.num_programs(1) - 1)
    def _():
        o_ref[...] = x_ref[...] + g2_ref[0] * acc[...]


def _peer_call(h16, u_tabs, v_tabs, layer, w, x, mod3):
    ne = PM_SLABS * PK_SIDE
    tok = pl.BlockSpec((PM_TM, D_MODEL), lambda i, j: (i, 0))
    return pl.pallas_call(
        _peer_kernel,
        grid=(x.shape[0] // PM_TM, PK_SIDE // PM_SLABS),
        in_specs=[tok,
                  pl.BlockSpec((1, ne, D_MODEL), lambda i, j: (layer, j, 0)),
                  pl.BlockSpec((1, ne, D_MODEL), lambda i, j: (layer, j, 0)),
                  pl.BlockSpec((PM_TM // WB_TM, PM_SLABS, WB_TM, PK_SIDE), lambda i, j: (i, j, 0, 0)),
                  tok,
                  pl.BlockSpec((1, 1, D_MODEL), lambda i, j: (_cond_row(i, PM_TM) * 6 + 5, 0, 0))],
        out_specs=tok,
        out_shape=jax.ShapeDtypeStruct(x.shape, F32),
        scratch_shapes=[pltpu.VMEM((PM_TM, D_MODEL), F32)],
        compiler_params=_cparams(("parallel", "arbitrary")),
        name="peer_experts",
    )(h16, u_tabs, v_tabs, w, x, mod3)


def _final_norm_kernel(x_ref, g_ref, o_ref):
    x = x_ref[...]
    o_ref[...] = x * lax.rsqrt(jnp.mean(x * x, axis=-1, keepdims=True) + NORM_EPS) * g_ref[...]


def _final_norm_call(x, g):
    spec = pl.BlockSpec((TM, D_MODEL), lambda i: (i, 0))
    return pl.pallas_call(
        _final_norm_kernel,
        grid=(x.shape[0] // TM,),
        in_specs=[spec, pl.BlockSpec((1, D_MODEL), lambda i: (0, 0))],
        out_specs=spec,
        out_shape=jax.ShapeDtypeStruct(x.shape, F32),
        compiler_params=_cparams(("parallel",)),
        name="final_norm",
    )(x, g)


def _dn_params(conv_w, a_log, dt_bias, dn_norm):
    bc = lambda p: jnp.broadcast_to(p.astype(F32).T[:, :, None, None], (DN_HEADS, 2, 1, LANES))
    return {"conv": conv_w.astype(F32), "alog": bc(a_log), "dtb": bc(dt_bias),
            "gn": dn_norm.astype(F32).reshape(1, LANES)}


def _dn_call(z, qkv, col, alog, dtb, gn, s0, seq_len, n_seq, row0, with_state_out):
    blk0 = row0 // seq_len
    hd = DN_HEADS
    nhb = hd // HPS
    wide = HPS * LANES
    gate0 = col["gate"] // HPS
    one = pl.Buffered(1)
    in_specs = [pl.BlockSpec((seq_len, wide), lambda s, h: (s, h), pipeline_mode=one),
                pl.BlockSpec((seq_len, wide), lambda s, h: (s, nhb + h), pipeline_mode=one),
                pl.BlockSpec((seq_len, wide), lambda s, h: (s, 2 * nhb + h), pipeline_mode=one),
                pl.BlockSpec((seq_len, LANES), lambda s, h: (blk0 + s, col["ab"])),
                pl.BlockSpec((seq_len, wide), lambda s, h: (blk0 + s, gate0 + h), pipeline_mode=one),
                pl.BlockSpec((HPS, 2, 1, LANES), lambda s, h: (h, 0, 0, 0)),
                pl.BlockSpec((HPS, 2, 1, LANES), lambda s, h: (h, 0, 0, 0)),
                pl.BlockSpec((1, LANES), lambda s, h: (0, 0))]
    args = [qkv, qkv, qkv, z, z, alog, dtb, gn]
    st_spec = pl.BlockSpec((1, 2, HPS, LANES, LANES), lambda s, h: (s, 0, h, 0, 0))
    has_s0 = s0 is not None
    if has_s0:
        in_specs.append(st_spec)
        args.append(s0)
    out_specs = [pl.BlockSpec((seq_len, wide), lambda s, h: (s, h))]
    out_shape = [jax.ShapeDtypeStruct((n_seq * seq_len, hd * LANES), BF16)]
    if with_state_out:
        out_specs.append(st_spec)
        out_shape.append(jax.ShapeDtypeStruct((n_seq, 2, hd, LANES, LANES), F32))
    return pl.pallas_call(
        functools.partial(_dn_kernel, seq_len=seq_len, with_state_out=with_state_out, has_s0=has_s0),
        grid=(n_seq, nhb),
        in_specs=in_specs,
        out_specs=out_specs,
        out_shape=out_shape,
        scratch_shapes=[pltpu.VMEM((seq_len, wide), F32), pltpu.VMEM((seq_len, wide), F32)],
        compiler_params=_cparams(("parallel", "parallel")),
        name=f"mix_dn_{seq_len}",
    )(*args)


def _grid_col_major(t):
    n, d = t.shape
    rows = DEC_SEQ // GRID_W
    return t.reshape(DEC_BATCH, rows, GRID_W, d).transpose(0, 2, 1, 3).reshape(n, d)


def _grid_row_major(t):
    n, d = t.shape
    rows = DEC_SEQ // GRID_W
    return t.reshape(DEC_BATCH, GRID_W, rows, d).transpose(0, 2, 1, 3).reshape(n, d)


def _both_groups(fn, s0_sample):
    res_p = fn(None, SEQ, BATCH, 0, True)
    res_s = fn(s0_sample, DEC_SEQ, DEC_BATCH, T_PROMPT, False)
    return (res_p[0], res_s[0]), res_p[1]


def kernel(x_prompt, x_sample, state_gla, state_hgrn, state_s5_re, state_s5_im, state_dn, c, c_ctx, w_ada, b_ada, norm1, norm2, w_in_even, w_gla_gate, b_gla_gate, gla_norm, hg_lb, hg_norm, w_in_odd, s5_lam_re, s5_lam_im, s5_log_dt, s5_b_re, s5_b_im, s5_c_re, s5_c_im, s5_d, s5_w_glu, s5_b_glu, dn_conv, dn_a_log, dn_dt_bias, dn_norm, w_out, pk_w_q, pk_keys, pk_u, pk_v, final_norm):
    x = jnp.concatenate([x_prompt.reshape(T_PROMPT, D_MODEL), x_sample.reshape(T_SAMPLE, D_MODEL)], axis=0)
    cond = jnp.zeros((N_COND, D_MODEL), F32).at[0].set(c_ctx.astype(F32)).at[1:1 + DEC_BATCH].set(c.astype(F32))
    mod = _ada_call(cond, w_ada.astype(F32), b_ada.astype(F32))

    out_g, out_h, out_re, out_im, out_dn = [], [], [], [], []
    for l in range(DEPTH):
        mod3 = mod[l].reshape(N_COND * 6, 1, D_MODEL)
        g1 = norm1[l].astype(F32).reshape(1, D_MODEL)
        if l % 2 == 0:
            e = l // 2
            w_in = _relayout_cols(w_in_even[e], EVEN_IDX).astype(BF16)
            z = _in_call(x, None, mod3, g1, w_in, 2)
            prm = _even_params(w_gla_gate[e], b_gla_gate[e], gla_norm[e], hg_lb, hg_norm[e], e)
            s0g = jnp.pad(state_gla[:, e].astype(F32), ((0, 0),) * 3 + ((0, HG_DK - GLA_DK), (0, 0)))
            m_g, st_g = _both_groups(
                lambda s0, L, n, r0, so: _even_call(z, "gla", EVEN_COL, prm["gla"], s0, L, n, r0, so), s0g)
            m_h, st_h = _both_groups(
                lambda s0, L, n, r0, so: _even_call(z, "hgrn", EVEN_COL, prm["hgrn"], s0, L, n, r0, so),
                state_hgrn[:, e].astype(F32))
            m_a, m_b = m_g, m_h
            out_g.append(st_g[..., :GLA_DK, :])
            out_h.append(st_h)
        else:
            j = l // 2
            w_in = _relayout_cols(w_in_odd[j], ODD_IDX).astype(BF16)
            z = _in_call(x, _grid_col_major(x[T_PROMPT:]), mod3, g1, w_in, 3)
            bb, cc, apow = _s5_params(s5_lam_re[j], s5_lam_im[j], s5_log_dt[j], s5_b_re[j], s5_b_im[j],
                                      s5_c_re[j], s5_c_im[j])
            dsk = s5_d[j].astype(F32).reshape(1, MIX_W)
            y_p, fr, fi = _s5_call(z, ODD_COL["u"], bb, cc, apow, dsk, None, None, SEQ, BATCH, 0)
            y_s, _, _ = _s5_call(z, ODD_COL["u"], bb, cc, apow, dsk,
                                 state_s5_re[:, j].astype(F32).reshape(DEC_BATCH, 2, -1),
                                 state_s5_im[:, j].astype(F32).reshape(DEC_BATCH, 2, -1),
                                 DEC_SEQ, DEC_BATCH, T_PROMPT)
            w_glu, b_glu = s5_w_glu[j].astype(BF16), s5_b_glu[j].astype(F32).reshape(1, MIX_W)
            m_s5 = (_glu_call(y_p, w_glu, b_glu), _glu_call(y_s, w_glu, b_glu))
            dp = _dn_params(dn_conv[j], dn_a_log[j], dn_dt_bias[j], dn_norm[j])

            def dn(s0, L, n, r0, so):
                qkv = _conv_call(z, ODD_COL["qkv"], dp["conv"], L, n, r0)
                return _dn_call(z, qkv, ODD_COL, dp["alog"], dp["dtb"], dp["gn"], s0, L, n, r0, so)

            m_dn, st_dn = _both_groups(dn, state_dn[:, j].astype(F32))
            m_a = (m_s5[0], _grid_row_major(m_s5[1]))
            m_b = (m_dn[0], _grid_row_major(m_dn[1]))
            out_re.append(fr.reshape(BATCH, 2, S5_G, S5_P))
            out_im.append(fi.reshape(BATCH, 2, S5_G, S5_P))
            out_dn.append(st_dn)
        x, h2 = _out_call(x, m_a, m_b, w_out[l].astype(BF16), mod3, norm2[l].astype(F32).reshape(1, D_MODEL))
        s = _score_call(h2, pk_w_q[l].astype(BF16), pk_keys[l].reshape(PK_HP, PK_KEYS, PK_DQ // 2).astype(BF16))
        i1, i2, gate = _topk_call(s)
        w = _wbuild_call(i1, i2, gate)
        x = _peer_call(h2, pk_u, pk_v, l, w, x, mod3)
    y = _final_norm_call(x, final_norm.astype(F32).reshape(1, D_MODEL))
    return (y[:T_PROMPT].reshape(BATCH, SEQ, D_MODEL), y[T_PROMPT:].reshape(DEC_BATCH, DEC_SEQ, D_MODEL),
            jnp.stack(out_g, axis=1), jnp.stack(out_h, axis=1), jnp.stack(out_re, axis=1),
            jnp.stack(out_im, axis=1), jnp.stack(out_dn, axis=1))
```

```python
import functools

import numpy as np
import jax
import jax.numpy as jnp
from jax import lax
from jax.experimental import pallas as pl
from jax.experimental.pallas import tpu as pltpu

F32 = jnp.float32
BF16 = jnp.bfloat16

D_MODEL = 1024
BATCH = 32
SEQ = 256
DEPTH = 4
DEC_BATCH = 8
DEC_SEQ = 2048
GRID_W = 64
N_EVEN = 2
N_ODD = 2
MIX_W = 512
GLA_HEADS = 4
GLA_DK = 64
GLA_DV = 128
GLA_RANK = 16
GLA_TAU = 16.0
HG_HEADS = 4
HG_DK = 128
HG_DV = 128
S5_GS = 16
S5_G = 32
S5_P = 64
DN_HEADS = 4
DN_DK = 128
DN_DV = 128
DN_CONV = 5
PK_HEADS = 8
PK_KEYS = 128
PK_DQ = 256
PK_TOPK = 16
NORM_EPS = 1e-6
GATE_FLOOR = 1e-30

T_PROMPT = BATCH * SEQ
T_SAMPLE = DEC_BATCH * DEC_SEQ
T_ALL = T_PROMPT + T_SAMPLE
N_COND = 16

LANES = 128
SUBLANES = 8
VMEM_LIMIT = 56 * 1024 * 1024

TM = 512


def _layout(pieces):
    idx, col = [], {}
    for name, src, width in pieces:
        assert len(idx) % LANES == 0 and width % LANES == 0
        col[name] = len(idx) // LANES
        src = list(src)
        idx += src + [-1] * (width - len(src))
    return np.asarray(idx, np.int32), col


def _even_layout():
    o = np.cumsum((0, 256, 256, 512, 16, 16, 512, 512, 512, 512, 512, 512))
    rng = lambda i: range(o[i], o[i + 1])
    pieces = []
    for nm, base in (("gq", o[0]), ("gk", o[1])):
        for h in range(GLA_HEADS):
            pieces.append((nm if h == 0 else f"{nm}{h}", range(base + 64 * h, base + 64 * h + 64), LANES))
    pieces += [("gv", rng(2), 512), ("gr", rng(5), 512),
               ("hq", rng(6), 512), ("hf_f", rng(7), 512), ("hf_b", rng(8), 512),
               ("hi", rng(9), 512), ("hgate", rng(10), 512),
               ("lr", list(rng(3)) + list(rng(4)), LANES), ("pad", [], LANES)]
    return _layout(pieces)


def _odd_layout():
    o = np.cumsum((0, 512, 1536, 4, 4, 4, 4, 512))
    rng = lambda i: range(o[i], o[i + 1])
    return _layout([("u", rng(0), 512), ("qkv", rng(1), 1536), ("gate", rng(6), 512),
                    ("ab", range(o[2], o[6]), LANES)])


EVEN_IDX, EVEN_COL = _even_layout()
ODD_IDX, ODD_COL = _odd_layout()


def _relayout_cols(w, idx):
    runs = []
    for v in (int(v) for v in idx):
        grows = runs and ((v < 0 and runs[-1][0] < 0) or (v >= 0 and runs[-1][0] >= 0 and runs[-1][0] + runs[-1][1] == v))
        if grows:
            runs[-1][1] += 1
        else:
            runs.append([max(v, -1), 1])
    pieces = [jnp.zeros(w.shape[:-1] + (n,), w.dtype) if c < 0 else w[..., c:c + n] for c, n in runs]
    return jnp.concatenate(pieces, axis=-1)


def _cparams(sem):
    return pltpu.CompilerParams(dimension_semantics=sem, vmem_limit_bytes=VMEM_LIMIT)


def _cond_row(i, tm=TM):
    n_p = T_PROMPT // tm
    per_seq = DEC_SEQ // tm
    return jnp.where(i < n_p, 0, 1 + (i - n_p) // per_seq)


def _split3(x):
    hi = x.astype(BF16)
    r1 = x - hi.astype(F32)
    mid = r1.astype(BF16)
    lo = (r1 - mid.astype(F32)).astype(BF16)
    return hi, mid, lo


def _dot(a, b):
    return jnp.dot(a, b, preferred_element_type=F32)


def _dot_nt(a, b):
    return lax.dot_general(a, b, (((1,), (1,)), ((), ())), preferred_element_type=F32)


def _dot_tn(a, b):
    return lax.dot_general(a, b, (((0,), (0,)), ((), ())), preferred_element_type=F32)


def _dot_sel(sel_bf16, x):
    hi, mid, lo = _split3(x)
    return _dot(sel_bf16, hi) + _dot(sel_bf16, mid) + _dot(sel_bf16, lo)


def _dot_x_sel(x, sel_bf16):
    hi, mid, lo = _split3(x)
    return _dot(hi, sel_bf16) + _dot(mid, sel_bf16) + _dot(lo, sel_bf16)


def _dot_hi(a, b):
    ah = a.astype(BF16)
    al = (a - ah.astype(F32)).astype(BF16)
    bh = b.astype(BF16)
    bl = (b - bh.astype(F32)).astype(BF16)
    return _dot(ah, bh) + (_dot(ah, bl) + _dot(al, bh))


def _ada_kernel(c_ref, w_ref, b_ref, o_ref):
    c = c_ref[...]
    cs = c * jax.nn.sigmoid(c)
    o_ref[0] = _dot(cs, w_ref[0]) + b_ref[0]


def _ada_call(cond, w_ada, b_ada):
    nt = 6
    return pl.pallas_call(
        _ada_kernel,
        grid=(DEPTH, nt),
        in_specs=[
            pl.BlockSpec((N_COND, D_MODEL), lambda l, j: (0, 0)),
            pl.BlockSpec((1, D_MODEL, D_MODEL), lambda l, j: (l, 0, j)),
            pl.BlockSpec((1, 1, D_MODEL), lambda l, j: (l, 0, j)),
        ],
        out_specs=pl.BlockSpec((1, N_COND, D_MODEL), lambda l, j: (l, 0, j)),
        out_shape=jax.ShapeDtypeStruct((DEPTH, N_COND, 6 * D_MODEL), F32),
        compiler_params=_cparams(("parallel", "parallel")),
        name="ada_mod",
    )(cond, w_ada, b_ada.reshape(DEPTH, 1, 6 * D_MODEL))


def _modulated_norm(x, g, sc, sh):
    ms = jnp.mean(x * x, axis=-1, keepdims=True)
    y = x * lax.rsqrt(ms + NORM_EPS) * g
    return y * (1.0 + sc) + sh


N_PROMPT_TILES = T_PROMPT // TM


def _prompt_tile(i):
    return jnp.minimum(i, N_PROMPT_TILES - 1)


def _sample_tile(i):
    return jnp.maximum(i - N_PROMPT_TILES, 0)


def _in_kernel(*refs, sample_alt):
    if sample_alt:
        x_ref, xs_ref, sh_ref, sc_ref, g_ref, w_ref, o_ref = refs
        x = jnp.where(pl.program_id(1) >= N_PROMPT_TILES, xs_ref[...], x_ref[...])
    else:
        x_ref, sh_ref, sc_ref, g_ref, w_ref, o_ref = refs
        x = x_ref[...]
    h = _modulated_norm(x, g_ref[...], sc_ref[0], sh_ref[0])
    o_ref[...] = _dot(h.astype(BF16), w_ref[...])


def _in_call(x, x_sample_alt, mod3, g, w, n_col_tiles):
    n = w.shape[1]
    tn = n // n_col_tiles
    sample_alt = x_sample_alt is not None
    if sample_alt:
        tok_specs = [pl.BlockSpec((TM, D_MODEL), lambda j, i: (_prompt_tile(i), 0)),
                     pl.BlockSpec((TM, D_MODEL), lambda j, i: (_sample_tile(i), 0))]
        toks = [x, x_sample_alt]
    else:
        tok_specs = [pl.BlockSpec((TM, D_MODEL), lambda j, i: (i, 0))]
        toks = [x]
    return pl.pallas_call(
        functools.partial(_in_kernel, sample_alt=sample_alt),
        grid=(n_col_tiles, x.shape[0] // TM),
        in_specs=tok_specs + [
            pl.BlockSpec((1, 1, D_MODEL), lambda j, i: (_cond_row(i) * 6 + 0, 0, 0)),
            pl.BlockSpec((1, 1, D_MODEL), lambda j, i: (_cond_row(i) * 6 + 1, 0, 0)),
            pl.BlockSpec((1, D_MODEL), lambda j, i: (0, 0)),
            pl.BlockSpec((D_MODEL, tn), lambda j, i: (0, j)),
        ],
        out_specs=pl.BlockSpec((TM, tn), lambda j, i: (i, j)),
        out_shape=jax.ShapeDtypeStruct((x.shape[0], n), F32),
        compiler_params=_cparams(("parallel", "parallel")),
        name="in_proj",
    )(*toks, mod3, mod3, g, w)


def _out_kernel(x_ref, ap_ref, as_ref, bp_ref, bs_ref, wa_ref, wb_ref, g1_ref, sh_ref, sc_ref, g_ref,
                xo_ref, h_ref):
    is_sample = pl.program_id(0) >= N_PROMPT_TILES
    m_a = jnp.where(is_sample, as_ref[...], ap_ref[...])
    m_b = jnp.where(is_sample, bs_ref[...], bp_ref[...])
    x = x_ref[...] + g1_ref[0] * (_dot(m_a, wa_ref[...]) + _dot(m_b, wb_ref[...]))
    xo_ref[...] = x
    h_ref[...] = _modulated_norm(x, g_ref[...], sc_ref[0], sh_ref[0]).astype(h_ref.dtype)


def _out_call(x, m_a, m_b, w_out, mod3, g2):
    spec_tok = pl.BlockSpec((TM, D_MODEL), lambda i: (i, 0))
    spec_p = pl.BlockSpec((TM, MIX_W), lambda i: (_prompt_tile(i), 0))
    spec_s = pl.BlockSpec((TM, MIX_W), lambda i: (_sample_tile(i), 0))
    spec_w = pl.BlockSpec((MIX_W, D_MODEL), lambda i: (0, 0))

    def mod_spec(k):
        return pl.BlockSpec((1, 1, D_MODEL), lambda i: (_cond_row(i) * 6 + k, 0, 0))

    return pl.pallas_call(
        _out_kernel,
        grid=(x.shape[0] // TM,),
        in_specs=[
            spec_tok, spec_p, spec_s, spec_p, spec_s, spec_w, spec_w,
            mod_spec(2), mod_spec(3), mod_spec(4),
            pl.BlockSpec((1, D_MODEL), lambda i: (0, 0)),
        ],
        out_specs=[spec_tok, spec_tok],
        out_shape=[jax.ShapeDtypeStruct(x.shape, F32), jax.ShapeDtypeStruct(x.shape, BF16)],
        compiler_params=_cparams(("parallel",)),
        name="out_proj",
    )(x, m_a[0], m_a[1], m_b[0], m_b[1], w_out[:MIX_W], w_out[MIX_W:], mod3, mod3, mod3, g2)


CHUNK = 64
SUB = 16
NSUB = CHUNK // SUB
NEG_BIG = -1e30
HPS = 4


def _tri(n, lower, strict=False):
    r = lax.broadcasted_iota(jnp.int32, (n, n), 0)
    c = lax.broadcasted_iota(jnp.int32, (n, n), 1)
    if lower:
        return (c < r) if strict else (c <= r)
    return (c > r) if strict else (c >= r)


def _lockstep(gens):
    results = [None] * len(gens)
    active = list(enumerate(gens))
    while active:
        still = []
        for i, g in active:
            try:
                next(g)
                still.append((i, g))
            except StopIteration as stop:
                results[i] = stop.value
        active = still
    return results


def _gla_chunk(q, k, v, la, st, reverse):
    kdim = q.shape[1]
    tri = jnp.where(_tri(CHUNK, not reverse), 1.0, 0.0).astype(BF16)
    b = _dot_sel(tri, la)
    yield
    last = 0 if reverse else CHUNK - 1
    b_last = b[last:last + 1, :]
    o_inter = _dot_nt((q * jnp.exp(b)).astype(BF16), st.astype(BF16))
    kd = k * jnp.exp(b_last - b)
    st_new = st * jnp.exp(b_last) + _dot_tn(v.astype(BF16), kd.astype(BF16))
    yield

    ones_k = jnp.ones((kdim, LANES), BF16)
    jrow = lax.broadcasted_iota(jnp.int32, (SUB, kdim), 0)
    rsel = lax.broadcasted_iota(jnp.int32, (SUB, SUB * SUB), 0)
    csel = lax.broadcasted_iota(jnp.int32, (SUB, SUB * SUB), 1)
    in_grp = jnp.logical_and(csel >= rsel * SUB, csel < rsel * SUB + SUB)
    sel = jnp.where(in_grp, 1.0, 0.0).astype(BF16)
    atts, offs = [], []
    for blk in range(NSUB):
        r0 = blk * SUB
        q_i, k_i, b_i = q[r0:r0 + SUB], k[r0:r0 + SUB], b[r0:r0 + SUB]
        rows = []
        for i in range(SUB):
            keep = (jrow >= i) if reverse else (jrow <= i)
            e = jnp.exp(jnp.where(keep, b_i[i:i + 1, :] - b_i, NEG_BIG))
            rows.append((q_i[i:i + 1, :] * k_i) * e)
        p = jnp.concatenate(rows, axis=0)
        atts.append(_dot(p.astype(BF16), ones_k))
        past = slice(r0 + SUB, CHUNK) if reverse else slice(0, r0)
        if past.start < past.stop:
            ref = b[r0 + SUB:r0 + SUB + 1, :] if reverse else b[r0 - 1:r0, :]
            qt = q_i * jnp.exp(b_i - ref)
            kt = k[past] * jnp.exp(ref - b[past])
            offs.append((past, _dot_nt(qt.astype(BF16), kt.astype(BF16))))
        else:
            offs.append(None)
    yield
    outs = []
    for blk in range(NSUB):
        v_i = v[blk * SUB:(blk + 1) * SUB]
        zv = atts[blk] * jnp.concatenate([v_i] * SUB, axis=0)
        o_blk = _dot(sel, zv.astype(BF16))
        if offs[blk] is not None:
            past, a_off = offs[blk]
            o_blk = o_blk + _dot(a_off.astype(BF16), v[past].astype(BF16))
        outs.append(o_blk)
    return o_inter + jnp.concatenate(outs, axis=0), st_new


def _log_sigmoid(x):
    return -(jnp.maximum(-x, 0.0) + jnp.log1p(jnp.exp(-jnp.abs(x))))


def _head_norm_gate(o, g, gate):
    o = o * lax.rsqrt(jnp.mean(o * o, axis=-1, keepdims=True) + NORM_EPS)
    return (o * g) * (gate * jax.nn.sigmoid(gate))


def _even_kernel(*refs, mode, seq_len, with_state_out, has_s0):
    it = iter(refs)
    if mode == "gla":
        q_ref, k_ref, v_ref, lr_ref, gate_ref, wg_ref, bg_ref, gn_ref = (next(it) for _ in range(8))
    else:
        q_ref, ff_ref, fb_ref, v_ref, gate_ref, lb_ref, gn_ref = (next(it) for _ in range(7))
    s0_ref = next(it) if has_s0 else None
    m_ref = next(it)
    so_ref = next(it) if with_state_out else None
    of_scr, ob_scr = next(it), next(it)
    n_chunks = seq_len // CHUNK
    streams = [(hh, d) for hh in range(HPS) for d in range(2)]

    def chain(c, hh, d, st):
        rows = pl.ds(pl.multiple_of(c * CHUNK, CHUNK), CHUNK)
        cols = slice(hh * LANES, (hh + 1) * LANES)
        if mode == "gla":
            q = q_ref[rows, cols] * (GLA_DK ** -0.5)
            k = k_ref[rows, cols]
            pre = _dot_hi(lr_ref[rows, :], wg_ref[hh, d]) + bg_ref[hh, d]
            yield
            la = _log_sigmoid(pre) / GLA_TAU
        else:
            q = q_ref[rows, cols]
            f = (ff_ref, fb_ref)[d][rows, cols]
            lb = lb_ref[d:d + 1, cols]
            gate = lb + (1.0 - lb) * jax.nn.sigmoid(f)
            la = jnp.log(jnp.maximum(gate, GATE_FLOOR))
            k = (1.0 - lb) * jax.nn.sigmoid(-f)
        o, st_new = yield from _gla_chunk(q, k, v_ref[rows, cols], la, st, d == 1)
        return rows, cols, o, st_new

    def body(c, carry):
        res = _lockstep([chain(n_chunks - 1 - c if d else c, hh, d, carry[i])
                         for i, (hh, d) in enumerate(streams)])
        for (hh, d), (rows, cols, o, _) in zip(streams, res):
            (ob_scr if d else of_scr)[rows, cols] = o
        return tuple(r[3] for r in res)

    if has_s0:
        init = tuple(s0_ref[0, d, hh].T for hh, d in streams)
    else:
        init = tuple(jnp.zeros((LANES, LANES), F32) for _ in streams)
    fin = lax.fori_loop(0, n_chunks, body, init)
    for hh in range(HPS):
        cols = slice(hh * LANES, (hh + 1) * LANES)
        m_ref[:, cols] = _head_norm_gate(of_scr[:, cols] + ob_scr[:, cols], gn_ref[...],
                                         gate_ref[:, cols]).astype(m_ref.dtype)
    if with_state_out:
        for i, (hh, d) in enumerate(streams):
            so_ref[0, d, hh] = fin[i].T


def _even_params(w_gate, b_gate, gla_norm, hg_lb, hg_norm, e):
    wg = jnp.zeros((GLA_HEADS, 2, LANES, LANES), F32)
    bg = jnp.zeros((GLA_HEADS, 2, 1, LANES), F32)
    for h in range(GLA_HEADS):
        for d in range(2):
            wg = wg.at[h, d, d * GLA_RANK:(d + 1) * GLA_RANK, :GLA_DK].set(
                w_gate[d, :, h * GLA_DK:(h + 1) * GLA_DK].astype(F32))
            bg = bg.at[h, d, 0, :GLA_DK].set(b_gate[d, h * GLA_DK:(h + 1) * GLA_DK].astype(F32))
    lbp = jax.nn.softmax(hg_lb.astype(F32), axis=1)
    lb = (jnp.cumsum(lbp, axis=1) - lbp[:, :1])[:, e]
    return {"gla": (wg, bg, gla_norm.astype(F32).reshape(1, LANES)),
            "hgrn": (lb, hg_norm.astype(F32).reshape(1, LANES))}


def _even_call(z, mode, col, params, s0, seq_len, n_seq, row0, with_state_out):
    heads = 4
    blk0 = row0 // seq_len
    wide = HPS * LANES

    def tok(name):
        c0 = col[name] // HPS
        return pl.BlockSpec((seq_len, wide), lambda s, h: (blk0 + s, c0 + h), pipeline_mode=pl.Buffered(1))

    if mode == "gla":
        wg, bg, gn = params
        in_specs = [tok("gq"), tok("gk"), tok("gv"),
                    pl.BlockSpec((seq_len, LANES), lambda s, h: (blk0 + s, col["lr"])),
                    tok("gr"),
                    pl.BlockSpec((HPS, 2, LANES, LANES), lambda s, h: (h, 0, 0, 0)),
                    pl.BlockSpec((HPS, 2, 1, LANES), lambda s, h: (h, 0, 0, 0)),
                    pl.BlockSpec((1, LANES), lambda s, h: (0, 0))]
        args = [z, z, z, z, z, wg, bg, gn]
    else:
        lb, gn = params
        in_specs = [tok("hq"), tok("hf_f"), tok("hf_b"), tok("hi"), tok("hgate"),
                    pl.BlockSpec((2, wide), lambda s, h: (0, h)),
                    pl.BlockSpec((1, LANES), lambda s, h: (0, 0))]
        args = [z, z, z, z, z, lb, gn]
    st_spec = pl.BlockSpec((1, 2, HPS, LANES, LANES), lambda s, h: (s, 0, h, 0, 0))
    has_s0 = s0 is not None
    if has_s0:
        in_specs.append(st_spec)
        args.append(s0)
    out_specs = [pl.BlockSpec((seq_len, wide), lambda s, h: (s, h))]
    out_shape = [jax.ShapeDtypeStruct((n_seq * seq_len, heads * LANES), BF16)]
    if with_state_out:
        out_specs.append(st_spec)
        out_shape.append(jax.ShapeDtypeStruct((n_seq, 2, heads, LANES, LANES), F32))
    res = pl.pallas_call(
        functools.partial(_even_kernel, mode=mode, seq_len=seq_len,
                          with_state_out=with_state_out, has_s0=has_s0),
        grid=(n_seq, heads // HPS),
        in_specs=in_specs,
        out_specs=out_specs,
        out_shape=out_shape,
        scratch_shapes=[pltpu.VMEM((seq_len, wide), F32), pltpu.VMEM((seq_len, wide), F32)],
        compiler_params=_cparams(("parallel", "parallel")),
        name=f"mix_{mode}_{seq_len}",
    )(*args)
    return res


CONV_PAD = SUBLANES


def _conv_kernel(x_ref, w_ref, o_ref, pad_scr, *, seq_len):
    cb = pl.program_id(1)
    wide = DN_HEADS * LANES
    zeros = jnp.zeros((CONV_PAD, wide), F32)
    pad_scr[0:CONV_PAD, :] = zeros
    pad_scr[CONV_PAD + seq_len:2 * CONV_PAD + seq_len, :] = zeros
    pad_scr[CONV_PAD:CONV_PAD + seq_len, :] = x_ref[...]
    is_q = cb == 0
    is_qk = cb < 2
    scale = jnp.where(is_q, DN_DK ** -0.5, 1.0)
    for hh in range(DN_HEADS):
        cols = slice(hh * LANES, (hh + 1) * LANES)
        acc = jnp.zeros((seq_len, LANES), F32)
        for w in range(DN_CONV):
            acc = acc + pad_scr[pl.ds(CONV_PAD + w - DN_CONV // 2, seq_len), cols] * w_ref[w:w + 1, cols]
        y = acc * jax.nn.sigmoid(acc)
        yn = y * lax.rsqrt(jnp.sum(y * y, axis=-1, keepdims=True) + NORM_EPS)
        o_ref[:, cols] = jnp.where(is_qk, yn * scale, y)


def _conv_call(z, col0, w, seq_len, n_seq, row0):
    blk0 = row0 // seq_len
    wide = DN_HEADS * LANES
    c0 = col0 // DN_HEADS
    return pl.pallas_call(
        functools.partial(_conv_kernel, seq_len=seq_len),
        grid=(n_seq, 3),
        in_specs=[pl.BlockSpec((seq_len, wide), lambda s, c: (blk0 + s, c0 + c)),
                  pl.BlockSpec((DN_CONV, wide), lambda s, c: (0, c))],
        out_specs=pl.BlockSpec((seq_len, wide), lambda s, c: (s, c)),
        out_shape=jax.ShapeDtypeStruct((n_seq * seq_len, 3 * wide), F32),
        scratch_shapes=[pltpu.VMEM((seq_len + 2 * CONV_PAD, wide), F32)],
        compiler_params=_cparams(("parallel", "parallel")),
        name=f"dn_conv_{seq_len}",
    )(z, w)


def _softplus(x):
    return jnp.maximum(x, 0.0) + jnp.log1p(jnp.exp(-jnp.abs(x)))


def _dn_chunk(q, k, v, g, beta, s, reverse):
    c = CHUNK
    lower = not reverse
    tri = jnp.where(_tri(c, lower), 1.0, 0.0).astype(BF16)
    gam = _dot_sel(tri, g)
    yield
    ones_c = jnp.ones((c, c), BF16)
    gam_row = _dot_sel(ones_c, jnp.where(_tri(c, reverse), g[:, :c], 0.0))
    incl = _tri(c, lower)
    strict = _tri(c, lower, strict=True)
    diff = gam[:, :c] - gam_row
    dec = jnp.where(incl, jnp.exp(jnp.where(incl, diff, 0.0)), 0.0)
    kb = k * beta
    k16 = k.astype(BF16)
    m = jnp.where(strict, _dot_nt(kb.astype(BF16), k16) * dec, 0.0)
    yield
    eg = jnp.exp(gam)
    x = jnp.concatenate([v * beta, kb * eg], axis=1)
    p = -m
    n_fac = int(np.log2(c))
    for j in range(n_fac):
        x = x + _dot_hi(p, x)
        if j + 1 < n_fac:
            p = _dot_hi(p, p)
        yield
    u, w = x[:, :LANES], x[:, LANES:]
    s16 = s.astype(BF16)
    v_new = u - _dot(w.astype(BF16), s16)
    att = _dot_nt(q.astype(BF16), k16) * dec
    yield
    o = _dot((q * eg).astype(BF16), s16) + _dot(att.astype(BF16), v_new.astype(BF16))
    last = 0 if reverse else c - 1
    gl = gam[last:last + 1, :]
    s_new = s * jnp.exp(gl) + _dot_tn((k * jnp.exp(gl - gam)).astype(BF16), v_new.astype(BF16))
    return o, s_new


def _dn_kernel(*refs, seq_len, with_state_out, has_s0):
    it = iter(refs)
    q_ref, k_ref, v_ref, ab_ref, gate_ref, alog_ref, dtb_ref, gn_ref = (next(it) for _ in range(8))
    s0_ref = next(it) if has_s0 else None
    m_ref = next(it)
    so_ref = next(it) if with_state_out else None
    of_scr, ob_scr = next(it), next(it)
    h0 = pl.program_id(1) * HPS
    n_chunks = seq_len // CHUNK
    srow = lax.broadcasted_iota(jnp.int32, (LANES, LANES), 0)
    streams = [(hh, d) for hh in range(HPS) for d in range(2)]

    def chain(c, hh, d, s):
        rows = pl.ds(pl.multiple_of(c * CHUNK, CHUNK), CHUNK)
        cols = slice(hh * LANES, (hh + 1) * LANES)
        ab = ab_ref[rows, :]
        sel_a = jnp.where(srow == DN_HEADS * d + h0 + hh, 1.0, 0.0).astype(BF16)
        sel_b = jnp.where(srow == 2 * DN_HEADS + DN_HEADS * d + h0 + hh, 1.0, 0.0).astype(BF16)
        a = _dot_x_sel(ab, sel_a)
        beta = jax.nn.sigmoid(_dot_x_sel(ab, sel_b))
        g = -jnp.exp(alog_ref[hh, d]) * _softplus(a + dtb_ref[hh, d])
        yield
        o, s_new = yield from _dn_chunk(q_ref[rows, cols], k_ref[rows, cols], v_ref[rows, cols],
                                        g, beta, s, d == 1)
        return rows, cols, o, s_new

    def body(c, carry):
        res = _lockstep([chain(n_chunks - 1 - c if d else c, hh, d, carry[i])
                         for i, (hh, d) in enumerate(streams)])
        for (hh, d), (rows, cols, o, _) in zip(streams, res):
            (ob_scr if d else of_scr)[rows, cols] = o
        return tuple(r[3] for r in res)

    if has_s0:
        init = tuple(s0_ref[0, d, hh] for hh, d in streams)
    else:
        init = tuple(jnp.zeros((LANES, LANES), F32) for _ in streams)
    fin = lax.fori_loop(0, n_chunks, body, init)
    for hh in range(HPS):
        cols = slice(hh * LANES, (hh + 1) * LANES)
        m_ref[:, cols] = _head_norm_gate(of_scr[:, cols] + ob_scr[:, cols], gn_ref[...],
                                         gate_ref[:, cols]).astype(m_ref.dtype)
    if with_state_out:
        for i, (hh, d) in enumerate(streams):
            so_ref[0, d, hh] = fin[i]


S5_TC = 128
S5_CB = 4
S5_SW = S5_G * S5_P // S5_CB


def _s5_params(lam_re, lam_im, log_dt, b_re, b_im, c_re, c_im):
    gpb = S5_G // S5_CB
    eye = jnp.eye(gpb, dtype=F32)
    n = jnp.arange(1, S5_TC + 1, dtype=F32)[:, None, None]
    bbs, ccs, aps = [], [], []
    for d in range(2):
        lr = jnp.minimum(lam_re[d].astype(F32), -1e-4)
        li = lam_im[d].astype(F32)
        dt = jnp.exp(log_dt[d].astype(F32))[:, None]
        mag = jnp.exp(lr * dt)
        ar, ai = mag * jnp.cos(li * dt), mag * jnp.sin(li * dt)
        den = lr * lr + li * li
        nr = ar - 1.0
        cr = (nr * lr + ai * li) / den
        ci = (ai * lr - nr * li) / den
        bbr = cr[..., None] * b_re - ci[..., None] * b_im
        bbi = cr[..., None] * b_im + ci[..., None] * b_re
        blk = lambda t: jnp.einsum('bgpc,gh->bgchp', t.reshape(S5_CB, gpb, S5_P, S5_GS), eye).reshape(
            S5_CB, gpb * S5_GS, gpb * S5_P)
        bbs.append(jnp.stack([blk(bbr), blk(bbi)]))
        cblk = lambda t: jnp.einsum('bgcp,gh->bgphc', t.reshape(S5_CB, gpb, S5_GS, S5_P), eye).reshape(
            S5_CB, gpb * S5_P, gpb * S5_GS)
        ccs.append(jnp.stack([cblk(c_re[d].astype(F32)), cblk(c_im[d].astype(F32))]))
        pm = jnp.exp(n * (lr * dt)[None])
        pr = (pm * jnp.cos(n * (li * dt)[None])).reshape(S5_TC, -1)
        pi = (pm * jnp.sin(n * (li * dt)[None])).reshape(S5_TC, -1)
        if d == 1:
            pr, pi = pr[::-1], pi[::-1]
        aps.append(jnp.stack([pr, pi]))
    bb = jnp.stack(bbs, axis=2).astype(BF16)
    cc = jnp.stack(ccs, axis=2).astype(BF16)
    apow = jnp.stack(aps, axis=1)
    return bb, cc, apow


def _s5_scan(xr, xi, pr, pi, cr, ci, reverse, loc, loc0):
    tc, w = xr.shape
    g8 = SUBLANES
    ng = tc // g8

    def power(n):
        idx = tc - n if reverse else n - 1
        return pr[idx:idx + 1, :], pi[idx:idx + 1, :]

    def step(xr, xi, s, unit):
        n = xr.shape[0]
        row = lax.broadcasted_iota(jnp.int32, xr.shape, 0)
        period = g8 if unit == 1 else n
        pos = jnp.bitwise_and(row, period - 1) if unit == 1 else row
        keep = (pos < period - s) if reverse else (pos >= s)
        shift = n - s if reverse else s
        a_r, a_i = power(s * unit)
        if unit == 1:
            rot = lambda x: pltpu.roll(x.reshape(n // g8, g8, w), g8 - s if reverse else s, 1).reshape(n, w)
            row8 = lax.broadcasted_iota(jnp.int32, (g8, w), 0)
            keep8 = (row8 < g8 - s) if reverse else (row8 >= s)
            a_r = jnp.concatenate([jnp.where(keep8, a_r, 0.0)] * (n // g8), axis=0)
            a_i = jnp.concatenate([jnp.where(keep8, a_i, 0.0)] * (n // g8), axis=0)
            sr, si = rot(xr), rot(xi)
        else:
            sr = jnp.where(keep, pltpu.roll(xr, shift, 0), 0.0)
            si = jnp.where(keep, pltpu.roll(xi, shift, 0), 0.0)
        return xr + (a_r * sr - a_i * si), xi + (a_r * si + a_i * sr)

    s = 1
    while s < g8:
        xr, xi = step(xr, xi, s, 1)
        s *= 2
    end = 0 if reverse else g8 - 1
    ers, eis = [], []
    for k in range(w // LANES):
        r0, i0 = (loc0 + k) * tc, (loc0 + w // LANES + k) * tc
        loc[r0:r0 + tc, :] = xr[:, k * LANES:(k + 1) * LANES]
        loc[i0:i0 + tc, :] = xi[:, k * LANES:(k + 1) * LANES]
        ers.append(loc[pl.ds(r0 + end, ng, stride=g8), :])
        eis.append(loc[pl.ds(i0 + end, ng, stride=g8), :])
    er = jnp.concatenate(ers, axis=1)
    ei = jnp.concatenate(eis, axis=1)
    s = 1
    while s < ng:
        er, ei = step(er, ei, s, g8)
        s *= 2
    grow = lax.broadcasted_iota(jnp.int32, (ng, w), 0)
    if reverse:
        pin_r = jnp.concatenate([pr[g8 * (g + 1):g8 * (g + 1) + 1] for g in range(ng - 1)] + [jnp.ones((1, w), F32)], 0)
        pin_i = jnp.concatenate([pi[g8 * (g + 1):g8 * (g + 1) + 1] for g in range(ng - 1)] + [jnp.zeros((1, w), F32)], 0)
        prev_r = jnp.where(grow < ng - 1, pltpu.roll(er, ng - 1, 0), 0.0)
        prev_i = jnp.where(grow < ng - 1, pltpu.roll(ei, ng - 1, 0), 0.0)
    else:
        pin_r = jnp.concatenate([jnp.ones((1, w), F32)] + [pr[g8 * g - 1:g8 * g] for g in range(1, ng)], 0)
        pin_i = jnp.concatenate([jnp.zeros((1, w), F32)] + [pi[g8 * g - 1:g8 * g] for g in range(1, ng)], 0)
        prev_r = jnp.where(grow >= 1, pltpu.roll(er, 1, 0), 0.0)
        prev_i = jnp.where(grow >= 1, pltpu.roll(ei, 1, 0), 0.0)
    in_r = prev_r + (pin_r * cr - pin_i * ci)
    in_i = prev_i + (pin_r * ci + pin_i * cr)
    in_r = jnp.concatenate([jnp.broadcast_to(in_r[g:g + 1], (g8, w)) for g in range(ng)], axis=0)
    in_i = jnp.concatenate([jnp.broadcast_to(in_i[g:g + 1], (g8, w)) for g in range(ng)], axis=0)
    p8r = pr[tc - g8:] if reverse else pr[:g8]
    p8i = pi[tc - g8:] if reverse else pi[:g8]
    a_r = jnp.concatenate([p8r] * ng, axis=0)
    a_i = jnp.concatenate([p8i] * ng, axis=0)
    return xr + (a_r * in_r - a_i * in_i), xi + (a_r * in_i + a_i * in_r)


def _s5_kernel(*refs, seq_len, has_h0):
    it = iter(refs)
    u_ref, bb_ref, cc_ref, ap_ref, dsk_ref = (next(it) for _ in range(5))
    h0r_ref, h0i_ref = (next(it), next(it)) if has_h0 else (None, None)
    y_ref, fr_ref, fi_ref = next(it), next(it), next(it)
    yf_scr, yb_scr, loc_scr = next(it), next(it), next(it)
    n_chunks = seq_len // S5_TC

    def run(c, d, carry):
        cr, ci = carry
        rows = pl.ds(pl.multiple_of(c * S5_TC, S5_TC), S5_TC)
        ub = u_ref[rows, :].astype(BF16)
        pr, pi = ap_ref[0, d], ap_ref[1, d]
        xr, xi = _dot(ub, bb_ref[0, 0, d]), _dot(ub, bb_ref[1, 0, d])
        hr, hi = _s5_scan(xr, xi, pr, pi, cr, ci, d == 1, loc_scr, d * 2 * (S5_SW // LANES))
        (yf_scr, yb_scr)[d][rows, :] = (_dot(hr.astype(BF16), cc_ref[0, 0, d])
                                         - _dot(hi.astype(BF16), cc_ref[1, 0, d]))
        last = 0 if d == 1 else S5_TC - 1
        return hr[last:last + 1, :], hi[last:last + 1, :]

    def body(c, carry):
        cf, cb = carry
        return run(c, 0, cf), run(n_chunks - 1 - c, 1, cb)

    if has_h0:
        init = tuple((h0r_ref[0, d:d + 1, :], h0i_ref[0, d:d + 1, :]) for d in range(2))
    else:
        z = jnp.zeros((1, S5_SW), F32)
        init = ((z, z), (z, z))
    fin = lax.fori_loop(0, n_chunks, body, init)
    y_ref[...] = dsk_ref[...] * u_ref[...] + (yf_scr[...] + yb_scr[...])
    for d in range(2):
        fr_ref[0, d:d + 1, :] = fin[d][0]
        fi_ref[0, d:d + 1, :] = fin[d][1]


def _s5_call(z, col0, bb, cc, apow, dsk, h0r, h0i, seq_len, n_seq, row0):
    blk0 = row0 // seq_len
    has_h0 = h0r is not None
    st_spec = pl.BlockSpec((1, 2, S5_SW), lambda s, c: (s, 0, c))
    in_specs = [pl.BlockSpec((seq_len, LANES), lambda s, c: (blk0 + s, col0 + c)),
                pl.BlockSpec((2, 1, 2, LANES, S5_SW), lambda s, c: (0, c, 0, 0, 0)),
                pl.BlockSpec((2, 1, 2, S5_SW, LANES), lambda s, c: (0, c, 0, 0, 0)),
                pl.BlockSpec((2, 2, S5_TC, S5_SW), lambda s, c: (0, 0, 0, c)),
                pl.BlockSpec((1, LANES), lambda s, c: (0, c))]
    args = [z, bb, cc, apow, dsk]
    if has_h0:
        in_specs += [st_spec, st_spec]
        args += [h0r, h0i]
    st_shape = jax.ShapeDtypeStruct((n_seq, 2, S5_G * S5_P), F32)
    return pl.pallas_call(
        functools.partial(_s5_kernel, seq_len=seq_len, has_h0=has_h0),
        grid=(n_seq, S5_CB),
        in_specs=in_specs,
        out_specs=[pl.BlockSpec((seq_len, LANES), lambda s, c: (s, c)), st_spec, st_spec],
        out_shape=[jax.ShapeDtypeStruct((n_seq * seq_len, MIX_W), F32), st_shape, st_shape],
        scratch_shapes=[pltpu.VMEM((seq_len, LANES), F32), pltpu.VMEM((seq_len, LANES), F32),
                        pltpu.VMEM((4 * (S5_SW // LANES) * S5_TC, LANES), F32)],
        compiler_params=_cparams(("parallel", "parallel")),
        name=f"mix_s5_{seq_len}",
    )(*args)


def _glu_kernel(y_ref, w_ref, b_ref, o_ref):
    zz = jax.nn.gelu(y_ref[...])
    o_ref[...] = (zz * jax.nn.sigmoid(_dot(zz.astype(BF16), w_ref[...]) + b_ref[...])).astype(o_ref.dtype)


def _glu_call(y, w, b):
    t = y.shape[0]
    return pl.pallas_call(
        _glu_kernel,
        grid=(t // TM,),
        in_specs=[pl.BlockSpec((TM, MIX_W), lambda i: (i, 0)),
                  pl.BlockSpec((MIX_W, MIX_W), lambda i: (0, 0)),
                  pl.BlockSpec((1, MIX_W), lambda i: (0, 0))],
        out_specs=pl.BlockSpec((TM, MIX_W), lambda i: (i, 0)),
        out_shape=jax.ShapeDtypeStruct((t, MIX_W), BF16),
        compiler_params=_cparams(("parallel",)),
        name="s5_glu",
    )(y, w, b)


PK_HP = 2 * PK_HEADS
PK_SIDE = PK_KEYS
PK_PICKS = PK_HEADS * PK_TOPK
TK_TM = 128
TK_HG = 2
WB_TM = 128
WB_GRP = 16
WB_STRIDE = PK_SIDE + SUBLANES
PM_TM = 1024
PM_SLABS = 8


def _score_kernel(h_ref, wq_ref, keys_ref, s_ref):
    q = _dot(h_ref[...], wq_ref[...])
    for hp in range(PK_HP):
        s_ref[hp] = _dot_nt(keys_ref[hp], q[:, hp * LANES:(hp + 1) * LANES].astype(BF16))


def _score_call(h, wq, keys):
    return pl.pallas_call(
        _score_kernel,
        grid=(h.shape[0] // TM,),
        in_specs=[pl.BlockSpec((TM, D_MODEL), lambda i: (i, 0)),
                  pl.BlockSpec((D_MODEL, PK_HP * LANES), lambda i: (0, 0)),
                  pl.BlockSpec((PK_HP, PK_KEYS, LANES), lambda i: (0, 0, 0))],
        out_specs=pl.BlockSpec((PK_HP, PK_KEYS, TM), lambda i: (0, 0, i)),
        out_shape=jax.ShapeDtypeStruct((PK_HP, PK_KEYS, h.shape[0]), F32),
        compiler_params=_cparams(("parallel",)),
        name="peer_scores",
    )(h, wq, keys)


def _top16_sorted(s):
    n_grp = s.shape[0] // SUBLANES
    n = s.shape[1]
    base = lax.broadcasted_iota(jnp.int32, (SUBLANES, n), 0).astype(F32)
    vals = [s[SUBLANES * g:SUBLANES * (g + 1), :] for g in range(n_grp)]
    keys = [base + float(SUBLANES * g) for g in range(n_grp)]
    for rnd in range(n_grp):
        for k in range(rnd % 2, n_grp - 1, 2):
            a, b, ka, kb = vals[k], vals[k + 1], keys[k], keys[k + 1]
            swap = b > a
            vals[k], vals[k + 1] = jnp.maximum(a, b), jnp.minimum(a, b)
            keys[k], keys[k + 1] = jnp.where(swap, kb, ka), jnp.where(swap, ka, kb)
    rank = lax.broadcasted_iota(jnp.int32, (PK_TOPK, n), 0)
    top_v = jnp.zeros((PK_TOPK, n), F32)
    top_k = jnp.zeros((PK_TOPK, n), F32)
    for r in range(PK_TOPK):
        m = jnp.max(vals[0], axis=0, keepdims=True)
        am = jnp.min(jnp.where(vals[0] == m, keys[0], float(s.shape[0])), axis=0, keepdims=True)
        top_v = jnp.where(rank == r, m, top_v)
        top_k = jnp.where(rank == r, am, top_k)
        hit = keys[0] == am
        for k in range(min(n_grp - 1, PK_TOPK - 1 - r)):
            vals[k] = jnp.where(hit, vals[k + 1], vals[k])
            keys[k] = jnp.where(hit, keys[k + 1], keys[k])
    return top_v, top_k


def _top16_rows(tiles):
    shape = tiles[0].shape
    row = lax.broadcasted_iota(jnp.int32, shape, 0).astype(F32)
    out_row = lax.broadcasted_iota(jnp.int32, (PK_TOPK, shape[1]), 0)

    def take(r, s, vals, idxs):
        m = jnp.max(s, axis=0, keepdims=True)
        am = jnp.min(jnp.where(s == m, row, float(shape[0])), axis=0, keepdims=True)
        at = out_row == r
        return jnp.where(row == am, -jnp.inf, s), jnp.where(at, m, vals), jnp.where(at, am, idxs)

    def body(r, carry):
        out = ()
        for i in range(len(tiles)):
            out += take(r, *carry[3 * i:3 * i + 3])
        return out

    zero = jnp.zeros((PK_TOPK, shape[1]), F32)
    init = ()
    for s in tiles:
        init += (s, zero, zero)
    res = lax.fori_loop(0, PK_TOPK, body, init)
    return [(res[3 * i + 1], res[3 * i + 2]) for i in range(len(tiles))]


def _top16_pairs(v1, a1, v2, a2):
    half = PK_TOPK // 2
    n = v1.shape[1]
    rank = lax.broadcasted_iota(jnp.int32, (PK_TOPK, n), 0)
    rank_f = rank.astype(F32)
    vals = [v1 + jnp.broadcast_to(v2[0:1], (PK_TOPK, n))]
    vals += [v1[:half] + jnp.broadcast_to(v2[b:b + 1], (half, n)) for b in range(1, PK_TOPK)]
    ids = [rank_f * PK_TOPK] + [rank_f[:half] * PK_TOPK + float(b) for b in range(1, PK_TOPK)]
    best = jnp.zeros((PK_TOPK, n), F32)
    p1 = jnp.zeros((PK_TOPK, n), F32)
    p2 = jnp.zeros((PK_TOPK, n), F32)
    for r in range(PK_TOPK):
        m = jnp.max(vals[0], axis=0, keepdims=True)
        pos = jnp.min(jnp.where(vals[0] == m, ids[0], float(PK_TOPK * PK_TOPK)), axis=0, keepdims=True)
        hit = ids[0] == pos
        a_sel = jnp.max(jnp.where(hit, rank_f, -1.0), axis=0, keepdims=True)
        e1 = jnp.max(jnp.where(hit, a1, -1.0), axis=0, keepdims=True)
        e2 = jnp.max(jnp.where(rank_f == pos - a_sel * PK_TOPK, a2, -1.0), axis=0, keepdims=True)
        best = jnp.where(rank == r, m, best)
        p1 = jnp.where(rank == r, e1, p1)
        p2 = jnp.where(rank == r, e2, p2)
        if r + 1 < PK_TOPK:
            hit_lo = hit[:half]
            vals[0] = jnp.concatenate([jnp.where(hit_lo, vals[1], vals[0][:half]),
                                       jnp.where(hit[half:], -jnp.inf, vals[0][half:])], axis=0)
            ids[0] = jnp.concatenate([jnp.where(hit_lo, ids[1], ids[0][:half]), ids[0][half:]], axis=0)
            for k in range(1, PK_TOPK - 1 - r):
                vals[k] = jnp.where(hit_lo, vals[k + 1], vals[k])
                ids[k] = jnp.where(hit_lo, ids[k + 1], ids[k])
    return best, p1, p2


def _topk_kernel(s_ref, i1_ref, i2_ref, g_ref, p1_scr, p2_scr, g_scr):
    def heads(hg, carry):
        tops = [_top16_sorted(s_ref[2 * TK_HG * hg + i]) for i in range(2 * TK_HG)]
        for j in range(TK_HG):
            (v1, a1), (v2, a2) = tops[2 * j], tops[2 * j + 1]
            best, p1, p2 = _top16_pairs(v1, a1, v2, a2)
            ex = jnp.exp(best - jnp.max(best, axis=0, keepdims=True))
            rows = pl.ds(pl.multiple_of((hg * TK_HG + j) * PK_TOPK, PK_TOPK), PK_TOPK)
            p1_scr[rows, :] = p1
            p2_scr[rows, :] = p2
            g_scr[rows, :] = ex / jnp.sum(ex, axis=0, keepdims=True)
        return carry

    lax.fori_loop(0, PK_HEADS // TK_HG, heads, 0)
    i1_ref[...] = p1_scr[...].T
    i2_ref[...] = p2_scr[...].T
    g_ref[...] = g_scr[...].T


def _topk_call(s):
    t = s.shape[2]
    spec = pl.BlockSpec((TK_TM, PK_PICKS), lambda i: (i, 0))
    shp = jax.ShapeDtypeStruct((t, PK_PICKS), F32)
    return pl.pallas_call(
        _topk_kernel,
        grid=(t // TK_TM,),
        in_specs=[pl.BlockSpec((PK_HP, PK_KEYS, TK_TM), lambda i: (0, 0, i))],
        out_specs=[spec, spec, spec],
        out_shape=[shp, shp, shp],
        scratch_shapes=[pltpu.VMEM((PK_PICKS, TK_TM), F32)] * 3,
        compiler_params=_cparams(("parallel",)),
        name="peer_topk",
    )(s)


def _wbuild_kernel(i1_ref, i2_ref, g_ref, w_ref, stage_a, stage_b):
    sub = lax.broadcasted_iota(jnp.int32, (PK_SIDE, PK_PICKS), 0).astype(F32).astype(BF16)
    one = jnp.ones((PK_SIDE, PK_PICKS), BF16)
    zero = jnp.zeros((PK_SIDE, PK_PICKS), BF16)

    def grids(t0, stage):
        for tt in range(WB_GRP):
            r1 = i1_ref[pl.ds(t0 + tt, 1), :].astype(BF16)
            r2 = i2_ref[pl.ds(t0 + tt, 1), :].astype(BF16)
            rg = g_ref[pl.ds(t0 + tt, 1), :].astype(BF16)
            p1t = jnp.where(sub == jnp.broadcast_to(r1, sub.shape), one, zero)
            q2t = jnp.where(sub == jnp.broadcast_to(r2, sub.shape), jnp.broadcast_to(rg, sub.shape), zero)
            stage[tt * WB_STRIDE:tt * WB_STRIDE + PK_SIDE, :] = _dot_nt(p1t, q2t)

    def regroup(t0, stage):
        for i1 in range(PK_SIDE):
            lo = stage[pl.ds(i1, SUBLANES, stride=WB_STRIDE), :]
            hi = stage[pl.ds(i1 + SUBLANES * WB_STRIDE, SUBLANES, stride=WB_STRIDE), :]
            w_ref[0, i1, pl.ds(t0, WB_GRP), :] = jnp.concatenate([lo, hi], axis=0).astype(BF16)

    def pair(gi, carry):
        t_a = pl.multiple_of(gi * 2 * WB_GRP, WB_GRP)
        t_b = pl.multiple_of(gi * 2 * WB_GRP + WB_GRP, WB_GRP)
        grids(t_a, stage_a)
        grids(t_b, stage_b)
        regroup(t_a, stage_a)
        regroup(t_b, stage_b)
        return carry

    lax.fori_loop(0, WB_TM // (2 * WB_GRP), pair, 0)


def _wbuild_call(i1, i2, g):
    spec = pl.BlockSpec((WB_TM, PK_PICKS), lambda i: (i, 0))
    return pl.pallas_call(
        _wbuild_kernel,
        grid=(i1.shape[0] // WB_TM,),
        in_specs=[spec, spec, spec],
        out_specs=pl.BlockSpec((1, PK_SIDE, WB_TM, PK_SIDE), lambda i: (i, 0, 0, 0)),
        out_shape=jax.ShapeDtypeStruct((i1.shape[0] // WB_TM, PK_SIDE, WB_TM, PK_SIDE), BF16),
        scratch_shapes=[pltpu.VMEM((WB_GRP * WB_STRIDE, PK_SIDE), F32)] * 2,
        compiler_params=_cparams(("parallel",)),
        name="peer_route",
    )(i1, i2, g)


def _peer_kernel(h_ref, u_ref, v_ref, w_ref, x_ref, g2_ref, o_ref, acc):
    j = pl.program_id(1)

    @pl.when(j == 0)
    def _():
        acc[...] = jnp.zeros_like(acc)

    a = _dot_nt(h_ref[...], u_ref[0].astype(BF16))
    nsub = PM_TM // WB_TM
    w = jnp.concatenate(
        [jnp.concatenate([w_ref[n, sl] for sl in range(PM_SLABS)], axis=1) for n in range(nsub)], axis=0)
    acc[...] += _dot((jax.nn.gelu(a) * w.astype(F32)).astype(BF16), v_ref[0].astype(BF16))

    @pl.when(j == pl.num_programs(1) - 1)
    def _():
        o_ref[...] = x_ref[...] + g2_ref[0] * acc[...]


def _peer_call(h16, u_tabs, v_tabs, layer, w, x, mod3):
    ne = PM_SLABS * PK_SIDE
    tok = pl.BlockSpec((PM_TM, D_MODEL), lambda i, j: (i, 0))
    return pl.pallas_call(
        _peer_kernel,
        grid=(x.shape[0] // PM_TM, PK_SIDE // PM_SLABS),
        in_specs=[tok,
                  pl.BlockSpec((1, ne, D_MODEL), lambda i, j: (layer, j, 0)),
                  pl.BlockSpec((1, ne, D_MODEL), lambda i, j: (layer, j, 0)),
                  pl.BlockSpec((PM_TM // WB_TM, PM_SLABS, WB_TM, PK_SIDE), lambda i, j: (i, j, 0, 0)),
                  tok,
                  pl.BlockSpec((1, 1, D_MODEL), lambda i, j: (_cond_row(i, PM_TM) * 6 + 5, 0, 0))],
        out_specs=tok,
        out_shape=jax.ShapeDtypeStruct(x.shape, F32),
        scratch_shapes=[pltpu.VMEM((PM_TM, D_MODEL), F32)],
        compiler_params=_cparams(("parallel", "arbitrary")),
        name="peer_experts",
    )(h16, u_tabs, v_tabs, w, x, mod3)


def _final_norm_kernel(x_ref, g_ref, o_ref):
    x = x_ref[...]
    o_ref[...] = x * lax.rsqrt(jnp.mean(x * x, axis=-1, keepdims=True) + NORM_EPS) * g_ref[...]


def _final_norm_call(x, g):
    spec = pl.BlockSpec((TM, D_MODEL), lambda i: (i, 0))
    return pl.pallas_call(
        _final_norm_kernel,
        grid=(x.shape[0] // TM,),
        in_specs=[spec, pl.BlockSpec((1, D_MODEL), lambda i: (0, 0))],
        out_specs=spec,
        out_shape=jax.ShapeDtypeStruct(x.shape, F32),
        compiler_params=_cparams(("parallel",)),
        name="final_norm",
    )(x, g)


def _dn_params(conv_w, a_log, dt_bias, dn_norm):
    bc = lambda p: jnp.broadcast_to(p.astype(F32).T[:, :, None, None], (DN_HEADS, 2, 1, LANES))
    return {"conv": conv_w.astype(F32), "alog": bc(a_log), "dtb": bc(dt_bias),
            "gn": dn_norm.astype(F32).reshape(1, LANES)}


def _dn_call(z, qkv, col, alog, dtb, gn, s0, seq_len, n_seq, row0, with_state_out):
    blk0 = row0 // seq_len
    hd = DN_HEADS
    nhb = hd // HPS
    wide = HPS * LANES
    gate0 = col["gate"] // HPS
    one = pl.Buffered(1)
    in_specs = [pl.BlockSpec((seq_len, wide), lambda s, h: (s, h), pipeline_mode=one),
                pl.BlockSpec((seq_len, wide), lambda s, h: (s, nhb + h), pipeline_mode=one),
                pl.BlockSpec((seq_len, wide), lambda s, h: (s, 2 * nhb + h), pipeline_mode=one),
                pl.BlockSpec((seq_len, LANES), lambda s, h: (blk0 + s, col["ab"])),
                pl.BlockSpec((seq_len, wide), lambda s, h: (blk0 + s, gate0 + h), pipeline_mode=one),
                pl.BlockSpec((HPS, 2, 1, LANES), lambda s, h: (h, 0, 0, 0)),
                pl.BlockSpec((HPS, 2, 1, LANES), lambda s, h: (h, 0, 0, 0)),
                pl.BlockSpec((1, LANES), lambda s, h: (0, 0))]
    args = [qkv, qkv, qkv, z, z, alog, dtb, gn]
    st_spec = pl.BlockSpec((1, 2, HPS, LANES, LANES), lambda s, h: (s, 0, h, 0, 0))
    has_s0 = s0 is not None
    if has_s0:
        in_specs.append(st_spec)
        args.append(s0)
    out_specs = [pl.BlockSpec((seq_len, wide), lambda s, h: (s, h))]
    out_shape = [jax.ShapeDtypeStruct((n_seq * seq_len, hd * LANES), BF16)]
    if with_state_out:
        out_specs.append(st_spec)
        out_shape.append(jax.ShapeDtypeStruct((n_seq, 2, hd, LANES, LANES), F32))
    return pl.pallas_call(
        functools.partial(_dn_kernel, seq_len=seq_len, with_state_out=with_state_out, has_s0=has_s0),
        grid=(n_seq, nhb),
        in_specs=in_specs,
        out_specs=out_specs,
        out_shape=out_shape,
        scratch_shapes=[pltpu.VMEM((seq_len, wide), F32), pltpu.VMEM((seq_len, wide), F32)],
        compiler_params=_cparams(("parallel", "parallel")),
        name=f"mix_dn_{seq_len}",
    )(*args)


def _grid_col_major(t):
    n, d = t.shape
    rows = DEC_SEQ // GRID_W
    return t.reshape(DEC_BATCH, rows, GRID_W, d).transpose(0, 2, 1, 3).reshape(n, d)


def _grid_row_major(t):
    n, d = t.shape
    rows = DEC_SEQ // GRID_W
    return t.reshape(DEC_BATCH, GRID_W, rows, d).transpose(0, 2, 1, 3).reshape(n, d)


def _both_groups(fn, s0_sample):
    res_p = fn(None, SEQ, BATCH, 0, True)
    res_s = fn(s0_sample, DEC_SEQ, DEC_BATCH, T_PROMPT, False)
    return (res_p[0], res_s[0]), res_p[1]


def kernel(x_prompt, x_sample, state_gla, state_hgrn, state_s5_re, state_s5_im, state_dn, c, c_ctx, w_ada, b_ada, norm1, norm2, w_in_even, w_gla_gate, b_gla_gate, gla_norm, hg_lb, hg_norm, w_in_odd, s5_lam_re, s5_lam_im, s5_log_dt, s5_b_re, s5_b_im, s5_c_re, s5_c_im, s5_d, s5_w_glu, s5_b_glu, dn_conv, dn_a_log, dn_dt_bias, dn_norm, w_out, pk_w_q, pk_keys, pk_u, pk_v, final_norm):
    x = jnp.concatenate([x_prompt.reshape(T_PROMPT, D_MODEL), x_sample.reshape(T_SAMPLE, D_MODEL)], axis=0)
    cond = jnp.zeros((N_COND, D_MODEL), F32).at[0].set(c_ctx.astype(F32)).at[1:1 + DEC_BATCH].set(c.astype(F32))
    mod = _ada_call(cond, w_ada.astype(F32), b_ada.astype(F32))

    out_g, out_h, out_re, out_im, out_dn = [], [], [], [], []
    for l in range(DEPTH):
        mod3 = mod[l].reshape(N_COND * 6, 1, D_MODEL)
        g1 = norm1[l].astype(F32).reshape(1, D_MODEL)
        if l % 2 == 0:
            e = l // 2
            w_in = _relayout_cols(w_in_even[e], EVEN_IDX).astype(BF16)
            z = _in_call(x, None, mod3, g1, w_in, 1)
            prm = _even_params(w_gla_gate[e], b_gla_gate[e], gla_norm[e], hg_lb, hg_norm[e], e)
            s0g = jnp.pad(state_gla[:, e].astype(F32), ((0, 0),) * 3 + ((0, HG_DK - GLA_DK), (0, 0)))
            m_g, st_g = _both_groups(
                lambda s0, L, n, r0, so: _even_call(z, "gla", EVEN_COL, prm["gla"], s0, L, n, r0, so), s0g)
            m_h, st_h = _both_groups(
                lambda s0, L, n, r0, so: _even_call(z, "hgrn", EVEN_COL, prm["hgrn"], s0, L, n, r0, so),
                state_hgrn[:, e].astype(F32))
            m_a, m_b = m_g, m_h
            out_g.append(st_g[..., :GLA_DK, :])
            out_h.append(st_h)
        else:
            j = l // 2
            w_in = _relayout_cols(w_in_odd[j], ODD_IDX).astype(BF16)
            z = _in_call(x, _grid_col_major(x[T_PROMPT:]), mod3, g1, w_in, 1)
            bb, cc, apow = _s5_params(s5_lam_re[j], s5_lam_im[j], s5_log_dt[j], s5_b_re[j], s5_b_im[j],
                                      s5_c_re[j], s5_c_im[j])
            dsk = s5_d[j].astype(F32).reshape(1, MIX_W)
            y_p, fr, fi = _s5_call(z, ODD_COL["u"], bb, cc, apow, dsk, None, None, SEQ, BATCH, 0)
            y_s, _, _ = _s5_call(z, ODD_COL["u"], bb, cc, apow, dsk,
                                 state_s5_re[:, j].astype(F32).reshape(DEC_BATCH, 2, -1),
                                 state_s5_im[:, j].astype(F32).reshape(DEC_BATCH, 2, -1),
                                 DEC_SEQ, DEC_BATCH, T_PROMPT)
            w_glu, b_glu = s5_w_glu[j].astype(BF16), s5_b_glu[j].astype(F32).reshape(1, MIX_W)
            m_s5 = (_glu_call(y_p, w_glu, b_glu), _glu_call(y_s, w_glu, b_glu))
            dp = _dn_params(dn_conv[j], dn_a_log[j], dn_dt_bias[j], dn_norm[j])

            def dn(s0, L, n, r0, so):
                qkv = _conv_call(z, ODD_COL["qkv"], dp["conv"], L, n, r0)
                return _dn_call(z, qkv, ODD_COL, dp["alog"], dp["dtb"], dp["gn"], s0, L, n, r0, so)

            m_dn, st_dn = _both_groups(dn, state_dn[:, j].astype(F32))
            m_a = (m_s5[0], _grid_row_major(m_s5[1]))
            m_b = (m_dn[0], _grid_row_major(m_dn[1]))
            out_re.append(fr.reshape(BATCH, 2, S5_G, S5_P))
            out_im.append(fi.reshape(BATCH, 2, S5_G, S5_P))
            out_dn.append(st_dn)
        x, h2 = _out_call(x, m_a, m_b, w_out[l].astype(BF16), mod3, norm2[l].astype(F32).reshape(1, D_MODEL))
        s = _score_call(h2, pk_w_q[l].astype(BF16), pk_keys[l].reshape(PK_HP, PK_KEYS, PK_DQ // 2).astype(BF16))
        i1, i2, gate = _topk_call(s)
        w = _wbuild_call(i1, i2, gate)
        x = _peer_call(h2, pk_u, pk_v, l, w, x, mod3)
    y = _final_norm_call(x, final_norm.astype(F32).reshape(1, D_MODEL))
    return (y[:T_PROMPT].reshape(BATCH, SEQ, D_MODEL), y[T_PROMPT:].reshape(DEC_BATCH, DEC_SEQ, D_MODEL),
            jnp.stack(out_g, axis=1), jnp.stack(out_h, axis=1), jnp.stack(out_re, axis=1),
            jnp.stack(out_im, axis=1), jnp.stack(out_dn, axis=1))
```
